```python
import jax, jax.numpy as jnp
from jax import lax
import numpy as np

D_MODEL = 1024
BATCH = 8
SEQ = 2048
DEPTH = 2
DEC_BATCH = 128
DEC_SEQ = 1
PAST_LEN = 16384
PAGE_SIZE = 128

MIX_WIDTH = D_MODEL
MLSTM_WIDTH = MIX_WIDTH // 2
POOL_WIDTH = MIX_WIDTH - MLSTM_WIDTH
MLSTM_HEADS = 4
HEAD_DIM = MLSTM_WIDTH // MLSTM_HEADS
MLSTM_CHUNK = 64
POOL_WINDOWS = (2, 4, 8, 16)
POOL_GROUPS = len(POOL_WINDOWS)
POOL_GROUP = POOL_WIDTH // POOL_GROUPS
POOL_BUF = max(POOL_WINDOWS) - 1
PROJ_WIDTH = 4 * MLSTM_WIDTH + POOL_WIDTH + 2 * MLSTM_HEADS
PEER_HEADS = 8
PEER_NKEYS = 128
PEER_EXPERTS = PEER_NKEYS * PEER_NKEYS
PEER_QDIM = 256
PEER_HALF = PEER_QDIM // 2
PEER_TOPK = 16
PEER_BLOCK = 128
PLE_DIM = 256
EPS = 1e-6

kernel_name = "hymba_mlstm_pool_peer_step"


def rmsnorm(x, g):
    xf = x.astype(jnp.float32)
    y = xf * lax.rsqrt(jnp.mean(xf * xf, axis=-1, keepdims=True) + EPS)
    return (y * g.astype(jnp.float32)).astype(x.dtype)


def mlstm_chunk(carry, xs):
    C, n, m = carry
    q, k, v, ig, lf = xs
    L = q.shape[2]
    b = jnp.cumsum(lf, axis=-1)
    causal = jnp.tril(jnp.ones((L, L), dtype=bool))
    dlog = jnp.where(causal, b[..., :, None] - b[..., None, :] + ig[..., None, :], -jnp.inf)
    inter = b + m[..., None]
    m_t = jnp.maximum(jnp.max(dlog, axis=-1), inter)
    s = jnp.einsum('bhtd,bhsd->bhts', q, k) * jnp.exp(dlog - m_t[..., None])
    w_inter = jnp.exp(inter - m_t)
    num = jnp.einsum('bhts,bhsd->bhtd', s, v) + w_inter[..., None] * jnp.einsum('bhtd,bhde->bhte', q, C)
    den = jnp.sum(s, axis=-1) + w_inter * jnp.einsum('bhtd,bhd->bht', q, n)
    h = num / jnp.maximum(jnp.abs(den), jnp.exp(-m_t))[..., None]
    b_last = b[..., -1]
    wlog = b_last[..., None] - b + ig
    m_new = jnp.maximum(b_last + m, jnp.max(wlog, axis=-1))
    decay = jnp.exp(b_last + m - m_new)
    ws = jnp.exp(wlog - m_new[..., None])
    C_new = decay[..., None, None] * C + jnp.einsum('bhs,bhsd,bhse->bhde', ws, k, v)
    n_new = decay[..., None] * n + jnp.einsum('bhs,bhsd->bhd', ws, k)
    return (C_new, n_new, m_new), h


def mlstm_mixer(q, k, v, ig, lf, C, n, m):
    q, k, v = (jnp.swapaxes(a, 1, 2) for a in (q, k, v))
    ig, lf = (jnp.swapaxes(a, 1, 2) for a in (ig, lf))
    B, H, T, d = q.shape
    carry = (C.astype(jnp.float32), n.astype(jnp.float32), m.astype(jnp.float32))
    if T > MLSTM_CHUNK and T % MLSTM_CHUNK == 0:
        nc = T // MLSTM_CHUNK

        def to_chunks(a):
            a = a.reshape(a.shape[:2] + (nc, MLSTM_CHUNK) + a.shape[3:])
            return jnp.moveaxis(a, 2, 0)

        carry, h = lax.scan(mlstm_chunk, carry, tuple(to_chunks(a) for a in (q, k, v, ig, lf)))
        h = jnp.moveaxis(h, 0, 2).reshape(B, H, T, d)
    else:
        carry, h = mlstm_chunk(carry, (q, k, v, ig, lf))
    return jnp.swapaxes(h, 1, 2), carry


def pool_mixer(u, buf, start, w_pool, pool_scale):
    B, T, P = u.shape
    ext = jnp.concatenate([buf.astype(jnp.float32), u], axis=1)
    cs = jnp.concatenate([jnp.zeros((B, 1, P), jnp.float32), jnp.cumsum(ext, axis=1)], axis=1)
    pos = start + jnp.arange(T)
    outs = []
    for g, w in enumerate(POOL_WINDOWS):
        lo_c, hi_c = g * POOL_GROUP, (g + 1) * POOL_GROUP
        hi = cs[:, POOL_BUF + 1:POOL_BUF + 1 + T, lo_c:hi_c]
        lo = cs[:, POOL_BUF + 1 - w:POOL_BUF + 1 - w + T, lo_c:hi_c]
        cnt = jnp.minimum(pos + 1, w).astype(jnp.float32)[None, :, None]
        outs.append((hi - lo) / cnt - u[:, :, lo_c:hi_c])
    r = jnp.stack(outs, axis=2)
    z = jnp.einsum('btgc,gcd->btgd', r, w_pool.astype(jnp.float32)).reshape(B, T, P)
    z = z * pool_scale.astype(jnp.float32)
    return z, ext[:, -POOL_BUF:]


def peer_ffn(xn, wq, sub_keys, u_tab, v_tab):
    B, T, D = xn.shape
    flat = xn.reshape(B * T, D)
    ntok = B * T
    pad = (-ntok) % PEER_BLOCK
    flat = jnp.pad(flat, ((0, pad), (0, 0)))
    blocks = flat.reshape(-1, PEER_BLOCK, D)
    keys = sub_keys.astype(jnp.float32)

    def block_fn(xb):
        q = (xb @ wq).astype(jnp.float32).reshape(PEER_BLOCK, PEER_HEADS, 2, PEER_HALF)
        s = jnp.einsum('thcd,chkd->thck', q, keys)
        sv, si = lax.top_k(s, PEER_TOPK)
        cand = (sv[:, :, 0, :, None] + sv[:, :, 1, None, :]).reshape(PEER_BLOCK, PEER_HEADS, PEER_TOPK * PEER_TOPK)
        cidx = (si[:, :, 0, :, None] * PEER_NKEYS + si[:, :, 1, None, :]).reshape(PEER_BLOCK, PEER_HEADS, PEER_TOPK * PEER_TOPK)
        top, sel = lax.top_k(cand, PEER_TOPK)
        eidx = jnp.take_along_axis(cidx, sel, axis=-1)
        gate = jax.nn.softmax(top, axis=-1)
        ue = u_tab[eidx]
        act = jax.nn.gelu(jnp.einsum('td,thkd->thk', xb, ue).astype(jnp.float32))
        ve = v_tab[eidx]
        return jnp.einsum('thk,thkd->td', (gate * act).astype(xb.dtype), ve)

    out = lax.map(block_fn, blocks).reshape(-1, D)[:ntok]
    return out.reshape(B, T, D)


def layer(x, p, C, n, m, buf, start, w_in, b_gate, mlstm_norm, w_pool, pool_scale, w_out,
          norm1, norm2, peer_wq, peer_keys, peer_u, peer_v, w_ple, w_gate):
    B, T, _ = x.shape
    h = rmsnorm(x, norm1)
    proj = h @ w_in
    W = MLSTM_WIDTH
    q = proj[..., 0:W].astype(jnp.float32).reshape(B, T, MLSTM_HEADS, HEAD_DIM)
    k = proj[..., W:2 * W].astype(jnp.float32).reshape(B, T, MLSTM_HEADS, HEAD_DIM) * (HEAD_DIM ** -0.5)
    v = proj[..., 2 * W:3 * W].astype(jnp.float32).reshape(B, T, MLSTM_HEADS, HEAD_DIM)
    o = jax.nn.sigmoid(proj[..., 3 * W:4 * W].astype(jnp.float32))
    u_pool = proj[..., 4 * W:4 * W + POOL_WIDTH].astype(jnp.float32)
    gates = proj[..., 4 * W + POOL_WIDTH:].astype(jnp.float32) + b_gate.astype(jnp.float32)
    ig = gates[..., :MLSTM_HEADS]
    lf = jax.nn.log_sigmoid(gates[..., MLSTM_HEADS:])
    hm, (C_new, n_new, m_new) = mlstm_mixer(q, k, v, ig, lf, C, n, m)
    hm = hm * lax.rsqrt(jnp.mean(hm * hm, axis=-1, keepdims=True) + EPS)
    hm = hm.reshape(B, T, MLSTM_WIDTH) * mlstm_norm.astype(jnp.float32) * o
    zp, buf_new = pool_mixer(u_pool, buf, start, w_pool, pool_scale)
    mix = jnp.concatenate([hm, zp], axis=-1).astype(x.dtype) @ w_out
    x = x + mix
    x = x + peer_ffn(rmsnorm(x, norm2), peer_wq, peer_keys, peer_u, peer_v)
    x = x + jax.nn.sigmoid(x @ w_gate) * (p.astype(x.dtype) @ w_ple)
    return x, C_new, n_new, m_new, buf_new


def run_trunk(x, p, C0, n0, m0, buf0, start, params, norm_f):
    Cs, ns, ms, bufs = [], [], [], []
    for l in range(DEPTH):
        lw = tuple(w[l] for w in params)
        x, C, n, m, buf = layer(x, p[l], C0[l], n0[l], m0[l], buf0[l], start, *lw)
        Cs.append(C); ns.append(n); ms.append(m); bufs.append(buf)
    y = rmsnorm(x, norm_f)
    return y, jnp.stack(Cs), jnp.stack(ns), jnp.stack(ms), jnp.stack(bufs)


def setup_inputs(seed: int = 0) -> dict:
    key = jax.random.key(seed)
    ks = jax.random.split(key, 24)
    f32 = jnp.float32
    nrm = lambda k, s, sc: jax.random.normal(k, s, f32) * sc
    b_gate = jnp.concatenate([nrm(ks[10], (DEPTH, MLSTM_HEADS), 0.1),
                              3.0 + nrm(ks[11], (DEPTH, MLSTM_HEADS), 0.1)], axis=-1)
    return {
        "x_prompt": nrm(ks[0], (BATCH, SEQ, D_MODEL), 1.0),
        "x_sample": nrm(ks[1], (DEC_BATCH, DEC_SEQ, D_MODEL), 1.0),
        "p_prompt": nrm(ks[2], (DEPTH, BATCH, SEQ, PLE_DIM), 1.0),
        "p_sample": nrm(ks[3], (DEPTH, DEC_BATCH, DEC_SEQ, PLE_DIM), 1.0),
        "state_C": nrm(ks[4], (DEPTH, DEC_BATCH, MLSTM_HEADS, HEAD_DIM, HEAD_DIM), 1.0),
        "state_n": nrm(ks[5], (DEPTH, DEC_BATCH, MLSTM_HEADS, HEAD_DIM), 1.0),
        "state_m": nrm(ks[6], (DEPTH, DEC_BATCH, MLSTM_HEADS), 0.5),
        "state_pool": nrm(ks[7], (DEPTH, DEC_BATCH, POOL_BUF, POOL_WIDTH), 1.0),
        "w_in": nrm(ks[8], (DEPTH, D_MODEL, PROJ_WIDTH), D_MODEL ** -0.5),
        "b_gate": b_gate,
        "mlstm_norm": 1.0 + nrm(ks[9], (DEPTH, MLSTM_WIDTH), 0.05),
        "w_pool": nrm(ks[12], (DEPTH, POOL_GROUPS, POOL_GROUP, POOL_GROUP), POOL_GROUP ** -0.5),
        "pool_scale": 1.0 + nrm(ks[13], (DEPTH, POOL_WIDTH), 0.05),
        "w_out": nrm(ks[14], (DEPTH, MIX_WIDTH, D_MODEL), MIX_WIDTH ** -0.5),
        "norm1": 1.0 + nrm(ks[15], (DEPTH, D_MODEL), 0.05),
        "norm2": 1.0 + nrm(ks[16], (DEPTH, D_MODEL), 0.05),
        "peer_wq": nrm(ks[17], (DEPTH, D_MODEL, PEER_HEADS * PEER_QDIM), D_MODEL ** -0.5),
        "peer_keys": nrm(ks[18], (DEPTH, 2, PEER_HEADS, PEER_NKEYS, PEER_HALF), PEER_HALF ** -0.5),
        "peer_u": nrm(ks[19], (DEPTH, PEER_EXPERTS, D_MODEL), D_MODEL ** -0.5),
        "peer_v": nrm(ks[20], (DEPTH, PEER_EXPERTS, D_MODEL), 0.1),
        "w_ple": nrm(ks[21], (DEPTH, PLE_DIM, D_MODEL), PLE_DIM ** -0.5),
        "w_gate": nrm(ks[22], (DEPTH, D_MODEL, D_MODEL), D_MODEL ** -0.5),
        "norm_f": 1.0 + nrm(ks[23], (D_MODEL,), 0.05),
    }


def reference(x_prompt, x_sample, p_prompt, p_sample, state_C, state_n, state_m, state_pool,
              w_in, b_gate, mlstm_norm, w_pool, pool_scale, w_out, norm1, norm2,
              peer_wq, peer_keys, peer_u, peer_v, w_ple, w_gate, norm_f):
    params = (w_in, b_gate, mlstm_norm, w_pool, pool_scale, w_out, norm1, norm2,
              peer_wq, peer_keys, peer_u, peer_v, w_ple, w_gate)
    f32 = jnp.float32
    C0 = jnp.zeros((DEPTH, BATCH, MLSTM_HEADS, HEAD_DIM, HEAD_DIM), f32)
    n0 = jnp.zeros((DEPTH, BATCH, MLSTM_HEADS, HEAD_DIM), f32)
    m0 = jnp.zeros((DEPTH, BATCH, MLSTM_HEADS), f32)
    buf0 = jnp.zeros((DEPTH, BATCH, POOL_BUF, POOL_WIDTH), f32)
    y_prompt, C_p, n_p, m_p, pool_p = run_trunk(x_prompt, p_prompt, C0, n0, m0, buf0, 0, params, norm_f)
    y_sample, C_s, n_s, m_s, pool_s = run_trunk(x_sample, p_sample, state_C, state_n, state_m, state_pool,
                                                PAST_LEN, params, norm_f)
    return (y_prompt, y_sample, C_p, n_p, m_p, pool_p, C_s, n_s, m_s, pool_s)
```

```python
import functools
import math

import jax
import jax.numpy as jnp
from jax import lax
from jax.experimental import pallas as pl
from jax.experimental.pallas import tpu as pltpu

F32 = jnp.float32
BF16 = jnp.bfloat16
EPS = 1e-6
NEG_INF = float("-inf")

LANE = 128
V7X_VMEM_LIMIT = 56 * 1024 * 1024

HEADS = 4
HEAD_DIM = 128
MLSTM_W = HEADS * HEAD_DIM
POOL_WINDOWS = (2, 4, 8, 16)
POOL_GROUP = 128
POOL_W = POOL_GROUP * len(POOL_WINDOWS)
POOL_BUF = max(POOL_WINDOWS) - 1
CHUNK = 128
PEER_HEADS = 8
PEER_NKEYS = 128
PEER_HALF = 128
PEER_TOPK = 16
UNRANKED = 127.0
GELU_C0 = math.sqrt(2.0 / math.pi)
GELU_C1 = 0.044715 * GELU_C0


def _params(semantics):
    return pltpu.CompilerParams(dimension_semantics=semantics, vmem_limit_bytes=V7X_VMEM_LIMIT)


def _token_block(n, want):
    tb = min(n, want)
    assert n % tb == 0
    return tb


def _rmsnorm(x, g):
    return x * lax.rsqrt(jnp.mean(x * x, axis=-1, keepdims=True) + EPS) * g


def _log_sigmoid(x):
    return jnp.minimum(x, 0.0) - jnp.log(1.0 + jnp.exp(-jnp.abs(x)))


def _sigmoid(x):
    return 1.0 / (1.0 + jnp.exp(-x))


def _dot(a, b):
    return jnp.dot(a, b, preferred_element_type=F32)


def _dot_nt(a, b):
    return lax.dot_general(a, b, (((1,), (1,)), ((), ())), preferred_element_type=F32)


def _proj_kernel(x_ref, g_ref, wa_ref, wkt_ref, wgt_ref, bg_ref,
                 q_ref, kt_ref, v_ref, o_ref, u_ref, gt_ref):
    hn = _rmsnorm(x_ref[...], g_ref[...]).astype(BF16)
    pa = _dot(hn, wa_ref[...])
    w = MLSTM_W
    q_ref[...] = pa[:, 0:w].astype(BF16)
    v_ref[...] = pa[:, w:2 * w].astype(BF16)
    o_ref[...] = _sigmoid(pa[:, 2 * w:3 * w])
    u_ref[...] = pa[:, 3 * w:4 * w]
    kt = _dot_nt(wkt_ref[...], hn) * (HEAD_DIM ** -0.5)
    kt_ref[...] = kt.astype(BF16)
    gt_ref[...] = _dot_nt(wgt_ref[...], hn) + bg_ref[...]


def _proj(x, norm_g, wa, wkt, wgt, bg):
    n, d = x.shape
    tb = _token_block(n, 512)
    grid = (n // tb,)
    const = lambda i: (0, 0)
    return pl.pallas_call(
        _proj_kernel,
        grid=grid,
        in_specs=[
            pl.BlockSpec((tb, d), lambda i: (i, 0)),
            pl.BlockSpec((1, d), const),
            pl.BlockSpec(wa.shape, const),
            pl.BlockSpec(wkt.shape, const),
            pl.BlockSpec(wgt.shape, const),
            pl.BlockSpec(bg.shape, const),
        ],
        out_specs=[
            pl.BlockSpec((tb, MLSTM_W), lambda i: (i, 0)),
            pl.BlockSpec((MLSTM_W, tb), lambda i: (0, i)),
            pl.BlockSpec((tb, MLSTM_W), lambda i: (i, 0)),
            pl.BlockSpec((tb, MLSTM_W), lambda i: (i, 0)),
            pl.BlockSpec((tb, POOL_W), lambda i: (i, 0)),
            pl.BlockSpec((2 * HEADS, tb), lambda i: (0, i)),
        ],
        out_shape=[
            jax.ShapeDtypeStruct((n, MLSTM_W), BF16),
            jax.ShapeDtypeStruct((MLSTM_W, n), BF16),
            jax.ShapeDtypeStruct((n, MLSTM_W), BF16),
            jax.ShapeDtypeStruct((n, MLSTM_W), F32),
            jax.ShapeDtypeStruct((n, POOL_W), F32),
            jax.ShapeDtypeStruct((2 * HEADS, n), F32),
        ],
        compiler_params=_params(("parallel",)),
        name="proj",
    )(x, norm_g, wa, wkt, wgt, bg)


def _lane_cumsum(x):
    lane = lax.broadcasted_iota(jnp.int32, x.shape, 1)
    shift = 1
    while shift < x.shape[1]:
        x = x + jnp.where(lane >= shift, pltpu.roll(x, shift, axis=1), 0.0)
        shift *= 2
    return x


def _mlstm_kernel(q_ref, kt_ref, v_ref, o_ref, ig_ref, lf_ref, nw_ref,
                  hm_ref, cext_out_ref, m_out_ref, cext_ref):
    t_len = q_ref.shape[0]
    n_chunks = t_len // CHUNK
    ig2 = ig_ref[0]
    lf2 = _log_sigmoid(lf_ref[0])
    b2 = _lane_cumsum(lf2)
    a2 = ig2 - b2

    t_idx = lax.broadcasted_iota(jnp.int32, (CHUNK, CHUNK), 0)
    s_idx = lax.broadcasted_iota(jnp.int32, (CHUNK, CHUNK), 1)
    causal = s_idx <= t_idx
    ones_col = (lax.broadcasted_iota(jnp.int32, (CHUNK, HEAD_DIM), 1) == 0).astype(BF16)
    nw = nw_ref[...]

    cext_ref[...] = jnp.zeros_like(cext_ref)
    m = jnp.zeros((1, 1), F32)
    for c in range(n_chunks):
        rows = pl.ds(c * CHUNK, CHUNK)
        a_row = a2[c:c + 1, :]
        lf_row = lf2[c:c + 1, :]
        b_last = b2[c:c + 1, CHUNK - 1:CHUNK]
        b_col = jnp.sum(jnp.where(causal, lf_row, 0.0), axis=1, keepdims=True)
        amax_col = jnp.max(jnp.where(causal, a_row, NEG_INF), axis=1, keepdims=True)
        m_col = jnp.maximum(amax_col, m)
        decay_mat = jnp.where(causal, jnp.exp(a_row - m_col), 0.0)

        qc = q_ref[rows, :]
        ktc = kt_ref[:, rows]
        v_ext = jnp.concatenate([v_ref[rows, :], ones_col], axis=1)
        s = _dot(qc, ktc) * decay_mat
        intra = _dot(s.astype(BF16), v_ext)
        inter = _dot(qc, cext_ref[...].astype(BF16))
        tot = intra + jnp.exp(m - m_col) * inter
        num = tot[:, :HEAD_DIM]
        den = tot[:, HEAD_DIM:HEAD_DIM + 1]
        h = num / jnp.maximum(jnp.abs(den), jnp.exp(-(b_col + m_col)))
        h = h * lax.rsqrt(jnp.mean(h * h, axis=1, keepdims=True) + EPS)
        hm_ref[rows, :] = (h * nw * o_ref[rows, :]).astype(BF16)

        m_new = jnp.maximum(b_last + m, jnp.max(a_row, axis=1, keepdims=True) + b_last)
        ws_row = jnp.exp(a_row + (b_last - m_new))
        kws = (ktc.astype(F32) * ws_row).astype(BF16)
        cext_ref[...] = jnp.exp(b_last + m - m_new) * cext_ref[...] + _dot(kws, v_ext)
        m = m_new

    cext_out_ref[0, 0] = cext_ref[...]
    m_out_ref[0, 0] = jnp.broadcast_to(m, m_out_ref.shape[2:])


def _mlstm(q, kt, v, o, gates3, norm_w, batch, t_len):
    n = q.shape[0]
    n_chunks = t_len // CHUNK
    seq = lambda b, h: (b, h)
    return pl.pallas_call(
        _mlstm_kernel,
        grid=(batch, HEADS),
        in_specs=[
            pl.BlockSpec((t_len, HEAD_DIM), seq),
            pl.BlockSpec((HEAD_DIM, t_len), lambda b, h: (h, b)),
            pl.BlockSpec((t_len, HEAD_DIM), seq),
            pl.BlockSpec((t_len, HEAD_DIM), seq),
            pl.BlockSpec((1, n_chunks, CHUNK), lambda b, h: (h, b, 0)),
            pl.BlockSpec((1, n_chunks, CHUNK), lambda b, h: (h + HEADS, b, 0)),
            pl.BlockSpec((1, HEAD_DIM), lambda b, h: (0, h)),
        ],
        out_specs=[
            pl.BlockSpec((t_len, HEAD_DIM), seq),
            pl.BlockSpec((1, 1, HEAD_DIM, 2 * HEAD_DIM), lambda b, h: (b, h, 0, 0)),
            pl.BlockSpec((1, 1, 8, LANE), lambda b, h: (b, h, 0, 0)),
        ],
        out_shape=[
            jax.ShapeDtypeStruct((n, MLSTM_W), BF16),
            jax.ShapeDtypeStruct((batch, HEADS, HEAD_DIM, 2 * HEAD_DIM), F32),
            jax.ShapeDtypeStruct((batch, HEADS, 8, LANE), F32),
        ],
        scratch_shapes=[pltpu.VMEM((HEAD_DIM, 2 * HEAD_DIM), F32)],
        compiler_params=_params(("parallel", "parallel")),
        name="mlstm",
    )(q, kt, v, o, gates3, gates3, norm_w)


def _mlstm1_kernel(q_ref, k_ref, v_ref, o_ref, g_ref, c_ref, n_ref, m_ref, nw_ref,
                   hm_ref, c_out_ref, n_out_ref, m_out_ref):
    bb = q_ref.shape[0]
    row = lax.broadcasted_iota(jnp.int32, (bb, HEAD_DIM), 0)
    eye = (lax.broadcasted_iota(jnp.int32, (HEAD_DIM, HEAD_DIM), 0)
           == lax.broadcasted_iota(jnp.int32, (HEAD_DIM, HEAD_DIM), 1))
    lane_h = lax.broadcasted_iota(jnp.int32, (bb, HEADS), 1)
    g = g_ref[...]
    m_all = m_ref[...]
    m_out = jnp.zeros((bb, HEADS), F32)
    for h in range(HEADS):
        cols = slice(h * HEAD_DIM, (h + 1) * HEAD_DIM)
        qb = q_ref[:, cols]
        qf = qb.astype(F32)
        kf = k_ref[:, cols].astype(F32)
        vf = v_ref[:, cols].astype(F32)
        ig = g[:, h:h + 1]
        lf = _log_sigmoid(g[:, HEADS + h:HEADS + h + 1])
        m_old = m_all[:, h:h + 1]
        n_old = n_ref[:, cols]
        m_new = jnp.maximum(ig, lf + m_old)
        w_in = jnp.exp(ig - m_new)
        w_st = jnp.exp(lf + m_old - m_new)
        qc = jnp.zeros((bb, HEAD_DIM), F32)
        for j in range(bb):
            c_old = c_ref[j, h]
            res = _dot(qb, c_old.astype(BF16))
            qc = jnp.where(row == j, res, qc)
            k_col = jnp.sum(jnp.where(eye, kf[j:j + 1, :], 0.0), axis=1, keepdims=True)
            c_out_ref[j, h] = w_st[j:j + 1, :] * c_old + k_col * (w_in[j:j + 1, :] * vf[j:j + 1, :])
        s = jnp.sum(qf * kf, axis=1, keepdims=True) * w_in
        num = s * vf + w_st * qc
        den = s + w_st * jnp.sum(qf * n_old, axis=1, keepdims=True)
        hh = num / jnp.maximum(jnp.abs(den), jnp.exp(-m_new))
        hh = hh * lax.rsqrt(jnp.mean(hh * hh, axis=1, keepdims=True) + EPS)
        hm_ref[:, cols] = (hh * nw_ref[:, cols] * o_ref[:, cols]).astype(BF16)
        n_out_ref[:, cols] = w_st * n_old + w_in * kf
        m_out = jnp.where(lane_h == h, m_new, m_out)
    m_out_ref[...] = m_out


def _mlstm1(q, k, v, o, g, c0, n0, m0, norm_w):
    batch = q.shape[0]
    bb = _token_block(batch, 16)
    rows = lambda i: (i, 0)
    return pl.pallas_call(
        _mlstm1_kernel,
        grid=(batch // bb,),
        in_specs=[
            pl.BlockSpec((bb, MLSTM_W), rows),
            pl.BlockSpec((bb, MLSTM_W), rows),
            pl.BlockSpec((bb, MLSTM_W), rows),
            pl.BlockSpec((bb, MLSTM_W), rows),
            pl.BlockSpec((bb, 2 * HEADS), rows),
            pl.BlockSpec((bb, HEADS, HEAD_DIM, HEAD_DIM), lambda i: (i, 0, 0, 0)),
            pl.BlockSpec((bb, MLSTM_W), rows),
            pl.BlockSpec((bb, HEADS), rows),
            pl.BlockSpec((1, MLSTM_W), lambda i: (0, 0)),
        ],
        out_specs=[
            pl.BlockSpec((bb, MLSTM_W), rows),
            pl.BlockSpec((bb, HEADS, HEAD_DIM, HEAD_DIM), lambda i: (i, 0, 0, 0)),
            pl.BlockSpec((bb, MLSTM_W), rows),
            pl.BlockSpec((bb, HEADS), rows),
        ],
        out_shape=[
            jax.ShapeDtypeStruct((batch, MLSTM_W), BF16),
            jax.ShapeDtypeStruct(c0.shape, F32),
            jax.ShapeDtypeStruct(n0.shape, F32),
            jax.ShapeDtypeStruct(m0.shape, F32),
        ],
        compiler_params=_params(("parallel",)),
        name="mlstm1",
    )(q, k, v, o, g, c0, n0, m0, norm_w)


def _pool_kernel(u_ref, wp_ref, sc_ref, z_ref, buf_ref):
    t_len = u_ref.shape[0]
    t_idx = lax.broadcasted_iota(jnp.int32, (t_len, POOL_GROUP), 0)
    for g, w in enumerate(POOL_WINDOWS):
        cols = slice(g * POOL_GROUP, (g + 1) * POOL_GROUP)
        x = u_ref[:, cols]
        s = x
        k = 1
        while k < w:
            s = s + jnp.where(t_idx >= k, pltpu.roll(s, k, axis=0), 0.0)
            k *= 2
        cnt = jnp.minimum(t_idx + 1, w).astype(F32)
        r = s / cnt - x
        z = _dot(r.astype(BF16), wp_ref[g]) * sc_ref[:, cols]
        z_ref[:, cols] = z.astype(BF16)
    buf_ref[0] = u_ref[t_len - POOL_BUF:t_len, :]


def _pool(u, w_pool, scale, batch, t_len):
    n = u.shape[0]
    return pl.pallas_call(
        _pool_kernel,
        grid=(batch,),
        in_specs=[
            pl.BlockSpec((t_len, POOL_W), lambda b: (b, 0)),
            pl.BlockSpec(w_pool.shape, lambda b: (0, 0, 0)),
            pl.BlockSpec((1, POOL_W), lambda b: (0, 0)),
        ],
        out_specs=[
            pl.BlockSpec((t_len, POOL_W), lambda b: (b, 0)),
            pl.BlockSpec((1, POOL_BUF, POOL_W), lambda b: (b, 0, 0)),
        ],
        out_shape=[
            jax.ShapeDtypeStruct((n, POOL_W), BF16),
            jax.ShapeDtypeStruct((batch, POOL_BUF, POOL_W), F32),
        ],
        compiler_params=_params(("parallel",)),
        name="pool",
    )(u, w_pool, scale)


def _pool1_kernel(u_ref, buft_ref, wp_ref, sc_ref, z_ref, buft_out_ref):
    for g, w in enumerate(POOL_WINDOWS):
        cols = slice(g * POOL_GROUP, (g + 1) * POOL_GROUP)
        x = u_ref[:, cols]
        s = x
        for j in range(1, w):
            s = s + buft_ref[POOL_BUF - j, :, cols]
        r = s / float(w) - x
        z = _dot(r.astype(BF16), wp_ref[g]) * sc_ref[:, cols]
        z_ref[:, cols] = z.astype(BF16)
    for j in range(POOL_BUF - 1):
        buft_out_ref[j] = buft_ref[j + 1]
    buft_out_ref[POOL_BUF - 1] = u_ref[...]


def _pool1(u, buf_t, w_pool, scale):
    batch = u.shape[0]
    full2 = lambda i: (0, 0)
    full3 = lambda i: (0, 0, 0)
    return pl.pallas_call(
        _pool1_kernel,
        grid=(1,),
        in_specs=[
            pl.BlockSpec(u.shape, full2),
            pl.BlockSpec(buf_t.shape, full3),
            pl.BlockSpec(w_pool.shape, full3),
            pl.BlockSpec((1, POOL_W), full2),
        ],
        out_specs=[
            pl.BlockSpec((batch, POOL_W), full2),
            pl.BlockSpec(buf_t.shape, full3),
        ],
        out_shape=[
            jax.ShapeDtypeStruct((batch, POOL_W), BF16),
            jax.ShapeDtypeStruct(buf_t.shape, F32),
        ],
        compiler_params=_params(("arbitrary",)),
        name="pool1",
    )(u, buf_t, w_pool, scale)


def _mix_kernel(x_ref, hm_ref, zp_ref, wo_ref, g_ref, x1_ref, xnt_ref):
    mix = _dot(hm_ref[...], wo_ref[0:MLSTM_W, :]) + _dot(zp_ref[...], wo_ref[MLSTM_W:, :])
    x1 = x_ref[...] + mix
    x1_ref[...] = x1
    xnt_ref[...] = _rmsnorm(x1, g_ref[...]).T.astype(BF16)


def _mix(x, hm, zp, w_out, norm_g):
    n, d = x.shape
    tb = _token_block(n, 512)
    rows = lambda i: (i, 0)
    const = lambda i: (0, 0)
    return pl.pallas_call(
        _mix_kernel,
        grid=(n // tb,),
        in_specs=[
            pl.BlockSpec((tb, d), rows),
            pl.BlockSpec((tb, MLSTM_W), rows),
            pl.BlockSpec((tb, POOL_W), rows),
            pl.BlockSpec(w_out.shape, const),
            pl.BlockSpec((1, d), const),
        ],
        out_specs=[
            pl.BlockSpec((tb, d), rows),
            pl.BlockSpec((d, tb), lambda i: (0, i)),
        ],
        out_shape=[
            jax.ShapeDtypeStruct((n, d), F32),
            jax.ShapeDtypeStruct((d, n), BF16),
        ],
        compiler_params=_params(("parallel",)),
        name="mix",
    )(x, hm, zp, w_out, norm_g)


def _extract_max(vals, pos):
    m = vals[0]
    for v in vals[1:]:
        m = jnp.maximum(m, v)
    m = jnp.max(m, axis=0, keepdims=True)
    big = jnp.float32(1e9)
    idx = None
    for v, p in zip(vals, pos):
        cand = jnp.where(v == m, p, big)
        idx = cand if idx is None else jnp.minimum(idx, cand)
    idx = jnp.min(idx, axis=0, keepdims=True)
    return m, idx


def _top16(s):
    tb = s.shape[1]
    row = lax.broadcasted_iota(jnp.int32, s.shape, 0).astype(F32)
    krow = lax.broadcasted_iota(jnp.int32, (PEER_TOPK, tb), 0)
    rank = jnp.full(s.shape, UNRANKED, F32)
    sv = jnp.zeros((PEER_TOPK, tb), F32)
    for k in range(PEER_TOPK):
        m, idx = _extract_max([s], [row])
        hit = row == idx
        rank = jnp.where(hit, float(k), rank)
        s = jnp.where(hit, NEG_INF, s)
        sv = jnp.where(krow == k, m, sv)
    return sv, rank


def _candidate_groups(sv1, sv2):
    tb = sv1.shape[1]
    vals, pos = [], []
    r8 = lax.broadcasted_iota(jnp.int32, (8, tb), 0)
    vals.append(sv1[0:1, :] + sv2[8:16, :])
    pos.append((r8 + 8).astype(F32))
    vals.append(sv1[0:1, :] + sv2[0:8, :])
    pos.append(r8.astype(F32))
    for k1 in range(1, 8):
        lim = PEER_TOPK // (k1 + 1)
        v = sv1[k1:k1 + 1, :] + sv2[0:8, :]
        vals.append(jnp.where(r8 < lim, v, NEG_INF))
        pos.append((r8 + k1 * PEER_TOPK).astype(F32))
    vals.append(sv1[8:16, :] + sv2[0:1, :])
    pos.append(((r8 + 8) * PEER_TOPK).astype(F32))
    return vals, pos


def _route_kernel(xnt_ref, wqt_ref, k1_ref, k2_ref, cnt1_ref, g1_ref, r2_ref, e2_ref):
    qt = _dot(wqt_ref[...], xnt_ref[...]).astype(BF16)
    s1 = _dot(k1_ref[0], qt[0:PEER_HALF, :])
    s2 = _dot(k2_ref[0], qt[PEER_HALF:, :])
    sv1, r1 = _top16(s1)
    sv2, r2 = _top16(s2)

    vals, pos = _candidate_groups(sv1, sv2)
    orig = list(vals)
    sel = [jnp.zeros(v.shape, F32) for v in vals]
    for _ in range(PEER_TOPK):
        _, idx = _extract_max(vals, pos)
        for i in range(len(vals)):
            hit = pos[i] == idx
            sel[i] = jnp.where(hit, 1.0, sel[i])
            vals[i] = jnp.where(hit, NEG_INF, vals[i])

    cmax = orig[1][0:1, :]
    z = None
    for o, sl in zip(orig, sel):
        part = jnp.sum(jnp.where(sl > 0.0, jnp.exp(o - cmax), 0.0), axis=0, keepdims=True)
        z = part if z is None else z + part
    cnt = [jnp.sum(sel[0] + sel[1], axis=0, keepdims=True)]
    cnt += [jnp.sum(sel[k1 + 1], axis=0, keepdims=True) for k1 in range(1, 8)]
    cnt += [sel[9][j:j + 1, :] for j in range(8)]

    cnt1 = jnp.zeros(r1.shape, F32)
    for k1 in range(PEER_TOPK):
        cnt1 = jnp.where(r1 == float(k1), cnt[k1], cnt1)
    cnt1_ref[0] = cnt1
    g1_ref[0] = jnp.exp(s1 - sv1[0:1, :]) * (0.5 / z)
    r2_ref[0] = r2
    e2_ref[0] = jnp.exp(s2 - sv2[0:1, :])


def _route(xnt, wqt, keys):
    d, n = xnt.shape
    tb = _token_block(n, 256)
    per_head = pl.BlockSpec((1, PEER_NKEYS, tb), lambda i, h: (h, 0, i))
    shape = jax.ShapeDtypeStruct((PEER_HEADS, PEER_NKEYS, n), F32)
    return pl.pallas_call(
        _route_kernel,
        grid=(n // tb, PEER_HEADS),
        in_specs=[
            pl.BlockSpec((d, tb), lambda i, h: (0, i)),
            pl.BlockSpec((2 * PEER_HALF, d), lambda i, h: (h, 0)),
            pl.BlockSpec((1, PEER_NKEYS, PEER_HALF), lambda i, h: (h, 0, 0)),
            pl.BlockSpec((1, PEER_NKEYS, PEER_HALF), lambda i, h: (h + PEER_HEADS, 0, 0)),
        ],
        out_specs=[per_head] * 4,
        out_shape=[shape] * 4,
        compiler_params=_params(("parallel", "parallel")),
        name="route",
    )(xnt, wqt, keys, keys)


I1_PER_TILE = 8
EXPERT_TILE = I1_PER_TILE * PEER_NKEYS


def _experts_kernel(xnt_ref, cnt1_ref, g1_ref, r2_ref, e2_ref, u_ref, vt_ref, x1_ref,
                    x2_ref, acc_ref, act_ref, wa_ref):
    e = pl.program_id(1)
    tb = xnt_ref.shape[1]

    @pl.when(e == 0)
    def _():
        acc_ref[...] = jnp.zeros_like(acc_ref)

    act_ref[...] = _dot(u_ref[...], xnt_ref[...])
    sub = 16
    i1_rows = pl.ds(pl.multiple_of(e * I1_PER_TILE, I1_PER_TILE), I1_PER_TILE)
    for lt in range(tb // LANE):
        lanes = pl.ds(lt * LANE, LANE)
        cnt_tile = [cnt1_ref[h, i1_rows, lanes] for h in range(PEER_HEADS)]
        g1_tile = [g1_ref[h, i1_rows, lanes] for h in range(PEER_HEADS)]
        for j in range(I1_PER_TILE):
            cnt_rows = [t[j:j + 1, :] for t in cnt_tile]
            g1_rows = [t[j:j + 1, :] for t in g1_tile]
            for rb in range(PEER_NKEYS // sub):
                krows = pl.ds(rb * sub, sub)
                w = jnp.zeros((sub, LANE), F32)
                for h in range(PEER_HEADS):
                    mask = r2_ref[h, krows, lanes] < cnt_rows[h]
                    w = w + jnp.where(mask, e2_ref[h, krows, lanes] * g1_rows[h], 0.0)
                erows = pl.ds(j * PEER_NKEYS + rb * sub, sub)
                a = act_ref[erows, lanes]
                th = jnp.tanh(a * (GELU_C0 + GELU_C1 * (a * a)))
                wa_ref[erows, lanes] = (w * (a * (1.0 + th))).astype(BF16)
    acc_ref[...] += _dot(vt_ref[...], wa_ref[...])

    @pl.when(e == pl.num_programs(1) - 1)
    def _():
        x2_ref[...] = x1_ref[...] + acc_ref[...].T


def _experts(xnt, cnt1, g1, r2, e2, u_tab, vt_tab, x1):
    d, n = xnt.shape
    n_exp = u_tab.shape[0]
    tb = _token_block(n, 512)
    assert tb % LANE == 0 and n_exp % EXPERT_TILE == 0
    per_head = pl.BlockSpec((PEER_HEADS, PEER_NKEYS, tb), lambda i, e: (0, 0, i))
    return pl.pallas_call(
        _experts_kernel,
        grid=(n // tb, n_exp // EXPERT_TILE),
        in_specs=[
            pl.BlockSpec((d, tb), lambda i, e: (0, i)),
            per_head, per_head, per_head, per_head,
            pl.BlockSpec((EXPERT_TILE, d), lambda i, e: (e, 0)),
            pl.BlockSpec((d, EXPERT_TILE), lambda i, e: (0, e)),
            pl.BlockSpec((tb, d), lambda i, e: (i, 0)),
        ],
        out_specs=pl.BlockSpec((tb, d), lambda i, e: (i, 0)),
        out_shape=jax.ShapeDtypeStruct((n, d), F32),
        scratch_shapes=[
            pltpu.VMEM((d, tb), F32),
            pltpu.VMEM((EXPERT_TILE, tb), F32),
            pltpu.VMEM((EXPERT_TILE, tb), BF16),
        ],
        compiler_params=_params(("parallel", "arbitrary")),
        name="experts",
    )(xnt, cnt1, g1, r2, e2, u_tab, vt_tab, x1)


def _ple_kernel(x_ref, p_ref, wg_ref, wp_ref, nf_ref, y_ref, *, final_norm):
    x = x_ref[...]
    gate = _sigmoid(_dot(x.astype(BF16), wg_ref[...]))
    y = x + gate * _dot(p_ref[...].astype(BF16), wp_ref[...])
    if final_norm:
        y = _rmsnorm(y, nf_ref[...])
    y_ref[...] = y


def _ple(x, p, w_gate, w_ple, norm_f, final_norm):
    n, d = x.shape
    tb = _token_block(n, 512)
    rows = lambda i: (i, 0)
    const = lambda i: (0, 0)
    return pl.pallas_call(
        functools.partial(_ple_kernel, final_norm=final_norm),
        grid=(n // tb,),
        in_specs=[
            pl.BlockSpec((tb, d), rows),
            pl.BlockSpec((tb, p.shape[1]), rows),
            pl.BlockSpec(w_gate.shape, const),
            pl.BlockSpec(w_ple.shape, const),
            pl.BlockSpec((1, d), const),
        ],
        out_specs=pl.BlockSpec((tb, d), rows),
        out_shape=jax.ShapeDtypeStruct((n, d), F32),
        compiler_params=_params(("parallel",)),
        name="ple",
    )(x, p, w_gate, w_ple, norm_f)


def _prep_layer_weights(l, w_in, b_gate, mlstm_norm, w_pool, pool_scale, w_out, norm1, norm2,
                        peer_wq, peer_keys, peer_u, peer_v, w_ple, w_gate):
    w = MLSTM_W
    wi = w_in[l]
    d = wi.shape[0]
    wa = jnp.concatenate([wi[:, 0:w], wi[:, 2 * w:4 * w + POOL_W]], axis=1).astype(BF16)
    wkt = wi[:, w:2 * w].T.astype(BF16)
    wgt = wi[:, 4 * w + POOL_W:].T.astype(BF16)
    return dict(
        wa=wa, wkt=wkt, wgt=wgt,
        bg=b_gate[l].reshape(2 * HEADS, 1).astype(F32),
        mlstm_norm=mlstm_norm[l].reshape(1, w),
        w_pool=w_pool[l].astype(BF16),
        pool_scale=pool_scale[l].reshape(1, POOL_W),
        w_out=w_out[l].astype(BF16),
        norm1=norm1[l].reshape(1, d), norm2=norm2[l].reshape(1, d),
        wqt=peer_wq[l].T.astype(BF16),
        keys=peer_keys[l].reshape(2 * PEER_HEADS, PEER_NKEYS, PEER_HALF).astype(BF16),
        u_tab=peer_u[l].astype(BF16),
        vt_tab=peer_v[l].T.astype(BF16),
        w_ple=w_ple[l].astype(BF16), w_gate=w_gate[l].astype(BF16),
    )


def _ffn_and_embed(x1, xnt, p, lw, norm_f, final_norm):
    cnt1, g1, r2, e2 = _route(xnt, lw["wqt"], lw["keys"])
    x2 = _experts(xnt, cnt1, g1, r2, e2, lw["u_tab"], lw["vt_tab"], x1)
    return _ple(x2, p, lw["w_gate"], lw["w_ple"], norm_f, final_norm)


def _prompt_layer(x, p, lw, norm_f, final_norm, batch, t_len):
    q, kt, v, o, u, gt = _proj(x, lw["norm1"], lw["wa"], lw["wkt"], lw["wgt"], lw["bg"])
    gates3 = gt.reshape(2 * HEADS, (batch * t_len) // CHUNK, CHUNK)
    hm, cext, m8 = _mlstm(q, kt, v, o, gates3, lw["mlstm_norm"], batch, t_len)
    zp, buf = _pool(u, lw["w_pool"], lw["pool_scale"], batch, t_len)
    x1, xnt = _mix(x, hm, zp, lw["w_out"], lw["norm2"])
    y = _ffn_and_embed(x1, xnt, p, lw, norm_f, final_norm)
    return y, cext[..., :HEAD_DIM], cext[..., HEAD_DIM], m8[..., 0, 0], buf


def _sample_layer(x, p, c0, n0, m0, buf0, lw, norm_f, final_norm):
    q, kt, v, o, u, gt = _proj(x, lw["norm1"], lw["wa"], lw["wkt"], lw["wgt"], lw["bg"])
    hm, c1, n1, m1 = _mlstm1(q, kt.T, v, o, gt.T, c0, n0.reshape(-1, MLSTM_W), m0, lw["mlstm_norm"])
    n1 = n1.reshape(n0.shape)
    zp, buf_t = _pool1(u, jnp.swapaxes(buf0, 0, 1), lw["w_pool"], lw["pool_scale"])
    x1, xnt = _mix(x, hm, zp, lw["w_out"], lw["norm2"])
    y = _ffn_and_embed(x1, xnt, p, lw, norm_f, final_norm)
    return y, c1, n1, m1, jnp.swapaxes(buf_t, 0, 1)


def kernel(x_prompt, x_sample, p_prompt, p_sample, state_C, state_n, state_m, state_pool,
           w_in, b_gate, mlstm_norm, w_pool, pool_scale, w_out, norm1, norm2,
           peer_wq, peer_keys, peer_u, peer_v, w_ple, w_gate, norm_f):
    depth = w_in.shape[0]
    batch, t_len, d = x_prompt.shape
    dec_batch, dec_len, _ = x_sample.shape
    assert dec_len == 1 and t_len % CHUNK == 0
    nf = norm_f.reshape(1, d)

    xp = x_prompt.reshape(batch * t_len, d)
    xs = x_sample.reshape(dec_batch, d)
    outs_p, outs_s = [], []
    for l in range(depth):
        lw = _prep_layer_weights(l, w_in, b_gate, mlstm_norm, w_pool, pool_scale, w_out, norm1,
                                 norm2, peer_wq, peer_keys, peer_u, peer_v, w_ple, w_gate)
        final = l == depth - 1
        pp = p_prompt[l].reshape(batch * t_len, -1)
        ps = p_sample[l].reshape(dec_batch, -1)
        xp, c_p, n_p, m_p, buf_p = _prompt_layer(xp, pp, lw, nf, final, batch, t_len)
        xs, c_s, n_s, m_s, buf_s = _sample_layer(xs, ps, state_C[l], state_n[l], state_m[l],
                                                 state_pool[l], lw, nf, final)
        outs_p.append((c_p, n_p, m_p, buf_p))
        outs_s.append((c_s, n_s, m_s, buf_s))

    stack = lambda outs, i: jnp.stack([o[i] for o in outs])
    return (xp.reshape(batch, t_len, d), xs.reshape(dec_batch, dec_len, d),
            stack(outs_p, 0), stack(outs_p, 1), stack(outs_p, 2), stack(outs_p, 3),
            stack(outs_s, 0), stack(outs_s, 1), stack(outs_s, 2), stack(outs_s, 3))
```

```python
import functools
import math

import jax
import jax.numpy as jnp
from jax import lax
from jax.experimental import pallas as pl
from jax.experimental.pallas import tpu as pltpu

F32 = jnp.float32
BF16 = jnp.bfloat16
EPS = 1e-6
NEG_INF = float("-inf")

LANE = 128
V7X_VMEM_LIMIT = 56 * 1024 * 1024

HEADS = 4
HEAD_DIM = 128
MLSTM_W = HEADS * HEAD_DIM
POOL_WINDOWS = (2, 4, 8, 16)
POOL_GROUP = 128
POOL_W = POOL_GROUP * len(POOL_WINDOWS)
POOL_BUF = max(POOL_WINDOWS) - 1
CHUNK = 128
PEER_HEADS = 8
PEER_NKEYS = 128
PEER_HALF = 128
PEER_TOPK = 16
UNRANKED = 127.0
GELU_C0 = math.sqrt(2.0 / math.pi)
GELU_C1 = 0.044715 * GELU_C0


def _params(semantics):
    return pltpu.CompilerParams(dimension_semantics=semantics, vmem_limit_bytes=V7X_VMEM_LIMIT)


def _token_block(n, want):
    tb = min(n, want)
    assert n % tb == 0
    return tb


def _rmsnorm(x, g):
    return x * lax.rsqrt(jnp.mean(x * x, axis=-1, keepdims=True) + EPS) * g


def _log_sigmoid(x):
    return jnp.minimum(x, 0.0) - jnp.log(1.0 + jnp.exp(-jnp.abs(x)))


def _sigmoid(x):
    return 1.0 / (1.0 + jnp.exp(-x))


def _dot(a, b):
    return jnp.dot(a, b, preferred_element_type=F32)


def _dot_nt(a, b):
    return lax.dot_general(a, b, (((1,), (1,)), ((), ())), preferred_element_type=F32)


def _proj_kernel(x_ref, g_ref, wa_ref, wkt_ref, wgt_ref, bg_ref,
                 q_ref, kt_ref, v_ref, o_ref, u_ref, gt_ref):
    hn = _rmsnorm(x_ref[...], g_ref[...]).astype(BF16)
    pa = _dot(hn, wa_ref[...])
    w = MLSTM_W
    q_ref[...] = pa[:, 0:w].astype(BF16)
    v_ref[...] = pa[:, w:2 * w].astype(BF16)
    o_ref[...] = _sigmoid(pa[:, 2 * w:3 * w])
    u_ref[...] = pa[:, 3 * w:4 * w]
    kt = _dot_nt(wkt_ref[...], hn) * (HEAD_DIM ** -0.5)
    kt_ref[...] = kt.astype(BF16)
    gt_ref[...] = _dot_nt(wgt_ref[...], hn) + bg_ref[...]


def _proj(x, norm_g, wa, wkt, wgt, bg):
    n, d = x.shape
    tb = _token_block(n, 512)
    grid = (n // tb,)
    const = lambda i: (0, 0)
    return pl.pallas_call(
        _proj_kernel,
        grid=grid,
        in_specs=[
            pl.BlockSpec((tb, d), lambda i: (i, 0)),
            pl.BlockSpec((1, d), const),
            pl.BlockSpec(wa.shape, const),
            pl.BlockSpec(wkt.shape, const),
            pl.BlockSpec(wgt.shape, const),
            pl.BlockSpec(bg.shape, const),
        ],
        out_specs=[
            pl.BlockSpec((tb, MLSTM_W), lambda i: (i, 0)),
            pl.BlockSpec((MLSTM_W, tb), lambda i: (0, i)),
            pl.BlockSpec((tb, MLSTM_W), lambda i: (i, 0)),
            pl.BlockSpec((tb, MLSTM_W), lambda i: (i, 0)),
            pl.BlockSpec((tb, POOL_W), lambda i: (i, 0)),
            pl.BlockSpec((2 * HEADS, tb), lambda i: (0, i)),
        ],
        out_shape=[
            jax.ShapeDtypeStruct((n, MLSTM_W), BF16),
            jax.ShapeDtypeStruct((MLSTM_W, n), BF16),
            jax.ShapeDtypeStruct((n, MLSTM_W), BF16),
            jax.ShapeDtypeStruct((n, MLSTM_W), F32),
            jax.ShapeDtypeStruct((n, POOL_W), F32),
            jax.ShapeDtypeStruct((2 * HEADS, n), F32),
        ],
        compiler_params=_params(("parallel",)),
        name="proj",
    )(x, norm_g, wa, wkt, wgt, bg)


def _lane_cumsum(x):
    lane = lax.broadcasted_iota(jnp.int32, x.shape, 1)
    shift = 1
    while shift < x.shape[1]:
        x = x + jnp.where(lane >= shift, pltpu.roll(x, shift, axis=1), 0.0)
        shift *= 2
    return x


def _mlstm_kernel(q_ref, kt_ref, v_ref, o_ref, ig_ref, lf_ref, nw_ref,
                  hm_ref, cext_out_ref, m_out_ref, cext_ref):
    t_len = q_ref.shape[0]
    n_chunks = t_len // CHUNK
    ig2 = ig_ref[0]
    lf2 = _log_sigmoid(lf_ref[0])
    b2 = _lane_cumsum(lf2)
    a2 = ig2 - b2

    t_idx = lax.broadcasted_iota(jnp.int32, (CHUNK, CHUNK), 0)
    s_idx = lax.broadcasted_iota(jnp.int32, (CHUNK, CHUNK), 1)
    causal = s_idx <= t_idx
    ones_col = (lax.broadcasted_iota(jnp.int32, (CHUNK, HEAD_DIM), 1) == 0).astype(BF16)
    nw = nw_ref[...]

    cext_ref[...] = jnp.zeros_like(cext_ref)
    m = jnp.zeros((1, 1), F32)
    for c in range(n_chunks):
        rows = pl.ds(c * CHUNK, CHUNK)
        a_row = a2[c:c + 1, :]
        lf_row = lf2[c:c + 1, :]
        b_last = b2[c:c + 1, CHUNK - 1:CHUNK]
        b_col = jnp.sum(jnp.where(causal, lf_row, 0.0), axis=1, keepdims=True)
        amax_col = jnp.max(jnp.where(causal, a_row, NEG_INF), axis=1, keepdims=True)
        m_col = jnp.maximum(amax_col, m)
        decay_mat = jnp.where(causal, jnp.exp(a_row - m_col), 0.0)

        qc = q_ref[rows, :]
        ktc = kt_ref[:, rows]
        v_ext = jnp.concatenate([v_ref[rows, :], ones_col], axis=1)
        s = _dot(qc, ktc) * decay_mat
        intra = _dot(s.astype(BF16), v_ext)
        inter = _dot(qc, cext_ref[...].astype(BF16))
        tot = intra + jnp.exp(m - m_col) * inter
        num = tot[:, :HEAD_DIM]
        den = tot[:, HEAD_DIM:HEAD_DIM + 1]
        h = num / jnp.maximum(jnp.abs(den), jnp.exp(-(b_col + m_col)))
        h = h * lax.rsqrt(jnp.mean(h * h, axis=1, keepdims=True) + EPS)
        hm_ref[rows, :] = (h * nw * o_ref[rows, :]).astype(BF16)

        m_new = jnp.maximum(b_last + m, jnp.max(a_row, axis=1, keepdims=True) + b_last)
        ws_row = jnp.exp(a_row + (b_last - m_new))
        kws = (ktc.astype(F32) * ws_row).astype(BF16)
        cext_ref[...] = jnp.exp(b_last + m - m_new) * cext_ref[...] + _dot(kws, v_ext)
        m = m_new

    cext_out_ref[0, 0] = cext_ref[...]
    m_out_ref[0, 0] = jnp.broadcast_to(m, m_out_ref.shape[2:])


def _mlstm(q, kt, v, o, gates3, norm_w, batch, t_len):
    n = q.shape[0]
    n_chunks = t_len // CHUNK
    seq = lambda b, h: (b, h)
    return pl.pallas_call(
        _mlstm_kernel,
        grid=(batch, HEADS),
        in_specs=[
            pl.BlockSpec((t_len, HEAD_DIM), seq),
            pl.BlockSpec((HEAD_DIM, t_len), lambda b, h: (h, b)),
            pl.BlockSpec((t_len, HEAD_DIM), seq),
            pl.BlockSpec((t_len, HEAD_DIM), seq),
            pl.BlockSpec((1, n_chunks, CHUNK), lambda b, h: (h, b, 0)),
            pl.BlockSpec((1, n_chunks, CHUNK), lambda b, h: (h + HEADS, b, 0)),
            pl.BlockSpec((1, HEAD_DIM), lambda b, h: (0, h)),
        ],
        out_specs=[
            pl.BlockSpec((t_len, HEAD_DIM), seq),
            pl.BlockSpec((1, 1, HEAD_DIM, 2 * HEAD_DIM), lambda b, h: (b, h, 0, 0)),
            pl.BlockSpec((1, 1, 8, LANE), lambda b, h: (b, h, 0, 0)),
        ],
        out_shape=[
            jax.ShapeDtypeStruct((n, MLSTM_W), BF16),
            jax.ShapeDtypeStruct((batch, HEADS, HEAD_DIM, 2 * HEAD_DIM), F32),
            jax.ShapeDtypeStruct((batch, HEADS, 8, LANE), F32),
        ],
        scratch_shapes=[pltpu.VMEM((HEAD_DIM, 2 * HEAD_DIM), F32)],
        compiler_params=_params(("parallel", "parallel")),
        name="mlstm",
    )(q, kt, v, o, gates3, gates3, norm_w)


def _mlstm1_kernel(q_ref, k_ref, v_ref, o_ref, g_ref, c_ref, n_ref, m_ref, nw_ref,
                   hm_ref, c_out_ref, n_out_ref, m_out_ref):
    bb = q_ref.shape[0]
    row = lax.broadcasted_iota(jnp.int32, (bb, HEAD_DIM), 0)
    eye = (lax.broadcasted_iota(jnp.int32, (HEAD_DIM, HEAD_DIM), 0)
           == lax.broadcasted_iota(jnp.int32, (HEAD_DIM, HEAD_DIM), 1))
    lane_h = lax.broadcasted_iota(jnp.int32, (bb, HEADS), 1)
    g = g_ref[...]
    m_all = m_ref[...]
    m_out = jnp.zeros((bb, HEADS), F32)
    for h in range(HEADS):
        cols = slice(h * HEAD_DIM, (h + 1) * HEAD_DIM)
        qb = q_ref[:, cols]
        qf = qb.astype(F32)
        kf = k_ref[:, cols].astype(F32)
        vf = v_ref[:, cols].astype(F32)
        ig = g[:, h:h + 1]
        lf = _log_sigmoid(g[:, HEADS + h:HEADS + h + 1])
        m_old = m_all[:, h:h + 1]
        n_old = n_ref[:, cols]
        m_new = jnp.maximum(ig, lf + m_old)
        w_in = jnp.exp(ig - m_new)
        w_st = jnp.exp(lf + m_old - m_new)
        qc = jnp.zeros((bb, HEAD_DIM), F32)
        for j in range(bb):
            c_old = c_ref[j, h]
            res = _dot(qb, c_old.astype(BF16))
            qc = jnp.where(row == j, res, qc)
            k_col = jnp.sum(jnp.where(eye, kf[j:j + 1, :], 0.0), axis=1, keepdims=True)
            c_out_ref[j, h] = w_st[j:j + 1, :] * c_old + k_col * (w_in[j:j + 1, :] * vf[j:j + 1, :])
        s = jnp.sum(qf * kf, axis=1, keepdims=True) * w_in
        num = s * vf + w_st * qc
        den = s + w_st * jnp.sum(qf * n_old, axis=1, keepdims=True)
        hh = num / jnp.maximum(jnp.abs(den), jnp.exp(-m_new))
        hh = hh * lax.rsqrt(jnp.mean(hh * hh, axis=1, keepdims=True) + EPS)
        hm_ref[:, cols] = (hh * nw_ref[:, cols] * o_ref[:, cols]).astype(BF16)
        n_out_ref[:, cols] = w_st * n_old + w_in * kf
        m_out = jnp.where(lane_h == h, m_new, m_out)
    m_out_ref[...] = m_out


def _mlstm1(q, k, v, o, g, c0, n0, m0, norm_w):
    batch = q.shape[0]
    bb = _token_block(batch, 16)
    rows = lambda i: (i, 0)
    return pl.pallas_call(
        _mlstm1_kernel,
        grid=(batch // bb,),
        in_specs=[
            pl.BlockSpec((bb, MLSTM_W), rows),
            pl.BlockSpec((bb, MLSTM_W), rows),
            pl.BlockSpec((bb, MLSTM_W), rows),
            pl.BlockSpec((bb, MLSTM_W), rows),
            pl.BlockSpec((bb, 2 * HEADS), rows),
            pl.BlockSpec((bb, HEADS, HEAD_DIM, HEAD_DIM), lambda i: (i, 0, 0, 0)),
            pl.BlockSpec((bb, MLSTM_W), rows),
            pl.BlockSpec((bb, HEADS), rows),
            pl.BlockSpec((1, MLSTM_W), lambda i: (0, 0)),
        ],
        out_specs=[
            pl.BlockSpec((bb, MLSTM_W), rows),
            pl.BlockSpec((bb, HEADS, HEAD_DIM, HEAD_DIM), lambda i: (i, 0, 0, 0)),
            pl.BlockSpec((bb, MLSTM_W), rows),
            pl.BlockSpec((bb, HEADS), rows),
        ],
        out_shape=[
            jax.ShapeDtypeStruct((batch, MLSTM_W), BF16),
            jax.ShapeDtypeStruct(c0.shape, F32),
            jax.ShapeDtypeStruct(n0.shape, F32),
            jax.ShapeDtypeStruct(m0.shape, F32),
        ],
        compiler_params=_params(("parallel",)),
        name="mlstm1",
    )(q, k, v, o, g, c0, n0, m0, norm_w)


def _pool_kernel(u_ref, wp_ref, sc_ref, z_ref, buf_ref):
    t_len = u_ref.shape[0]
    t_idx = lax.broadcasted_iota(jnp.int32, (t_len, POOL_GROUP), 0)
    for g, w in enumerate(POOL_WINDOWS):
        cols = slice(g * POOL_GROUP, (g + 1) * POOL_GROUP)
        x = u_ref[:, cols]
        s = x
        k = 1
        while k < w:
            s = s + jnp.where(t_idx >= k, pltpu.roll(s, k, axis=0), 0.0)
            k *= 2
        cnt = jnp.minimum(t_idx + 1, w).astype(F32)
        r = s / cnt - x
        z = _dot(r.astype(BF16), wp_ref[g]) * sc_ref[:, cols]
        z_ref[:, cols] = z.astype(BF16)
    buf_ref[0] = u_ref[t_len - POOL_BUF:t_len, :]


def _pool(u, w_pool, scale, batch, t_len):
    n = u.shape[0]
    return pl.pallas_call(
        _pool_kernel,
        grid=(batch,),
        in_specs=[
            pl.BlockSpec((t_len, POOL_W), lambda b: (b, 0)),
            pl.BlockSpec(w_pool.shape, lambda b: (0, 0, 0)),
            pl.BlockSpec((1, POOL_W), lambda b: (0, 0)),
        ],
        out_specs=[
            pl.BlockSpec((t_len, POOL_W), lambda b: (b, 0)),
            pl.BlockSpec((1, POOL_BUF, POOL_W), lambda b: (b, 0, 0)),
        ],
        out_shape=[
            jax.ShapeDtypeStruct((n, POOL_W), BF16),
            jax.ShapeDtypeStruct((batch, POOL_BUF, POOL_W), F32),
        ],
        compiler_params=_params(("parallel",)),
        name="pool",
    )(u, w_pool, scale)


def _pool1_kernel(u_ref, buft_ref, wp_ref, sc_ref, z_ref, buft_out_ref):
    for g, w in enumerate(POOL_WINDOWS):
        cols = slice(g * POOL_GROUP, (g + 1) * POOL_GROUP)
        x = u_ref[:, cols]
        s = x
        for j in range(1, w):
            s = s + buft_ref[POOL_BUF - j, :, cols]
        r = s / float(w) - x
        z = _dot(r.astype(BF16), wp_ref[g]) * sc_ref[:, cols]
        z_ref[:, cols] = z.astype(BF16)
    for j in range(POOL_BUF - 1):
        buft_out_ref[j] = buft_ref[j + 1]
    buft_out_ref[POOL_BUF - 1] = u_ref[...]


def _pool1(u, buf_t, w_pool, scale):
    batch = u.shape[0]
    full2 = lambda i: (0, 0)
    full3 = lambda i: (0, 0, 0)
    return pl.pallas_call(
        _pool1_kernel,
        grid=(1,),
        in_specs=[
            pl.BlockSpec(u.shape, full2),
            pl.BlockSpec(buf_t.shape, full3),
            pl.BlockSpec(w_pool.shape, full3),
            pl.BlockSpec((1, POOL_W), full2),
        ],
        out_specs=[
            pl.BlockSpec((batch, POOL_W), full2),
            pl.BlockSpec(buf_t.shape, full3),
        ],
        out_shape=[
            jax.ShapeDtypeStruct((batch, POOL_W), BF16),
            jax.ShapeDtypeStruct(buf_t.shape, F32),
        ],
        compiler_params=_params(("arbitrary",)),
        name="pool1",
    )(u, buf_t, w_pool, scale)


def _mix_kernel(x_ref, hm_ref, zp_ref, wo_ref, g_ref, x1_ref, xnt_ref):
    mix = _dot(hm_ref[...], wo_ref[0:MLSTM_W, :]) + _dot(zp_ref[...], wo_ref[MLSTM_W:, :])
    x1 = x_ref[...] + mix
    x1_ref[...] = x1
    xnt_ref[...] = _rmsnorm(x1, g_ref[...]).T.astype(BF16)


def _mix(x, hm, zp, w_out, norm_g):
    n, d = x.shape
    tb = _token_block(n, 512)
    rows = lambda i: (i, 0)
    const = lambda i: (0, 0)
    return pl.pallas_call(
        _mix_kernel,
        grid=(n // tb,),
        in_specs=[
            pl.BlockSpec((tb, d), rows),
            pl.BlockSpec((tb, MLSTM_W), rows),
            pl.BlockSpec((tb, POOL_W), rows),
            pl.BlockSpec(w_out.shape, const),
            pl.BlockSpec((1, d), const),
        ],
        out_specs=[
            pl.BlockSpec((tb, d), rows),
            pl.BlockSpec((d, tb), lambda i: (0, i)),
        ],
        out_shape=[
            jax.ShapeDtypeStruct((n, d), F32),
            jax.ShapeDtypeStruct((d, n), BF16),
        ],
        compiler_params=_params(("parallel",)),
        name="mix",
    )(x, hm, zp, w_out, norm_g)


def _extract_max(vals, pos):
    m = vals[0]
    for v in vals[1:]:
        m = jnp.maximum(m, v)
    m = jnp.max(m, axis=0, keepdims=True)
    big = jnp.float32(1e9)
    idx = None
    for v, p in zip(vals, pos):
        cand = jnp.where(v == m, p, big)
        idx = cand if idx is None else jnp.minimum(idx, cand)
    idx = jnp.min(idx, axis=0, keepdims=True)
    return m, idx


def _top16(s):
    tb = s.shape[1]
    row = lax.broadcasted_iota(jnp.int32, s.shape, 0).astype(F32)
    krow = lax.broadcasted_iota(jnp.int32, (PEER_TOPK, tb), 0)
    rank = jnp.full(s.shape, UNRANKED, F32)
    sv = jnp.zeros((PEER_TOPK, tb), F32)
    for k in range(PEER_TOPK):
        m, idx = _extract_max([s], [row])
        hit = row == idx
        rank = jnp.where(hit, float(k), rank)
        s = jnp.where(hit, NEG_INF, s)
        sv = jnp.where(krow == k, m, sv)
    return sv, rank


def _candidate_groups(sv1, sv2):
    tb = sv1.shape[1]
    vals, pos = [], []
    r8 = lax.broadcasted_iota(jnp.int32, (8, tb), 0)
    vals.append(sv1[0:1, :] + sv2[8:16, :])
    pos.append((r8 + 8).astype(F32))
    vals.append(sv1[0:1, :] + sv2[0:8, :])
    pos.append(r8.astype(F32))
    for k1 in range(1, 8):
        lim = PEER_TOPK // (k1 + 1)
        v = sv1[k1:k1 + 1, :] + sv2[0:8, :]
        vals.append(jnp.where(r8 < lim, v, NEG_INF))
        pos.append((r8 + k1 * PEER_TOPK).astype(F32))
    vals.append(sv1[8:16, :] + sv2[0:1, :])
    pos.append(((r8 + 8) * PEER_TOPK).astype(F32))
    return vals, pos


def _route_kernel(xnt_ref, wqt_ref, k1_ref, k2_ref, cnt1_ref, g1_ref, r2_ref, e2_ref):
    qt = _dot(wqt_ref[...], xnt_ref[...]).astype(BF16)
    s1 = _dot(k1_ref[0], qt[0:PEER_HALF, :])
    s2 = _dot(k2_ref[0], qt[PEER_HALF:, :])
    sv1, r1 = _top16(s1)
    sv2, r2 = _top16(s2)

    vals, pos = _candidate_groups(sv1, sv2)
    orig = list(vals)
    sel = [jnp.zeros(v.shape, F32) for v in vals]
    for _ in range(PEER_TOPK):
        _, idx = _extract_max(vals, pos)
        for i in range(len(vals)):
            hit = pos[i] == idx
            sel[i] = jnp.where(hit, 1.0, sel[i])
            vals[i] = jnp.where(hit, NEG_INF, vals[i])

    cmax = orig[1][0:1, :]
    z = None
    for o, sl in zip(orig, sel):
        part = jnp.sum(jnp.where(sl > 0.0, jnp.exp(o - cmax), 0.0), axis=0, keepdims=True)
        z = part if z is None else z + part
    cnt = [jnp.sum(sel[0] + sel[1], axis=0, keepdims=True)]
    cnt += [jnp.sum(sel[k1 + 1], axis=0, keepdims=True) for k1 in range(1, 8)]
    cnt += [sel[9][j:j + 1, :] for j in range(8)]

    cnt1 = jnp.zeros(r1.shape, F32)
    for k1 in range(PEER_TOPK):
        cnt1 = jnp.where(r1 == float(k1), cnt[k1], cnt1)
    cnt1_ref[0] = cnt1
    g1_ref[0] = jnp.exp(s1 - sv1[0:1, :]) * (0.5 / z)
    r2_ref[0] = r2.astype(BF16)
    e2_ref[0] = jnp.exp(s2 - sv2[0:1, :]).astype(BF16)


def _route(xnt, wqt, keys):
    d, n = xnt.shape
    tb = _token_block(n, 256)
    per_head = pl.BlockSpec((1, PEER_NKEYS, tb), lambda i, h: (h, 0, i))
    shape = lambda dt: jax.ShapeDtypeStruct((PEER_HEADS, PEER_NKEYS, n), dt)
    return pl.pallas_call(
        _route_kernel,
        grid=(n // tb, PEER_HEADS),
        in_specs=[
            pl.BlockSpec((d, tb), lambda i, h: (0, i)),
            pl.BlockSpec((2 * PEER_HALF, d), lambda i, h: (h, 0)),
            pl.BlockSpec((1, PEER_NKEYS, PEER_HALF), lambda i, h: (h, 0, 0)),
            pl.BlockSpec((1, PEER_NKEYS, PEER_HALF), lambda i, h: (h + PEER_HEADS, 0, 0)),
        ],
        out_specs=[per_head] * 4,
        out_shape=[shape(F32), shape(F32), shape(BF16), shape(BF16)],
        compiler_params=_params(("parallel", "parallel")),
        name="route",
    )(xnt, wqt, keys, keys)


I1_PER_TILE = 8
BF16_TILE = (8, 2 * LANE)
EXPERT_TILE = I1_PER_TILE * PEER_NKEYS


def _experts_kernel(xnt_ref, cnt1_ref, g1_ref, r2_ref, e2_ref, u_ref, vt_ref, x1_ref,
                    x2_ref, acc_ref, act_ref, wa_ref):
    e = pl.program_id(1)
    tb = xnt_ref.shape[1]

    @pl.when(e == 0)
    def _():
        acc_ref[...] = jnp.zeros_like(acc_ref)

    act_ref[...] = _dot(u_ref[...], xnt_ref[...]).astype(BF16)
    sub, width = BF16_TILE[0], min(BF16_TILE[1], tb)
    i1_rows = pl.ds(pl.multiple_of(e * I1_PER_TILE, I1_PER_TILE), I1_PER_TILE)
    for lt in range(tb // width):
        lanes = pl.ds(lt * width, width)
        cnt_tile = [cnt1_ref[h, i1_rows, lanes] for h in range(PEER_HEADS)]
        g1_tile = [g1_ref[h, i1_rows, lanes] for h in range(PEER_HEADS)]
        for j in range(I1_PER_TILE):
            cnt_b = [jnp.broadcast_to(t[j:j + 1, :], (sub, width)).astype(BF16) for t in cnt_tile]
            g1_b = [jnp.broadcast_to(t[j:j + 1, :], (sub, width)).astype(BF16) for t in g1_tile]
            for rb in range(PEER_NKEYS // sub):
                krows = pl.ds(rb * sub, sub)
                w = jnp.zeros((sub, width), BF16)
                for h in range(PEER_HEADS):
                    mask = r2_ref[h, krows, lanes] < cnt_b[h]
                    w = w + jnp.where(mask, e2_ref[h, krows, lanes] * g1_b[h], jnp.zeros((), BF16))
                erows = pl.ds(j * PEER_NKEYS + rb * sub, sub)
                a = act_ref[erows, lanes]
                th = jnp.tanh(a * (GELU_C0 + GELU_C1 * (a * a)))
                wa_ref[erows, lanes] = w * (a * (1.0 + th))
    acc_ref[...] += _dot(vt_ref[...], wa_ref[...])

    @pl.when(e == pl.num_programs(1) - 1)
    def _():
        x2_ref[...] = x1_ref[...] + acc_ref[...].T


def _experts(xnt, cnt1, g1, r2, e2, u_tab, vt_tab, x1):
    d, n = xnt.shape
    n_exp = u_tab.shape[0]
    tb = _token_block(n, 512)
    assert tb % LANE == 0 and n_exp % EXPERT_TILE == 0
    per_head = pl.BlockSpec((PEER_HEADS, PEER_NKEYS, tb), lambda i, e: (0, 0, i))
    return pl.pallas_call(
        _experts_kernel,
        grid=(n // tb, n_exp // EXPERT_TILE),
        in_specs=[
            pl.BlockSpec((d, tb), lambda i, e: (0, i)),
            per_head, per_head, per_head, per_head,
            pl.BlockSpec((EXPERT_TILE, d), lambda i, e: (e, 0)),
            pl.BlockSpec((d, EXPERT_TILE), lambda i, e: (0, e)),
            pl.BlockSpec((tb, d), lambda i, e: (i, 0)),
        ],
        out_specs=pl.BlockSpec((tb, d), lambda i, e: (i, 0)),
        out_shape=jax.ShapeDtypeStruct((n, d), F32),
        scratch_shapes=[
            pltpu.VMEM((d, tb), F32),
            pltpu.VMEM((EXPERT_TILE, tb), BF16),
            pltpu.VMEM((EXPERT_TILE, tb), BF16),
        ],
        compiler_params=_params(("parallel", "arbitrary")),
        name="experts",
    )(xnt, cnt1, g1, r2, e2, u_tab, vt_tab, x1)


def _ple_kernel(x_ref, p_ref, wg_ref, wp_ref, nf_ref, y_ref, *, final_norm):
    x = x_ref[...]
    gate = _sigmoid(_dot(x.astype(BF16), wg_ref[...]))
    y = x + gate * _dot(p_ref[...].astype(BF16), wp_ref[...])
    if final_norm:
        y = _rmsnorm(y, nf_ref[...])
    y_ref[...] = y


def _ple(x, p, w_gate, w_ple, norm_f, final_norm):
    n, d = x.shape
    tb = _token_block(n, 512)
    rows = lambda i: (i, 0)
    const = lambda i: (0, 0)
    return pl.pallas_call(
        functools.partial(_ple_kernel, final_norm=final_norm),
        grid=(n // tb,),
        in_specs=[
            pl.BlockSpec((tb, d), rows),
            pl.BlockSpec((tb, p.shape[1]), rows),
            pl.BlockSpec(w_gate.shape, const),
            pl.BlockSpec(w_ple.shape, const),
            pl.BlockSpec((1, d), const),
        ],
        out_specs=pl.BlockSpec((tb, d), rows),
        out_shape=jax.ShapeDtypeStruct((n, d), F32),
        compiler_params=_params(("parallel",)),
        name="ple",
    )(x, p, w_gate, w_ple, norm_f)


def _prep_layer_weights(l, w_in, b_gate, mlstm_norm, w_pool, pool_scale, w_out, norm1, norm2,
                        peer_wq, peer_keys, peer_u, peer_v, w_ple, w_gate):
    w = MLSTM_W
    wi = w_in[l]
    d = wi.shape[0]
    wa = jnp.concatenate([wi[:, 0:w], wi[:, 2 * w:4 * w + POOL_W]], axis=1).astype(BF16)
    wkt = wi[:, w:2 * w].T.astype(BF16)
    wgt = wi[:, 4 * w + POOL_W:].T.astype(BF16)
    return dict(
        wa=wa, wkt=wkt, wgt=wgt,
        bg=b_gate[l].reshape(2 * HEADS, 1).astype(F32),
        mlstm_norm=mlstm_norm[l].reshape(1, w),
        w_pool=w_pool[l].astype(BF16),
        pool_scale=pool_scale[l].reshape(1, POOL_W),
        w_out=w_out[l].astype(BF16),
        norm1=norm1[l].reshape(1, d), norm2=norm2[l].reshape(1, d),
        wqt=peer_wq[l].T.astype(BF16),
        keys=peer_keys[l].reshape(2 * PEER_HEADS, PEER_NKEYS, PEER_HALF).astype(BF16),
        u_tab=peer_u[l].astype(BF16),
        vt_tab=peer_v[l].T.astype(BF16),
        w_ple=w_ple[l].astype(BF16), w_gate=w_gate[l].astype(BF16),
    )


def _ffn_and_embed(x1, xnt, p, lw, norm_f, final_norm):
    cnt1, g1, r2, e2 = _route(xnt, lw["wqt"], lw["keys"])
    x2 = _experts(xnt, cnt1, g1, r2, e2, lw["u_tab"], lw["vt_tab"], x1)
    return _ple(x2, p, lw["w_gate"], lw["w_ple"], norm_f, final_norm)


def _prompt_layer(x, p, lw, norm_f, final_norm, batch, t_len):
    q, kt, v, o, u, gt = _proj(x, lw["norm1"], lw["wa"], lw["wkt"], lw["wgt"], lw["bg"])
    gates3 = gt.reshape(2 * HEADS, (batch * t_len) // CHUNK, CHUNK)
    hm, cext, m8 = _mlstm(q, kt, v, o, gates3, lw["mlstm_norm"], batch, t_len)
    zp, buf = _pool(u, lw["w_pool"], lw["pool_scale"], batch, t_len)
    x1, xnt = _mix(x, hm, zp, lw["w_out"], lw["norm2"])
    y = _ffn_and_embed(x1, xnt, p, lw, norm_f, final_norm)
    return y, cext[..., :HEAD_DIM], cext[..., HEAD_DIM], m8[..., 0, 0], buf


def _sample_layer(x, p, c0, n0, m0, buf0, lw, norm_f, final_norm):
    q, kt, v, o, u, gt = _proj(x, lw["norm1"], lw["wa"], lw["wkt"], lw["wgt"], lw["bg"])
    hm, c1, n1, m1 = _mlstm1(q, kt.T, v, o, gt.T, c0, n0.reshape(-1, MLSTM_W), m0, lw["mlstm_norm"])
    n1 = n1.reshape(n0.shape)
    zp, buf_t = _pool1(u, jnp.swapaxes(buf0, 0, 1), lw["w_pool"], lw["pool_scale"])
    x1, xnt = _mix(x, hm, zp, lw["w_out"], lw["norm2"])
    y = _ffn_and_embed(x1, xnt, p, lw, norm_f, final_norm)
    return y, c1, n1, m1, jnp.swapaxes(buf_t, 0, 1)


def kernel(x_prompt, x_sample, p_prompt, p_sample, state_C, state_n, state_m, state_pool,
           w_in, b_gate, mlstm_norm, w_pool, pool_scale, w_out, norm1, norm2,
           peer_wq, peer_keys, peer_u, peer_v, w_ple, w_gate, norm_f):
    depth = w_in.shape[0]
    batch, t_len, d = x_prompt.shape
    dec_batch, dec_len, _ = x_sample.shape
    assert dec_len == 1 and t_len % CHUNK == 0
    nf = norm_f.reshape(1, d)

    xp = x_prompt.reshape(batch * t_len, d)
    xs = x_sample.reshape(dec_batch, d)
    outs_p, outs_s = [], []
    for l in range(depth):
        lw = _prep_layer_weights(l, w_in, b_gate, mlstm_norm, w_pool, pool_scale, w_out, norm1,
                                 norm2, peer_wq, peer_keys, peer_u, peer_v, w_ple, w_gate)
        final = l == depth - 1
        pp = p_prompt[l].reshape(batch * t_len, -1)
        ps = p_sample[l].reshape(dec_batch, -1)
        xp, c_p, n_p, m_p, buf_p = _prompt_layer(xp, pp, lw, nf, final, batch, t_len)
        xs, c_s, n_s, m_s, buf_s = _sample_layer(xs, ps, state_C[l], state_n[l], state_m[l],
                                                 state_pool[l], lw, nf, final)
        outs_p.append((c_p, n_p, m_p, buf_p))
        outs_s.append((c_s, n_s, m_s, buf_s))

    stack = lambda outs, i: jnp.stack([o[i] for o in outs])
    return (xp.reshape(batch, t_len, d), xs.reshape(dec_batch, dec_len, d),
            stack(outs_p, 0), stack(outs_p, 1), stack(outs_p, 2), stack(outs_p, 3),
            stack(outs_s, 0), stack(outs_s, 1), stack(outs_s, 2), stack(outs_s, 3))
```

```python
import functools
import math

import jax
import jax.numpy as jnp
from jax import lax
from jax.experimental import pallas as pl
from jax.experimental.pallas import tpu as pltpu

F32 = jnp.float32
BF16 = jnp.bfloat16
EPS = 1e-6
NEG_INF = float("-inf")

LANE = 128
V7X_VMEM_LIMIT = 56 * 1024 * 1024

HEADS = 4
HEAD_DIM = 128
MLSTM_W = HEADS * HEAD_DIM
POOL_WINDOWS = (2, 4, 8, 16)
POOL_GROUP = 128
POOL_W = POOL_GROUP * len(POOL_WINDOWS)
POOL_BUF = max(POOL_WINDOWS) - 1
CHUNK = 128
PEER_HEADS = 8
PEER_NKEYS = 128
PEER_HALF = 128
PEER_TOPK = 16
UNRANKED = 127.0
GELU_C0 = math.sqrt(2.0 / math.pi)
GELU_C1 = 0.044715 * GELU_C0


def _params(semantics):
    return pltpu.CompilerParams(dimension_semantics=semantics, vmem_limit_bytes=V7X_VMEM_LIMIT)


def _token_block(n, want):
    tb = min(n, want)
    assert n % tb == 0
    return tb


def _rmsnorm(x, g):
    return x * lax.rsqrt(jnp.mean(x * x, axis=-1, keepdims=True) + EPS) * g


def _log_sigmoid(x):
    return jnp.minimum(x, 0.0) - jnp.log(1.0 + jnp.exp(-jnp.abs(x)))


def _sigmoid(x):
    return 1.0 / (1.0 + jnp.exp(-x))


def _dot(a, b):
    return jnp.dot(a, b, preferred_element_type=F32)


def _dot_nt(a, b):
    return lax.dot_general(a, b, (((1,), (1,)), ((), ())), preferred_element_type=F32)


def _proj_kernel(x_ref, g_ref, wa_ref, wkt_ref, wgt_ref, bg_ref,
                 q_ref, kt_ref, v_ref, o_ref, u_ref, gt_ref):
    hn = _rmsnorm(x_ref[...], g_ref[...]).astype(BF16)
    pa = _dot(hn, wa_ref[...])
    w = MLSTM_W
    q_ref[...] = pa[:, 0:w].astype(BF16)
    v_ref[...] = pa[:, w:2 * w].astype(BF16)
    o_ref[...] = _sigmoid(pa[:, 2 * w:3 * w])
    u_ref[...] = pa[:, 3 * w:4 * w]
    kt = _dot_nt(wkt_ref[...], hn) * (HEAD_DIM ** -0.5)
    kt_ref[...] = kt.astype(BF16)
    gt_ref[...] = _dot_nt(wgt_ref[...], hn) + bg_ref[...]


def _proj(x, norm_g, wa, wkt, wgt, bg):
    n, d = x.shape
    tb = _token_block(n, 512)
    grid = (n // tb,)
    const = lambda i: (0, 0)
    return pl.pallas_call(
        _proj_kernel,
        grid=grid,
        in_specs=[
            pl.BlockSpec((tb, d), lambda i: (i, 0)),
            pl.BlockSpec((1, d), const),
            pl.BlockSpec(wa.shape, const),
            pl.BlockSpec(wkt.shape, const),
            pl.BlockSpec(wgt.shape, const),
            pl.BlockSpec(bg.shape, const),
        ],
        out_specs=[
            pl.BlockSpec((tb, MLSTM_W), lambda i: (i, 0)),
            pl.BlockSpec((MLSTM_W, tb), lambda i: (0, i)),
            pl.BlockSpec((tb, MLSTM_W), lambda i: (i, 0)),
            pl.BlockSpec((tb, MLSTM_W), lambda i: (i, 0)),
            pl.BlockSpec((tb, POOL_W), lambda i: (i, 0)),
            pl.BlockSpec((2 * HEADS, tb), lambda i: (0, i)),
        ],
        out_shape=[
            jax.ShapeDtypeStruct((n, MLSTM_W), BF16),
            jax.ShapeDtypeStruct((MLSTM_W, n), BF16),
            jax.ShapeDtypeStruct((n, MLSTM_W), BF16),
            jax.ShapeDtypeStruct((n, MLSTM_W), F32),
            jax.ShapeDtypeStruct((n, POOL_W), F32),
            jax.ShapeDtypeStruct((2 * HEADS, n), F32),
        ],
        compiler_params=_params(("parallel",)),
        name="proj",
    )(x, norm_g, wa, wkt, wgt, bg)


def _lane_cumsum(x):
    lane = lax.broadcasted_iota(jnp.int32, x.shape, 1)
    shift = 1
    while shift < x.shape[1]:
        x = x + jnp.where(lane >= shift, pltpu.roll(x, shift, axis=1), 0.0)
        shift *= 2
    return x


def _mlstm_kernel(q_ref, kt_ref, v_ref, o_ref, ig_ref, lf_ref, nw_ref,
                  hm_ref, cext_out_ref, m_out_ref, cext_ref):
    t_len = q_ref.shape[0]
    n_chunks = t_len // CHUNK
    ig2 = ig_ref[0]
    lf2 = _log_sigmoid(lf_ref[0])
    b2 = _lane_cumsum(lf2)
    a2 = ig2 - b2

    t_idx = lax.broadcasted_iota(jnp.int32, (CHUNK, CHUNK), 0)
    s_idx = lax.broadcasted_iota(jnp.int32, (CHUNK, CHUNK), 1)
    causal = s_idx <= t_idx
    ones_col = (lax.broadcasted_iota(jnp.int32, (CHUNK, HEAD_DIM), 1) == 0).astype(BF16)
    nw = nw_ref[...]

    cext_ref[...] = jnp.zeros_like(cext_ref)
    m = jnp.zeros((1, 1), F32)
    for c in range(n_chunks):
        rows = pl.ds(c * CHUNK, CHUNK)
        a_row = a2[c:c + 1, :]
        lf_row = lf2[c:c + 1, :]
        b_last = b2[c:c + 1, CHUNK - 1:CHUNK]
        b_col = jnp.sum(jnp.where(causal, lf_row, 0.0), axis=1, keepdims=True)
        amax_col = jnp.max(jnp.where(causal, a_row, NEG_INF), axis=1, keepdims=True)
        m_col = jnp.maximum(amax_col, m)
        decay_mat = jnp.where(causal, jnp.exp(a_row - m_col), 0.0)

        qc = q_ref[rows, :]
        ktc = kt_ref[:, rows]
        v_ext = jnp.concatenate([v_ref[rows, :], ones_col], axis=1)
        s = _dot(qc, ktc) * decay_mat
        intra = _dot(s.astype(BF16), v_ext)
        inter = _dot(qc, cext_ref[...].astype(BF16))
        tot = intra + jnp.exp(m - m_col) * inter
        num = tot[:, :HEAD_DIM]
        den = tot[:, HEAD_DIM:HEAD_DIM + 1]
        h = num / jnp.maximum(jnp.abs(den), jnp.exp(-(b_col + m_col)))
        h = h * lax.rsqrt(jnp.mean(h * h, axis=1, keepdims=True) + EPS)
        hm_ref[rows, :] = (h * nw * o_ref[rows, :]).astype(BF16)

        m_new = jnp.maximum(b_last + m, jnp.max(a_row, axis=1, keepdims=True) + b_last)
        ws_row = jnp.exp(a_row + (b_last - m_new))
        kws = (ktc.astype(F32) * ws_row).astype(BF16)
        cext_ref[...] = jnp.exp(b_last + m - m_new) * cext_ref[...] + _dot(kws, v_ext)
        m = m_new

    cext_out_ref[0, 0] = cext_ref[...]
    m_out_ref[0, 0] = jnp.broadcast_to(m, m_out_ref.shape[2:])


def _mlstm(q, kt, v, o, gates3, norm_w, batch, t_len):
    n = q.shape[0]
    n_chunks = t_len // CHUNK
    seq = lambda b, h: (b, h)
    return pl.pallas_call(
        _mlstm_kernel,
        grid=(batch, HEADS),
        in_specs=[
            pl.BlockSpec((t_len, HEAD_DIM), seq),
            pl.BlockSpec((HEAD_DIM, t_len), lambda b, h: (h, b)),
            pl.BlockSpec((t_len, HEAD_DIM), seq),
            pl.BlockSpec((t_len, HEAD_DIM), seq),
            pl.BlockSpec((1, n_chunks, CHUNK), lambda b, h: (h, b, 0)),
            pl.BlockSpec((1, n_chunks, CHUNK), lambda b, h: (h + HEADS, b, 0)),
            pl.BlockSpec((1, HEAD_DIM), lambda b, h: (0, h)),
        ],
        out_specs=[
            pl.BlockSpec((t_len, HEAD_DIM), seq),
            pl.BlockSpec((1, 1, HEAD_DIM, 2 * HEAD_DIM), lambda b, h: (b, h, 0, 0)),
            pl.BlockSpec((1, 1, 8, LANE), lambda b, h: (b, h, 0, 0)),
        ],
        out_shape=[
            jax.ShapeDtypeStruct((n, MLSTM_W), BF16),
            jax.ShapeDtypeStruct((batch, HEADS, HEAD_DIM, 2 * HEAD_DIM), F32),
            jax.ShapeDtypeStruct((batch, HEADS, 8, LANE), F32),
        ],
        scratch_shapes=[pltpu.VMEM((HEAD_DIM, 2 * HEAD_DIM), F32)],
        compiler_params=_params(("parallel", "parallel")),
        name="mlstm",
    )(q, kt, v, o, gates3, gates3, norm_w)


def _mlstm1_kernel(q_ref, k_ref, v_ref, o_ref, g_ref, c_ref, n_ref, m_ref, nw_ref,
                   hm_ref, c_out_ref, n_out_ref, m_out_ref):
    bb = q_ref.shape[0]
    row = lax.broadcasted_iota(jnp.int32, (bb, HEAD_DIM), 0)
    eye = (lax.broadcasted_iota(jnp.int32, (HEAD_DIM, HEAD_DIM), 0)
           == lax.broadcasted_iota(jnp.int32, (HEAD_DIM, HEAD_DIM), 1))
    lane_h = lax.broadcasted_iota(jnp.int32, (bb, HEADS), 1)
    g = g_ref[...]
    m_all = m_ref[...]
    m_out = jnp.zeros((bb, HEADS), F32)
    for h in range(HEADS):
        cols = slice(h * HEAD_DIM, (h + 1) * HEAD_DIM)
        qb = q_ref[:, cols]
        qf = qb.astype(F32)
        kf = k_ref[:, cols].astype(F32)
        vf = v_ref[:, cols].astype(F32)
        ig = g[:, h:h + 1]
        lf = _log_sigmoid(g[:, HEADS + h:HEADS + h + 1])
        m_old = m_all[:, h:h + 1]
        n_old = n_ref[:, cols]
        m_new = jnp.maximum(ig, lf + m_old)
        w_in = jnp.exp(ig - m_new)
        w_st = jnp.exp(lf + m_old - m_new)
        qc = jnp.zeros((bb, HEAD_DIM), F32)
        for j in range(bb):
            c_old = c_ref[j, h]
            res = _dot(qb, c_old.astype(BF16))
            qc = jnp.where(row == j, res, qc)
            k_col = jnp.sum(jnp.where(eye, kf[j:j + 1, :], 0.0), axis=1, keepdims=True)
            c_out_ref[j, h] = w_st[j:j + 1, :] * c_old + k_col * (w_in[j:j + 1, :] * vf[j:j + 1, :])
        s = jnp.sum(qf * kf, axis=1, keepdims=True) * w_in
        num = s * vf + w_st * qc
        den = s + w_st * jnp.sum(qf * n_old, axis=1, keepdims=True)
        hh = num / jnp.maximum(jnp.abs(den), jnp.exp(-m_new))
        hh = hh * lax.rsqrt(jnp.mean(hh * hh, axis=1, keepdims=True) + EPS)
        hm_ref[:, cols] = (hh * nw_ref[:, cols] * o_ref[:, cols]).astype(BF16)
        n_out_ref[:, cols] = w_st * n_old + w_in * kf
        m_out = jnp.where(lane_h == h, m_new, m_out)
    m_out_ref[...] = m_out


def _mlstm1(q, k, v, o, g, c0, n0, m0, norm_w):
    batch = q.shape[0]
    bb = _token_block(batch, 16)
    rows = lambda i: (i, 0)
    return pl.pallas_call(
        _mlstm1_kernel,
        grid=(batch // bb,),
        in_specs=[
            pl.BlockSpec((bb, MLSTM_W), rows),
            pl.BlockSpec((bb, MLSTM_W), rows),
            pl.BlockSpec((bb, MLSTM_W), rows),
            pl.BlockSpec((bb, MLSTM_W), rows),
            pl.BlockSpec((bb, 2 * HEADS), rows),
            pl.BlockSpec((bb, HEADS, HEAD_DIM, HEAD_DIM), lambda i: (i, 0, 0, 0)),
            pl.BlockSpec((bb, MLSTM_W), rows),
            pl.BlockSpec((bb, HEADS), rows),
            pl.BlockSpec((1, MLSTM_W), lambda i: (0, 0)),
        ],
        out_specs=[
            pl.BlockSpec((bb, MLSTM_W), rows),
            pl.BlockSpec((bb, HEADS, HEAD_DIM, HEAD_DIM), lambda i: (i, 0, 0, 0)),
            pl.BlockSpec((bb, MLSTM_W), rows),
            pl.BlockSpec((bb, HEADS), rows),
        ],
        out_shape=[
            jax.ShapeDtypeStruct((batch, MLSTM_W), BF16),
            jax.ShapeDtypeStruct(c0.shape, F32),
            jax.ShapeDtypeStruct(n0.shape, F32),
            jax.ShapeDtypeStruct(m0.shape, F32),
        ],
        compiler_params=_params(("parallel",)),
        name="mlstm1",
    )(q, k, v, o, g, c0, n0, m0, norm_w)


def _pool_kernel(u_ref, wp_ref, sc_ref, z_ref, buf_ref):
    t_len = u_ref.shape[0]
    t_idx = lax.broadcasted_iota(jnp.int32, (t_len, POOL_GROUP), 0)
    for g, w in enumerate(POOL_WINDOWS):
        cols = slice(g * POOL_GROUP, (g + 1) * POOL_GROUP)
        x = u_ref[:, cols]
        s = x
        k = 1
        while k < w:
            s = s + jnp.where(t_idx >= k, pltpu.roll(s, k, axis=0), 0.0)
            k *= 2
        cnt = jnp.minimum(t_idx + 1, w).astype(F32)
        r = s / cnt - x
        z = _dot(r.astype(BF16), wp_ref[g]) * sc_ref[:, cols]
        z_ref[:, cols] = z.astype(BF16)
    buf_ref[0] = u_ref[t_len - POOL_BUF:t_len, :]


def _pool(u, w_pool, scale, batch, t_len):
    n = u.shape[0]
    return pl.pallas_call(
        _pool_kernel,
        grid=(batch,),
        in_specs=[
            pl.BlockSpec((t_len, POOL_W), lambda b: (b, 0)),
            pl.BlockSpec(w_pool.shape, lambda b: (0, 0, 0)),
            pl.BlockSpec((1, POOL_W), lambda b: (0, 0)),
        ],
        out_specs=[
            pl.BlockSpec((t_len, POOL_W), lambda b: (b, 0)),
            pl.BlockSpec((1, POOL_BUF, POOL_W), lambda b: (b, 0, 0)),
        ],
        out_shape=[
            jax.ShapeDtypeStruct((n, POOL_W), BF16),
            jax.ShapeDtypeStruct((batch, POOL_BUF, POOL_W), F32),
        ],
        compiler_params=_params(("parallel",)),
        name="pool",
    )(u, w_pool, scale)


def _pool1_kernel(u_ref, buft_ref, wp_ref, sc_ref, z_ref, buft_out_ref):
    for g, w in enumerate(POOL_WINDOWS):
        cols = slice(g * POOL_GROUP, (g + 1) * POOL_GROUP)
        x = u_ref[:, cols]
        s = x
        for j in range(1, w):
            s = s + buft_ref[POOL_BUF - j, :, cols]
        r = s / float(w) - x
        z = _dot(r.astype(BF16), wp_ref[g]) * sc_ref[:, cols]
        z_ref[:, cols] = z.astype(BF16)
    for j in range(POOL_BUF - 1):
        buft_out_ref[j] = buft_ref[j + 1]
    buft_out_ref[POOL_BUF - 1] = u_ref[...]


def _pool1(u, buf_t, w_pool, scale):
    batch = u.shape[0]
    full2 = lambda i: (0, 0)
    full3 = lambda i: (0, 0, 0)
    return pl.pallas_call(
        _pool1_kernel,
        grid=(1,),
        in_specs=[
            pl.BlockSpec(u.shape, full2),
            pl.BlockSpec(buf_t.shape, full3),
            pl.BlockSpec(w_pool.shape, full3),
            pl.BlockSpec((1, POOL_W), full2),
        ],
        out_specs=[
            pl.BlockSpec((batch, POOL_W), full2),
            pl.BlockSpec(buf_t.shape, full3),
        ],
        out_shape=[
            jax.ShapeDtypeStruct((batch, POOL_W), BF16),
            jax.ShapeDtypeStruct(buf_t.shape, F32),
        ],
        compiler_params=_params(("arbitrary",)),
        name="pool1",
    )(u, buf_t, w_pool, scale)


def _mix_kernel(x_ref, hm_ref, zp_ref, wo_ref, g_ref, x1_ref, xnt_ref):
    mix = _dot(hm_ref[...], wo_ref[0:MLSTM_W, :]) + _dot(zp_ref[...], wo_ref[MLSTM_W:, :])
    x1 = x_ref[...] + mix
    x1_ref[...] = x1
    xnt_ref[...] = _rmsnorm(x1, g_ref[...]).T.astype(BF16)


def _mix(x, hm, zp, w_out, norm_g):
    n, d = x.shape
    tb = _token_block(n, 512)
    rows = lambda i: (i, 0)
    const = lambda i: (0, 0)
    return pl.pallas_call(
        _mix_kernel,
        grid=(n // tb,),
        in_specs=[
            pl.BlockSpec((tb, d), rows),
            pl.BlockSpec((tb, MLSTM_W), rows),
            pl.BlockSpec((tb, POOL_W), rows),
            pl.BlockSpec(w_out.shape, const),
            pl.BlockSpec((1, d), const),
        ],
        out_specs=[
            pl.BlockSpec((tb, d), rows),
            pl.BlockSpec((d, tb), lambda i: (0, i)),
        ],
        out_shape=[
            jax.ShapeDtypeStruct((n, d), F32),
            jax.ShapeDtypeStruct((d, n), BF16),
        ],
        compiler_params=_params(("parallel",)),
        name="mix",
    )(x, hm, zp, w_out, norm_g)


def _extract_max(vals, pos):
    m = vals[0]
    for v in vals[1:]:
        m = jnp.maximum(m, v)
    m = jnp.max(m, axis=0, keepdims=True)
    big = jnp.float32(1e9)
    idx = None
    for v, p in zip(vals, pos):
        cand = jnp.where(v == m, p, big)
        idx = cand if idx is None else jnp.minimum(idx, cand)
    idx = jnp.min(idx, axis=0, keepdims=True)
    return m, idx


def _top16(s):
    tb = s.shape[1]
    row = lax.broadcasted_iota(jnp.int32, s.shape, 0).astype(F32)
    krow = lax.broadcasted_iota(jnp.int32, (PEER_TOPK, tb), 0)
    rank = jnp.full(s.shape, UNRANKED, F32)
    sv = jnp.zeros((PEER_TOPK, tb), F32)
    for k in range(PEER_TOPK):
        m, idx = _extract_max([s], [row])
        hit = row == idx
        rank = jnp.where(hit, float(k), rank)
        s = jnp.where(hit, NEG_INF, s)
        sv = jnp.where(krow == k, m, sv)
    return sv, rank


def _candidate_groups(sv1, sv2):
    tb = sv1.shape[1]
    vals, pos = [], []
    r8 = lax.broadcasted_iota(jnp.int32, (8, tb), 0)
    vals.append(sv1[0:1, :] + sv2[8:16, :])
    pos.append((r8 + 8).astype(F32))
    vals.append(sv1[0:1, :] + sv2[0:8, :])
    pos.append(r8.astype(F32))
    for k1 in range(1, 8):
        lim = PEER_TOPK // (k1 + 1)
        v = sv1[k1:k1 + 1, :] + sv2[0:8, :]
        vals.append(jnp.where(r8 < lim, v, NEG_INF))
        pos.append((r8 + k1 * PEER_TOPK).astype(F32))
    vals.append(sv1[8:16, :] + sv2[0:1, :])
    pos.append(((r8 + 8) * PEER_TOPK).astype(F32))
    return vals, pos


def _sorting_pairs(n):
    pairs = []
    p = 1
    while p < n:
        k = p
        while k >= 1:
            for j in range(k % p, n - k, 2 * k):
                for i in range(min(k, n - j - k)):
                    if (i + j) // (2 * p) == (i + j + k) // (2 * p):
                        pairs.append((i + j, i + j + k))
            k //= 2
        p *= 2
    return pairs


def _sublane_allreduce(x, op):
    for d in (1, 2, 4):
        x = op(x, pltpu.roll(x, d, axis=0))
    return x


def _sorted_top16(groups):
    g = list(groups)
    for i, j in _sorting_pairs(len(g)):
        g[i], g[j] = jnp.maximum(g[i], g[j]), jnp.minimum(g[i], g[j])
    n = len(g)
    for d in (1, 2, 4):
        p = [pltpu.roll(x, d, axis=0) for x in g]
        g = [jnp.maximum(g[i], p[n - 1 - i]) for i in range(n)]
        stride = n // 2
        while stride >= 1:
            for i in range(n):
                if i & stride == 0:
                    g[i], g[i + stride] = (jnp.maximum(g[i], g[i + stride]),
                                           jnp.minimum(g[i], g[i + stride]))
            stride //= 2
    return g


def _rank_bits(x, sv):
    c8 = x < sv[7]
    t = jnp.where(c8, sv[11], sv[3])
    c4 = x < t
    t = jnp.where(c8, jnp.where(c4, sv[13], sv[9]), jnp.where(c4, sv[5], sv[1]))
    c2 = x < t
    t = jnp.where(c8,
                  jnp.where(c4, jnp.where(c2, sv[14], sv[12]), jnp.where(c2, sv[10], sv[8])),
                  jnp.where(c4, jnp.where(c2, sv[6], sv[4]), jnp.where(c2, sv[2], sv[0])))
    c1 = x < t
    return (c8, c4, c2, c1), x < sv[15]


def _select16(bits, rows):
    c8, c4, c2, c1 = bits
    lvl = [jnp.where(c1, rows[2 * i + 1], rows[2 * i]) for i in range(8)]
    lvl = [jnp.where(c2, lvl[2 * i + 1], lvl[2 * i]) for i in range(4)]
    lvl = [jnp.where(c4, lvl[2 * i + 1], lvl[2 * i]) for i in range(2)]
    return jnp.where(c8, lvl[1], lvl[0])


def _route_fast(s1, s2):
    tb = s1.shape[1]
    r8 = lax.broadcasted_iota(jnp.int32, (8, tb), 0)
    n_grp = PEER_NKEYS // 8
    g1 = [s1[v * 8:(v + 1) * 8, :] for v in range(n_grp)]
    g2 = [s2[v * 8:(v + 1) * 8, :] for v in range(n_grp)]
    sv1 = _sorted_top16(g1)
    sv2 = _sorted_top16(g2)

    bad = jnp.zeros((8, tb), F32)
    for sv, grp in ((sv1, g1), (sv2, g2)):
        for k in range(PEER_TOPK - 1):
            bad = jnp.where(sv[k] == sv[k + 1], 1.0, bad)
        n_in = jnp.zeros((8, tb), F32)
        for x in grp:
            n_in = n_in + jnp.where(x >= sv[PEER_TOPK - 1], 1.0, 0.0)
        n_in = _sublane_allreduce(n_in, jnp.add)
        bad = jnp.where(n_in != float(PEER_TOPK), 1.0, bad)

    def by_sublane(rows):
        out = jnp.zeros((8, tb), F32)
        for k, row in enumerate(rows):
            out = jnp.where(r8 == k, row, out)
        return out
    a2_lo, a2_hi, a1_hi = by_sublane(sv2[0:8]), by_sublane(sv2[8:16]), by_sublane(sv1[8:16])
    orig = [sv1[0] + a2_hi, sv1[0] + a2_lo]
    for k1 in range(1, 8):
        orig.append(jnp.where(r8 < PEER_TOPK // (k1 + 1), sv1[k1] + a2_lo, NEG_INF))
    orig.append(a1_hi + sv2[0])
    vals = list(orig)
    prev = None
    for _ in range(PEER_TOPK):
        m = vals[0]
        for v in vals[1:]:
            m = jnp.maximum(m, v)
        m = _sublane_allreduce(m, jnp.maximum)
        vals = [jnp.where(v == m, NEG_INF, v) for v in vals]
        if prev is not None:
            bad = jnp.where(m == prev, 1.0, bad)
        prev = m
    sel = [o >= prev for o in orig]
    cmax = sv1[0] + sv2[0]
    z = jnp.zeros((8, tb), F32)
    for o, sl in zip(orig, sel):
        z = z + jnp.where(sl, jnp.exp(o - cmax), 0.0)
    z = _sublane_allreduce(z, jnp.add)
    ones = [jnp.where(sl, 1.0, 0.0) for sl in sel]
    cnt = [_sublane_allreduce(ones[0] + ones[1], jnp.add)]
    cnt += [_sublane_allreduce(ones[k1 + 1], jnp.add) for k1 in range(1, 8)]
    cnt += [ones[9][j:j + 1, :] for j in range(8)]
    total = cnt[0]
    for c in cnt[1:]:
        total = total + c
    bad = jnp.where(total != float(PEER_TOPK), 1.0, bad)

    cnt1, r2 = [], []
    weights = (8.0, 4.0, 2.0, 1.0)
    for x in g1:
        bits, below = _rank_bits(x, sv1)
        cnt1.append(jnp.where(below, 0.0, _select16(bits, cnt)))
    for x in g2:
        bits, below = _rank_bits(x, sv2)
        rank = jnp.zeros((8, tb), F32)
        for b, wgt in zip(bits, weights):
            rank = rank + jnp.where(b, wgt, 0.0)
        r2.append(jnp.where(below, UNRANKED, rank))
    return cnt1, r2, sv1[0][0:1, :], sv2[0][0:1, :], z[0:1, :], bad


def _route_exact(s1, s2):
    sv1, r1 = _top16(s1)
    sv2, r2 = _top16(s2)

    vals, pos = _candidate_groups(sv1, sv2)
    orig = list(vals)
    sel = [jnp.zeros(v.shape, F32) for v in vals]
    for _ in range(PEER_TOPK):
        _, idx = _extract_max(vals, pos)
        for i in range(len(vals)):
            hit = pos[i] == idx
            sel[i] = jnp.where(hit, 1.0, sel[i])
            vals[i] = jnp.where(hit, NEG_INF, vals[i])

    cmax = orig[1][0:1, :]
    z = None
    for o, sl in zip(orig, sel):
        part = jnp.sum(jnp.where(sl > 0.0, jnp.exp(o - cmax), 0.0), axis=0, keepdims=True)
        z = part if z is None else z + part
    cnt = [jnp.sum(sel[0] + sel[1], axis=0, keepdims=True)]
    cnt += [jnp.sum(sel[k1 + 1], axis=0, keepdims=True) for k1 in range(1, 8)]
    cnt += [sel[9][j:j + 1, :] for j in range(8)]

    cnt1 = jnp.zeros(r1.shape, F32)
    for k1 in range(PEER_TOPK):
        cnt1 = jnp.where(r1 == float(k1), cnt[k1], cnt1)
    return cnt1, r2, sv1[0:1, :], sv2[0:1, :], z


def _route_kernel(xnt_ref, wqt_ref, k1_ref, k2_ref, cnt1_ref, g1_ref, r2_ref, e2_ref):
    qt = _dot(wqt_ref[...], xnt_ref[...]).astype(BF16)
    s1 = _dot(k1_ref[0], qt[0:PEER_HALF, :])
    s2 = _dot(k2_ref[0], qt[PEER_HALF:, :])

    def emit_gates(max1, max2, z):
        g1_ref[0] = jnp.exp(s1 - max1) * (0.5 / z)
        e2_ref[0] = jnp.exp(s2 - max2).astype(BF16)

    cnt1, r2, max1, max2, z, tie = _route_fast(s1, s2)
    for v in range(PEER_NKEYS // 8):
        cnt1_ref[0, v * 8:(v + 1) * 8, :] = cnt1[v]
        r2_ref[0, v * 8:(v + 1) * 8, :] = r2[v].astype(BF16)
    emit_gates(max1, max2, z)

    @pl.when(jnp.max(tie) > 0.0)
    def _():
        cnt1, r2, max1, max2, z = _route_exact(s1, s2)
        cnt1_ref[0] = cnt1
        r2_ref[0] = r2.astype(BF16)
        emit_gates(max1, max2, z)


def _route(xnt, wqt, keys):
    d, n = xnt.shape
    tb = _token_block(n, 256)
    per_head = pl.BlockSpec((1, PEER_NKEYS, tb), lambda i, h: (h, 0, i))
    shape = lambda dt: jax.ShapeDtypeStruct((PEER_HEADS, PEER_NKEYS, n), dt)
    return pl.pallas_call(
        _route_kernel,
        grid=(n // tb, PEER_HEADS),
        in_specs=[
            pl.BlockSpec((d, tb), lambda i, h: (0, i)),
            pl.BlockSpec((2 * PEER_HALF, d), lambda i, h: (h, 0)),
            pl.BlockSpec((1, PEER_NKEYS, PEER_HALF), lambda i, h: (h, 0, 0)),
            pl.BlockSpec((1, PEER_NKEYS, PEER_HALF), lambda i, h: (h + PEER_HEADS, 0, 0)),
        ],
        out_specs=[per_head] * 4,
        out_shape=[shape(F32), shape(F32), shape(BF16), shape(BF16)],
        compiler_params=_params(("parallel", "parallel")),
        name="route",
    )(xnt, wqt, keys, keys)


I1_PER_TILE = 8
BF16_TILE = (8, 2 * LANE)
EXPERT_TILE = I1_PER_TILE * PEER_NKEYS


def _experts_kernel(xnt_ref, cnt1_ref, g1_ref, r2_ref, e2_ref, u_ref, vt_ref, x1_ref,
                    x2_ref, acc_ref, act_ref, wa_ref):
    e = pl.program_id(1)
    tb = xnt_ref.shape[1]

    @pl.when(e == 0)
    def _():
        acc_ref[...] = jnp.zeros_like(acc_ref)

    act_ref[...] = _dot(u_ref[...], xnt_ref[...]).astype(BF16)
    sub, width = BF16_TILE[0], min(BF16_TILE[1], tb)
    i1_rows = pl.ds(pl.multiple_of(e * I1_PER_TILE, I1_PER_TILE), I1_PER_TILE)
    for lt in range(tb // width):
        lanes = pl.ds(lt * width, width)
        cnt_tile = [cnt1_ref[h, i1_rows, lanes] for h in range(PEER_HEADS)]
        g1_tile = [g1_ref[h, i1_rows, lanes] for h in range(PEER_HEADS)]
        for j in range(I1_PER_TILE):
            cnt_b = [jnp.broadcast_to(t[j:j + 1, :], (sub, width)).astype(BF16) for t in cnt_tile]
            g1_b = [jnp.broadcast_to(t[j:j + 1, :], (sub, width)).astype(BF16) for t in g1_tile]
            for rb in range(PEER_NKEYS // sub):
                krows = pl.ds(rb * sub, sub)
                w = jnp.zeros((sub, width), BF16)
                for h in range(PEER_HEADS):
                    mask = r2_ref[h, krows, lanes] < cnt_b[h]
                    w = w + jnp.where(mask, e2_ref[h, krows, lanes] * g1_b[h], jnp.zeros((), BF16))
                erows = pl.ds(j * PEER_NKEYS + rb * sub, sub)
                a = act_ref[erows, lanes]
                th = jnp.tanh(a * (GELU_C0 + GELU_C1 * (a * a)))
                wa_ref[erows, lanes] = w * (a * (1.0 + th))
    acc_ref[...] += _dot(vt_ref[...], wa_ref[...])

    @pl.when(e == pl.num_programs(1) - 1)
    def _():
        x2_ref[...] = x1_ref[...] + acc_ref[...].T


def _experts(xnt, cnt1, g1, r2, e2, u_tab, vt_tab, x1):
    d, n = xnt.shape
    n_exp = u_tab.shape[0]
    tb = _token_block(n, 512)
    assert tb % LANE == 0 and n_exp % EXPERT_TILE == 0
    per_head = pl.BlockSpec((PEER_HEADS, PEER_NKEYS, tb), lambda i, e: (0, 0, i))
    return pl.pallas_call(
        _experts_kernel,
        grid=(n // tb, n_exp // EXPERT_TILE),
        in_specs=[
            pl.BlockSpec((d, tb), lambda i, e: (0, i)),
            per_head, per_head, per_head, per_head,
            pl.BlockSpec((EXPERT_TILE, d), lambda i, e: (e, 0)),
            pl.BlockSpec((d, EXPERT_TILE), lambda i, e: (0, e)),
            pl.BlockSpec((tb, d), lambda i, e: (i, 0)),
        ],
        out_specs=pl.BlockSpec((tb, d), lambda i, e: (i, 0)),
        out_shape=jax.ShapeDtypeStruct((n, d), F32),
        scratch_shapes=[
            pltpu.VMEM((d, tb), F32),
            pltpu.VMEM((EXPERT_TILE, tb), BF16),
            pltpu.VMEM((EXPERT_TILE, tb), BF16),
        ],
        compiler_params=_params(("parallel", "arbitrary")),
        name="experts",
    )(xnt, cnt1, g1, r2, e2, u_tab, vt_tab, x1)


def _ple_kernel(x_ref, p_ref, wg_ref, wp_ref, nf_ref, y_ref, *, final_norm):
    x = x_ref[...]
    gate = _sigmoid(_dot(x.astype(BF16), wg_ref[...]))
    y = x + gate * _dot(p_ref[...].astype(BF16), wp_ref[...])
    if final_norm:
        y = _rmsnorm(y, nf_ref[...])
    y_ref[...] = y


def _ple(x, p, w_gate, w_ple, norm_f, final_norm):
    n, d = x.shape
    tb = _token_block(n, 512)
    rows = lambda i: (i, 0)
    const = lambda i: (0, 0)
    return pl.pallas_call(
        functools.partial(_ple_kernel, final_norm=final_norm),
        grid=(n // tb,),
        in_specs=[
            pl.BlockSpec((tb, d), rows),
            pl.BlockSpec((tb, p.shape[1]), rows),
            pl.BlockSpec(w_gate.shape, const),
            pl.BlockSpec(w_ple.shape, const),
            pl.BlockSpec((1, d), const),
        ],
        out_specs=pl.BlockSpec((tb, d), rows),
        out_shape=jax.ShapeDtypeStruct((n, d), F32),
        compiler_params=_params(("parallel",)),
        name="ple",
    )(x, p, w_gate, w_ple, norm_f)


def _prep_layer_weights(l, w_in, b_gate, mlstm_norm, w_pool, pool_scale, w_out, norm1, norm2,
                        peer_wq, peer_keys, peer_u, peer_v, w_ple, w_gate):
    w = MLSTM_W
    wi = w_in[l]
    d = wi.shape[0]
    wa = jnp.concatenate([wi[:, 0:w], wi[:, 2 * w:4 * w + POOL_W]], axis=1).astype(BF16)
    wkt = wi[:, w:2 * w].T.astype(BF16)
    wgt = wi[:, 4 * w + POOL_W:].T.astype(BF16)
    return dict(
        wa=wa, wkt=wkt, wgt=wgt,
        bg=b_gate[l].reshape(2 * HEADS, 1).astype(F32),
        mlstm_norm=mlstm_norm[l].reshape(1, w),
        w_pool=w_pool[l].astype(BF16),
        pool_scale=pool_scale[l].reshape(1, POOL_W),
        w_out=w_out[l].astype(BF16),
        norm1=norm1[l].reshape(1, d), norm2=norm2[l].reshape(1, d),
        wqt=peer_wq[l].T.astype(BF16),
        keys=peer_keys[l].reshape(2 * PEER_HEADS, PEER_NKEYS, PEER_HALF).astype(BF16),
        u_tab=peer_u[l].astype(BF16),
        vt_tab=peer_v[l].T.astype(BF16),
        w_ple=w_ple[l].astype(BF16), w_gate=w_gate[l].astype(BF16),
    )


def _ffn_and_embed(x1, xnt, p, lw, norm_f, final_norm):
    cnt1, g1, r2, e2 = _route(xnt, lw["wqt"], lw["keys"])
    x2 = _experts(xnt, cnt1, g1, r2, e2, lw["u_tab"], lw["vt_tab"], x1)
    return _ple(x2, p, lw["w_gate"], lw["w_ple"], norm_f, final_norm)


def _prompt_layer(x, p, lw, norm_f, final_norm, batch, t_len):
    q, kt, v, o, u, gt = _proj(x, lw["norm1"], lw["wa"], lw["wkt"], lw["wgt"], lw["bg"])
    gates3 = gt.reshape(2 * HEADS, (batch * t_len) // CHUNK, CHUNK)
    hm, cext, m8 = _mlstm(q, kt, v, o, gates3, lw["mlstm_norm"], batch, t_len)
    zp, buf = _pool(u, lw["w_pool"], lw["pool_scale"], batch, t_len)
    x1, xnt = _mix(x, hm, zp, lw["w_out"], lw["norm2"])
    y = _ffn_and_embed(x1, xnt, p, lw, norm_f, final_norm)
    return y, cext[..., :HEAD_DIM], cext[..., HEAD_DIM], m8[..., 0, 0], buf


def _sample_layer(x, p, c0, n0, m0, buf0, lw, norm_f, final_norm):
    q, kt, v, o, u, gt = _proj(x, lw["norm1"], lw["wa"], lw["wkt"], lw["wgt"], lw["bg"])
    hm, c1, n1, m1 = _mlstm1(q, kt.T, v, o, gt.T, c0, n0.reshape(-1, MLSTM_W), m0, lw["mlstm_norm"])
    n1 = n1.reshape(n0.shape)
    zp, buf_t = _pool1(u, jnp.swapaxes(buf0, 0, 1), lw["w_pool"], lw["pool_scale"])
    x1, xnt = _mix(x, hm, zp, lw["w_out"], lw["norm2"])
    y = _ffn_and_embed(x1, xnt, p, lw, norm_f, final_norm)
    return y, c1, n1, m1, jnp.swapaxes(buf_t, 0, 1)


def kernel(x_prompt, x_sample, p_prompt, p_sample, state_C, state_n, state_m, state_pool,
           w_in, b_gate, mlstm_norm, w_pool, pool_scale, w_out, norm1, norm2,
           peer_wq, peer_keys, peer_u, peer_v, w_ple, w_gate, norm_f):
    depth = w_in.shape[0]
    batch, t_len, d = x_prompt.shape
    dec_batch, dec_len, _ = x_sample.shape
    assert dec_len == 1 and t_len % CHUNK == 0
    nf = norm_f.reshape(1, d)

    xp = x_prompt.reshape(batch * t_len, d)
    xs = x_sample.reshape(dec_batch, d)
    outs_p, outs_s = [], []
    for l in range(depth):
        lw = _prep_layer_weights(l, w_in, b_gate, mlstm_norm, w_pool, pool_scale, w_out, norm1,
                                 norm2, peer_wq, peer_keys, peer_u, peer_v, w_ple, w_gate)
        final = l == depth - 1
        pp = p_prompt[l].reshape(batch * t_len, -1)
        ps = p_sample[l].reshape(dec_batch, -1)
        xp, c_p, n_p, m_p, buf_p = _prompt_layer(xp, pp, lw, nf, final, batch, t_len)
        xs, c_s, n_s, m_s, buf_s = _sample_layer(xs, ps, state_C[l], state_n[l], state_m[l],
                                                 state_pool[l], lw, nf, final)
        outs_p.append((c_p, n_p, m_p, buf_p))
        outs_s.append((c_s, n_s, m_s, buf_s))

    stack = lambda outs, i: jnp.stack([o[i] for o in outs])
    return (xp.reshape(batch, t_len, d), xs.reshape(dec_batch, dec_len, d),
            stack(outs_p, 0), stack(outs_p, 1), stack(outs_p, 2), stack(outs_p, 3),
            stack(outs_s, 0), stack(outs_s, 1), stack(outs_s, 2), stack(outs_s, 3))
```

```python
import functools
import math

import jax
import jax.numpy as jnp
from jax import lax
from jax.experimental import pallas as pl
from jax.experimental.pallas import tpu as pltpu

F32 = jnp.float32
BF16 = jnp.bfloat16
EPS = 1e-6
NEG_INF = float("-inf")

LANE = 128
V7X_VMEM_LIMIT = 56 * 1024 * 1024

HEADS = 4
HEAD_DIM = 128
MLSTM_W = HEADS * HEAD_DIM
POOL_WINDOWS = (2, 4, 8, 16)
POOL_GROUP = 128
POOL_W = POOL_GROUP * len(POOL_WINDOWS)
POOL_BUF = max(POOL_WINDOWS) - 1
CHUNK = 128
PEER_HEADS = 8
PEER_NKEYS = 128
PEER_HALF = 128
PEER_TOPK = 16
UNRANKED = 127.0
GELU_C0 = math.sqrt(2.0 / math.pi)
GELU_C1 = 0.044715 * GELU_C0


def _params(semantics):
    return pltpu.CompilerParams(dimension_semantics=semantics, vmem_limit_bytes=V7X_VMEM_LIMIT)


def _token_block(n, want):
    tb = min(n, want)
    assert n % tb == 0
    return tb


def _rmsnorm(x, g):
    return x * lax.rsqrt(jnp.mean(x * x, axis=-1, keepdims=True) + EPS) * g


def _log_sigmoid(x):
    return jnp.minimum(x, 0.0) - jnp.log(1.0 + jnp.exp(-jnp.abs(x)))


def _sigmoid(x):
    return 1.0 / (1.0 + jnp.exp(-x))


def _dot(a, b):
    return jnp.dot(a, b, preferred_element_type=F32)


def _dot_nt(a, b):
    return lax.dot_general(a, b, (((1,), (1,)), ((), ())), preferred_element_type=F32)


def _proj_kernel(x_ref, g_ref, wa_ref, wkt_ref, wgt_ref, bg_ref,
                 q_ref, kt_ref, v_ref, o_ref, u_ref, gt_ref):
    hn = _rmsnorm(x_ref[...], g_ref[...]).astype(BF16)
    pa = _dot(hn, wa_ref[...])
    w = MLSTM_W
    q_ref[...] = pa[:, 0:w].astype(BF16)
    v_ref[...] = pa[:, w:2 * w].astype(BF16)
    o_ref[...] = _sigmoid(pa[:, 2 * w:3 * w])
    u_ref[...] = pa[:, 3 * w:4 * w]
    kt = _dot_nt(wkt_ref[...], hn) * (HEAD_DIM ** -0.5)
    kt_ref[...] = kt.astype(BF16)
    gt_ref[...] = _dot_nt(wgt_ref[...], hn) + bg_ref[...]


def _proj(x, norm_g, wa, wkt, wgt, bg):
    n, d = x.shape
    tb = _token_block(n, 512)
    grid = (n // tb,)
    const = lambda i: (0, 0)
    return pl.pallas_call(
        _proj_kernel,
        grid=grid,
        in_specs=[
            pl.BlockSpec((tb, d), lambda i: (i, 0)),
            pl.BlockSpec((1, d), const),
            pl.BlockSpec(wa.shape, const),
            pl.BlockSpec(wkt.shape, const),
            pl.BlockSpec(wgt.shape, const),
            pl.BlockSpec(bg.shape, const),
        ],
        out_specs=[
            pl.BlockSpec((tb, MLSTM_W), lambda i: (i, 0)),
            pl.BlockSpec((MLSTM_W, tb), lambda i: (0, i)),
            pl.BlockSpec((tb, MLSTM_W), lambda i: (i, 0)),
            pl.BlockSpec((tb, MLSTM_W), lambda i: (i, 0)),
            pl.BlockSpec((tb, POOL_W), lambda i: (i, 0)),
            pl.BlockSpec((2 * HEADS, tb), lambda i: (0, i)),
        ],
        out_shape=[
            jax.ShapeDtypeStruct((n, MLSTM_W), BF16),
            jax.ShapeDtypeStruct((MLSTM_W, n), BF16),
            jax.ShapeDtypeStruct((n, MLSTM_W), BF16),
            jax.ShapeDtypeStruct((n, MLSTM_W), F32),
            jax.ShapeDtypeStruct((n, POOL_W), F32),
            jax.ShapeDtypeStruct((2 * HEADS, n), F32),
        ],
        compiler_params=_params(("parallel",)),
        name="proj",
    )(x, norm_g, wa, wkt, wgt, bg)


def _lane_cumsum(x):
    lane = lax.broadcasted_iota(jnp.int32, x.shape, 1)
    shift = 1
    while shift < x.shape[1]:
        x = x + jnp.where(lane >= shift, pltpu.roll(x, shift, axis=1), 0.0)
        shift *= 2
    return x


def _mlstm_kernel(q_ref, kt_ref, v_ref, o_ref, ig_ref, lf_ref, nw_ref,
                  hm_ref, cext_out_ref, m_out_ref, cext_ref):
    t_len = q_ref.shape[0]
    n_chunks = t_len // CHUNK
    ig2 = ig_ref[0]
    lf2 = _log_sigmoid(lf_ref[0])
    b2 = _lane_cumsum(lf2)
    a2 = ig2 - b2

    t_idx = lax.broadcasted_iota(jnp.int32, (CHUNK, CHUNK), 0)
    s_idx = lax.broadcasted_iota(jnp.int32, (CHUNK, CHUNK), 1)
    causal = s_idx <= t_idx
    ones_col = (lax.broadcasted_iota(jnp.int32, (CHUNK, HEAD_DIM), 1) == 0).astype(BF16)
    nw = nw_ref[...]

    cext_ref[...] = jnp.zeros_like(cext_ref)
    m = jnp.zeros((1, 1), F32)
    for c in range(n_chunks):
        rows = pl.ds(c * CHUNK, CHUNK)
        a_row = a2[c:c + 1, :]
        lf_row = lf2[c:c + 1, :]
        b_last = b2[c:c + 1, CHUNK - 1:CHUNK]
        b_col = jnp.sum(jnp.where(causal, lf_row, 0.0), axis=1, keepdims=True)
        amax_col = jnp.max(jnp.where(causal, a_row, NEG_INF), axis=1, keepdims=True)
        m_col = jnp.maximum(amax_col, m)
        decay_mat = jnp.where(causal, jnp.exp(a_row - m_col), 0.0)

        qc = q_ref[rows, :]
        ktc = kt_ref[:, rows]
        v_ext = jnp.concatenate([v_ref[rows, :], ones_col], axis=1)
        s = _dot(qc, ktc) * decay_mat
        intra = _dot(s.astype(BF16), v_ext)
        inter = _dot(qc, cext_ref[...].astype(BF16))
        tot = intra + jnp.exp(m - m_col) * inter
        num = tot[:, :HEAD_DIM]
        den = tot[:, HEAD_DIM:HEAD_DIM + 1]
        h = num / jnp.maximum(jnp.abs(den), jnp.exp(-(b_col + m_col)))
        h = h * lax.rsqrt(jnp.mean(h * h, axis=1, keepdims=True) + EPS)
        hm_ref[rows, :] = (h * nw * o_ref[rows, :]).astype(BF16)

        m_new = jnp.maximum(b_last + m, jnp.max(a_row, axis=1, keepdims=True) + b_last)
        ws_row = jnp.exp(a_row + (b_last - m_new))
        kws = (ktc.astype(F32) * ws_row).astype(BF16)
        cext_ref[...] = jnp.exp(b_last + m - m_new) * cext_ref[...] + _dot(kws, v_ext)
        m = m_new

    cext_out_ref[0, 0] = cext_ref[...]
    m_out_ref[0, 0] = jnp.broadcast_to(m, m_out_ref.shape[2:])


def _mlstm(q, kt, v, o, gates3, norm_w, batch, t_len):
    n = q.shape[0]
    n_chunks = t_len // CHUNK
    seq = lambda b, h: (b, h)
    return pl.pallas_call(
        _mlstm_kernel,
        grid=(batch, HEADS),
        in_specs=[
            pl.BlockSpec((t_len, HEAD_DIM), seq),
            pl.BlockSpec((HEAD_DIM, t_len), lambda b, h: (h, b)),
            pl.BlockSpec((t_len, HEAD_DIM), seq),
            pl.BlockSpec((t_len, HEAD_DIM), seq),
            pl.BlockSpec((1, n_chunks, CHUNK), lambda b, h: (h, b, 0)),
            pl.BlockSpec((1, n_chunks, CHUNK), lambda b, h: (h + HEADS, b, 0)),
            pl.BlockSpec((1, HEAD_DIM), lambda b, h: (0, h)),
        ],
        out_specs=[
            pl.BlockSpec((t_len, HEAD_DIM), seq),
            pl.BlockSpec((1, 1, HEAD_DIM, 2 * HEAD_DIM), lambda b, h: (b, h, 0, 0)),
            pl.BlockSpec((1, 1, 8, LANE), lambda b, h: (b, h, 0, 0)),
        ],
        out_shape=[
            jax.ShapeDtypeStruct((n, MLSTM_W), BF16),
            jax.ShapeDtypeStruct((batch, HEADS, HEAD_DIM, 2 * HEAD_DIM), F32),
            jax.ShapeDtypeStruct((batch, HEADS, 8, LANE), F32),
        ],
        scratch_shapes=[pltpu.VMEM((HEAD_DIM, 2 * HEAD_DIM), F32)],
        compiler_params=_params(("parallel", "parallel")),
        name="mlstm",
    )(q, kt, v, o, gates3, gates3, norm_w)


def _mlstm1_kernel(q_ref, k_ref, v_ref, o_ref, g_ref, c_ref, n_ref, m_ref, nw_ref,
                   hm_ref, c_out_ref, n_out_ref, m_out_ref):
    bb = q_ref.shape[0]
    row = lax.broadcasted_iota(jnp.int32, (bb, HEAD_DIM), 0)
    eye = (lax.broadcasted_iota(jnp.int32, (HEAD_DIM, HEAD_DIM), 0)
           == lax.broadcasted_iota(jnp.int32, (HEAD_DIM, HEAD_DIM), 1))
    lane_h = lax.broadcasted_iota(jnp.int32, (bb, HEADS), 1)
    g = g_ref[...]
    m_all = m_ref[...]
    m_out = jnp.zeros((bb, HEADS), F32)
    for h in range(HEADS):
        cols = slice(h * HEAD_DIM, (h + 1) * HEAD_DIM)
        qb = q_ref[:, cols]
        qf = qb.astype(F32)
        kf = k_ref[:, cols].astype(F32)
        vf = v_ref[:, cols].astype(F32)
        ig = g[:, h:h + 1]
        lf = _log_sigmoid(g[:, HEADS + h:HEADS + h + 1])
        m_old = m_all[:, h:h + 1]
        n_old = n_ref[:, cols]
        m_new = jnp.maximum(ig, lf + m_old)
        w_in = jnp.exp(ig - m_new)
        w_st = jnp.exp(lf + m_old - m_new)
        qc = jnp.zeros((bb, HEAD_DIM), F32)
        for j in range(bb):
            c_old = c_ref[j, h]
            res = _dot(qb, c_old.astype(BF16))
            qc = jnp.where(row == j, res, qc)
            k_col = jnp.sum(jnp.where(eye, kf[j:j + 1, :], 0.0), axis=1, keepdims=True)
            c_out_ref[j, h] = w_st[j:j + 1, :] * c_old + k_col * (w_in[j:j + 1, :] * vf[j:j + 1, :])
        s = jnp.sum(qf * kf, axis=1, keepdims=True) * w_in
        num = s * vf + w_st * qc
        den = s + w_st * jnp.sum(qf * n_old, axis=1, keepdims=True)
        hh = num / jnp.maximum(jnp.abs(den), jnp.exp(-m_new))
        hh = hh * lax.rsqrt(jnp.mean(hh * hh, axis=1, keepdims=True) + EPS)
        hm_ref[:, cols] = (hh * nw_ref[:, cols] * o_ref[:, cols]).astype(BF16)
        n_out_ref[:, cols] = w_st * n_old + w_in * kf
        m_out = jnp.where(lane_h == h, m_new, m_out)
    m_out_ref[...] = m_out


def _mlstm1(q, k, v, o, g, c0, n0, m0, norm_w):
    batch = q.shape[0]
    bb = _token_block(batch, 16)
    rows = lambda i: (i, 0)
    return pl.pallas_call(
        _mlstm1_kernel,
        grid=(batch // bb,),
        in_specs=[
            pl.BlockSpec((bb, MLSTM_W), rows),
            pl.BlockSpec((bb, MLSTM_W), rows),
            pl.BlockSpec((bb, MLSTM_W), rows),
            pl.BlockSpec((bb, MLSTM_W), rows),
            pl.BlockSpec((bb, 2 * HEADS), rows),
            pl.BlockSpec((bb, HEADS, HEAD_DIM, HEAD_DIM), lambda i: (i, 0, 0, 0)),
            pl.BlockSpec((bb, MLSTM_W), rows),
            pl.BlockSpec((bb, HEADS), rows),
            pl.BlockSpec((1, MLSTM_W), lambda i: (0, 0)),
        ],
        out_specs=[
            pl.BlockSpec((bb, MLSTM_W), rows),
            pl.BlockSpec((bb, HEADS, HEAD_DIM, HEAD_DIM), lambda i: (i, 0, 0, 0)),
            pl.BlockSpec((bb, MLSTM_W), rows),
            pl.BlockSpec((bb, HEADS), rows),
        ],
        out_shape=[
            jax.ShapeDtypeStruct((batch, MLSTM_W), BF16),
            jax.ShapeDtypeStruct(c0.shape, F32),
            jax.ShapeDtypeStruct(n0.shape, F32),
            jax.ShapeDtypeStruct(m0.shape, F32),
        ],
        compiler_params=_params(("parallel",)),
        name="mlstm1",
    )(q, k, v, o, g, c0, n0, m0, norm_w)


def _pool_kernel(u_ref, wp_ref, sc_ref, z_ref, buf_ref):
    t_len = u_ref.shape[0]
    t_idx = lax.broadcasted_iota(jnp.int32, (t_len, POOL_GROUP), 0)
    for g, w in enumerate(POOL_WINDOWS):
        cols = slice(g * POOL_GROUP, (g + 1) * POOL_GROUP)
        x = u_ref[:, cols]
        s = x
        k = 1
        while k < w:
            s = s + jnp.where(t_idx >= k, pltpu.roll(s, k, axis=0), 0.0)
            k *= 2
        cnt = jnp.minimum(t_idx + 1, w).astype(F32)
        r = s / cnt - x
        z = _dot(r.astype(BF16), wp_ref[g]) * sc_ref[:, cols]
        z_ref[:, cols] = z.astype(BF16)
    buf_ref[0] = u_ref[t_len - POOL_BUF:t_len, :]


def _pool(u, w_pool, scale, batch, t_len):
    n = u.shape[0]
    return pl.pallas_call(
        _pool_kernel,
        grid=(batch,),
        in_specs=[
            pl.BlockSpec((t_len, POOL_W), lambda b: (b, 0)),
            pl.BlockSpec(w_pool.shape, lambda b: (0, 0, 0)),
            pl.BlockSpec((1, POOL_W), lambda b: (0, 0)),
        ],
        out_specs=[
            pl.BlockSpec((t_len, POOL_W), lambda b: (b, 0)),
            pl.BlockSpec((1, POOL_BUF, POOL_W), lambda b: (b, 0, 0)),
        ],
        out_shape=[
            jax.ShapeDtypeStruct((n, POOL_W), BF16),
            jax.ShapeDtypeStruct((batch, POOL_BUF, POOL_W), F32),
        ],
        compiler_params=_params(("parallel",)),
        name="pool",
    )(u, w_pool, scale)


def _pool1_kernel(u_ref, buft_ref, wp_ref, sc_ref, z_ref, buft_out_ref):
    for g, w in enumerate(POOL_WINDOWS):
        cols = slice(g * POOL_GROUP, (g + 1) * POOL_GROUP)
        x = u_ref[:, cols]
        s = x
        for j in range(1, w):
            s = s + buft_ref[POOL_BUF - j, :, cols]
        r = s / float(w) - x
        z = _dot(r.astype(BF16), wp_ref[g]) * sc_ref[:, cols]
        z_ref[:, cols] = z.astype(BF16)
    for j in range(POOL_BUF - 1):
        buft_out_ref[j] = buft_ref[j + 1]
    buft_out_ref[POOL_BUF - 1] = u_ref[...]


def _pool1(u, buf_t, w_pool, scale):
    batch = u.shape[0]
    full2 = lambda i: (0, 0)
    full3 = lambda i: (0, 0, 0)
    return pl.pallas_call(
        _pool1_kernel,
        grid=(1,),
        in_specs=[
            pl.BlockSpec(u.shape, full2),
            pl.BlockSpec(buf_t.shape, full3),
            pl.BlockSpec(w_pool.shape, full3),
            pl.BlockSpec((1, POOL_W), full2),
        ],
        out_specs=[
            pl.BlockSpec((batch, POOL_W), full2),
            pl.BlockSpec(buf_t.shape, full3),
        ],
        out_shape=[
            jax.ShapeDtypeStruct((batch, POOL_W), BF16),
            jax.ShapeDtypeStruct(buf_t.shape, F32),
        ],
        compiler_params=_params(("arbitrary",)),
        name="pool1",
    )(u, buf_t, w_pool, scale)


def _mix_kernel(x_ref, hm_ref, zp_ref, wo_ref, g_ref, x1_ref, xnt_ref):
    mix = _dot(hm_ref[...], wo_ref[0:MLSTM_W, :]) + _dot(zp_ref[...], wo_ref[MLSTM_W:, :])
    x1 = x_ref[...] + mix
    x1_ref[...] = x1
    xnt_ref[...] = _rmsnorm(x1, g_ref[...]).T.astype(BF16)


def _mix(x, hm, zp, w_out, norm_g):
    n, d = x.shape
    tb = _token_block(n, 512)
    rows = lambda i: (i, 0)
    const = lambda i: (0, 0)
    return pl.pallas_call(
        _mix_kernel,
        grid=(n // tb,),
        in_specs=[
            pl.BlockSpec((tb, d), rows),
            pl.BlockSpec((tb, MLSTM_W), rows),
            pl.BlockSpec((tb, POOL_W), rows),
            pl.BlockSpec(w_out.shape, const),
            pl.BlockSpec((1, d), const),
        ],
        out_specs=[
            pl.BlockSpec((tb, d), rows),
            pl.BlockSpec((d, tb), lambda i: (0, i)),
        ],
        out_shape=[
            jax.ShapeDtypeStruct((n, d), F32),
            jax.ShapeDtypeStruct((d, n), BF16),
        ],
        compiler_params=_params(("parallel",)),
        name="mix",
    )(x, hm, zp, w_out, norm_g)


def _extract_max(vals, pos):
    m = vals[0]
    for v in vals[1:]:
        m = jnp.maximum(m, v)
    m = jnp.max(m, axis=0, keepdims=True)
    big = jnp.float32(1e9)
    idx = None
    for v, p in zip(vals, pos):
        cand = jnp.where(v == m, p, big)
        idx = cand if idx is None else jnp.minimum(idx, cand)
    idx = jnp.min(idx, axis=0, keepdims=True)
    return m, idx


def _top16(s):
    tb = s.shape[1]
    row = lax.broadcasted_iota(jnp.int32, s.shape, 0).astype(F32)
    krow = lax.broadcasted_iota(jnp.int32, (PEER_TOPK, tb), 0)
    rank = jnp.full(s.shape, UNRANKED, F32)
    sv = jnp.zeros((PEER_TOPK, tb), F32)
    for k in range(PEER_TOPK):
        m, idx = _extract_max([s], [row])
        hit = row == idx
        rank = jnp.where(hit, float(k), rank)
        s = jnp.where(hit, NEG_INF, s)
        sv = jnp.where(krow == k, m, sv)
    return sv, rank


def _candidate_groups(sv1, sv2):
    tb = sv1.shape[1]
    vals, pos = [], []
    r8 = lax.broadcasted_iota(jnp.int32, (8, tb), 0)
    vals.append(sv1[0:1, :] + sv2[8:16, :])
    pos.append((r8 + 8).astype(F32))
    vals.append(sv1[0:1, :] + sv2[0:8, :])
    pos.append(r8.astype(F32))
    for k1 in range(1, 8):
        lim = PEER_TOPK // (k1 + 1)
        v = sv1[k1:k1 + 1, :] + sv2[0:8, :]
        vals.append(jnp.where(r8 < lim, v, NEG_INF))
        pos.append((r8 + k1 * PEER_TOPK).astype(F32))
    vals.append(sv1[8:16, :] + sv2[0:1, :])
    pos.append(((r8 + 8) * PEER_TOPK).astype(F32))
    return vals, pos


def _sorting_pairs(n):
    pairs = []
    p = 1
    while p < n:
        k = p
        while k >= 1:
            for j in range(k % p, n - k, 2 * k):
                for i in range(min(k, n - j - k)):
                    if (i + j) // (2 * p) == (i + j + k) // (2 * p):
                        pairs.append((i + j, i + j + k))
            k //= 2
        p *= 2
    return pairs


def _sublane_allreduce(x, op):
    for d in (1, 2, 4):
        x = op(x, pltpu.roll(x, d, axis=0))
    return x


def _sorted_top16(groups):
    g = list(groups)
    for i, j in _sorting_pairs(len(g)):
        g[i], g[j] = jnp.maximum(g[i], g[j]), jnp.minimum(g[i], g[j])
    n = len(g)
    for d in (1, 2, 4):
        p = [pltpu.roll(x, d, axis=0) for x in g]
        g = [jnp.maximum(g[i], p[n - 1 - i]) for i in range(n)]
        stride = n // 2
        while stride >= 1:
            for i in range(n):
                if i & stride == 0:
                    g[i], g[i + stride] = (jnp.maximum(g[i], g[i + stride]),
                                           jnp.minimum(g[i], g[i + stride]))
            stride //= 2
    return g


def _rank_bits(x, sv):
    c8 = x < sv[7]
    t = jnp.where(c8, sv[11], sv[3])
    c4 = x < t
    t = jnp.where(c8, jnp.where(c4, sv[13], sv[9]), jnp.where(c4, sv[5], sv[1]))
    c2 = x < t
    t = jnp.where(c8,
                  jnp.where(c4, jnp.where(c2, sv[14], sv[12]), jnp.where(c2, sv[10], sv[8])),
                  jnp.where(c4, jnp.where(c2, sv[6], sv[4]), jnp.where(c2, sv[2], sv[0])))
    c1 = x < t
    return (c8, c4, c2, c1), x < sv[15]


def _select16(bits, rows):
    c8, c4, c2, c1 = bits
    lvl = [jnp.where(c1, rows[2 * i + 1], rows[2 * i]) for i in range(8)]
    lvl = [jnp.where(c2, lvl[2 * i + 1], lvl[2 * i]) for i in range(4)]
    lvl = [jnp.where(c4, lvl[2 * i + 1], lvl[2 * i]) for i in range(2)]
    return jnp.where(c8, lvl[1], lvl[0])


def _route_fast(s1, s2):
    tb = s1.shape[1]
    r8 = lax.broadcasted_iota(jnp.int32, (8, tb), 0)
    n_grp = PEER_NKEYS // 8
    g1 = [s1[v * 8:(v + 1) * 8, :] for v in range(n_grp)]
    g2 = [s2[v * 8:(v + 1) * 8, :] for v in range(n_grp)]
    sv1 = _sorted_top16(g1)
    sv2 = _sorted_top16(g2)

    bad = jnp.zeros((8, tb), F32)
    for sv, grp in ((sv1, g1), (sv2, g2)):
        for k in range(PEER_TOPK - 1):
            bad = jnp.where(sv[k] == sv[k + 1], 1.0, bad)
        n_in = jnp.zeros((8, tb), F32)
        for x in grp:
            n_in = n_in + jnp.where(x >= sv[PEER_TOPK - 1], 1.0, 0.0)
        n_in = _sublane_allreduce(n_in, jnp.add)
        bad = jnp.where(n_in != float(PEER_TOPK), 1.0, bad)

    def by_sublane(rows):
        out = jnp.zeros((8, tb), F32)
        for k, row in enumerate(rows):
            out = jnp.where(r8 == k, row, out)
        return out
    a2_lo, a2_hi, a1_hi = by_sublane(sv2[0:8]), by_sublane(sv2[8:16]), by_sublane(sv1[8:16])
    orig = [sv1[0] + a2_hi, sv1[0] + a2_lo]
    for k1 in range(1, 8):
        orig.append(jnp.where(r8 < PEER_TOPK // (k1 + 1), sv1[k1] + a2_lo, NEG_INF))
    orig.append(a1_hi + sv2[0])
    vals = list(orig)
    prev = None
    for _ in range(PEER_TOPK):
        m = vals[0]
        for v in vals[1:]:
            m = jnp.maximum(m, v)
        m = _sublane_allreduce(m, jnp.maximum)
        vals = [jnp.where(v == m, NEG_INF, v) for v in vals]
        if prev is not None:
            bad = jnp.where(m == prev, 1.0, bad)
        prev = m
    sel = [o >= prev for o in orig]
    cmax = sv1[0] + sv2[0]
    z = jnp.zeros((8, tb), F32)
    for o, sl in zip(orig, sel):
        z = z + jnp.where(sl, jnp.exp(o - cmax), 0.0)
    z = _sublane_allreduce(z, jnp.add)
    ones = [jnp.where(sl, 1.0, 0.0) for sl in sel]
    cnt = [_sublane_allreduce(ones[0] + ones[1], jnp.add)]
    cnt += [_sublane_allreduce(ones[k1 + 1], jnp.add) for k1 in range(1, 8)]
    cnt += [ones[9][j:j + 1, :] for j in range(8)]
    total = cnt[0]
    for c in cnt[1:]:
        total = total + c
    bad = jnp.where(total != float(PEER_TOPK), 1.0, bad)

    cnt1, r2 = [], []
    weights = (8.0, 4.0, 2.0, 1.0)
    for x in g1:
        bits, below = _rank_bits(x, sv1)
        cnt1.append(jnp.where(below, 0.0, _select16(bits, cnt)))
    for x in g2:
        bits, below = _rank_bits(x, sv2)
        rank = jnp.zeros((8, tb), F32)
        for b, wgt in zip(bits, weights):
            rank = rank + jnp.where(b, wgt, 0.0)
        r2.append(jnp.where(below, UNRANKED, rank))
    return cnt1, r2, sv1[0][0:1, :], sv2[0][0:1, :], z[0:1, :], bad


def _route_exact(s1, s2):
    sv1, r1 = _top16(s1)
    sv2, r2 = _top16(s2)

    vals, pos = _candidate_groups(sv1, sv2)
    orig = list(vals)
    sel = [jnp.zeros(v.shape, F32) for v in vals]
    for _ in range(PEER_TOPK):
        _, idx = _extract_max(vals, pos)
        for i in range(len(vals)):
            hit = pos[i] == idx
            sel[i] = jnp.where(hit, 1.0, sel[i])
            vals[i] = jnp.where(hit, NEG_INF, vals[i])

    cmax = orig[1][0:1, :]
    z = None
    for o, sl in zip(orig, sel):
        part = jnp.sum(jnp.where(sl > 0.0, jnp.exp(o - cmax), 0.0), axis=0, keepdims=True)
        z = part if z is None else z + part
    cnt = [jnp.sum(sel[0] + sel[1], axis=0, keepdims=True)]
    cnt += [jnp.sum(sel[k1 + 1], axis=0, keepdims=True) for k1 in range(1, 8)]
    cnt += [sel[9][j:j + 1, :] for j in range(8)]

    cnt1 = jnp.zeros(r1.shape, F32)
    for k1 in range(PEER_TOPK):
        cnt1 = jnp.where(r1 == float(k1), cnt[k1], cnt1)
    return cnt1, r2, sv1[0:1, :], sv2[0:1, :], z


ROUTE_HEADS_PER_STEP = 2


def _route_kernel(xnt_ref, wqt_ref, k1_ref, k2_ref, cnt1_ref, g1_ref, r2_ref, e2_ref):
    qt_all = _dot(wqt_ref[...], xnt_ref[...]).astype(BF16)
    for hh in range(ROUTE_HEADS_PER_STEP):
        _route_one_head(hh, qt_all[hh * 2 * PEER_HALF:(hh + 1) * 2 * PEER_HALF, :],
                        k1_ref, k2_ref, cnt1_ref, g1_ref, r2_ref, e2_ref)


def _route_one_head(hh, qt, k1_ref, k2_ref, cnt1_ref, g1_ref, r2_ref, e2_ref):
    s1 = _dot(k1_ref[hh], qt[0:PEER_HALF, :])
    s2 = _dot(k2_ref[hh], qt[PEER_HALF:, :])

    def emit_gates(max1, max2, z):
        g1_ref[hh] = jnp.exp(s1 - max1) * (0.5 / z)
        e2_ref[hh] = jnp.exp(s2 - max2).astype(BF16)

    cnt1, r2, max1, max2, z, tie = _route_fast(s1, s2)
    for v in range(PEER_NKEYS // 8):
        cnt1_ref[hh, v * 8:(v + 1) * 8, :] = cnt1[v]
        r2_ref[hh, v * 8:(v + 1) * 8, :] = r2[v].astype(BF16)
    emit_gates(max1, max2, z)

    @pl.when(jnp.max(tie) > 0.0)
    def _():
        cnt1, r2, max1, max2, z = _route_exact(s1, s2)
        cnt1_ref[hh] = cnt1
        r2_ref[hh] = r2.astype(BF16)
        emit_gates(max1, max2, z)


def _route(xnt, wqt, keys):
    d, n = xnt.shape
    tb = _token_block(n, 256)
    hps = ROUTE_HEADS_PER_STEP
    per_head = pl.BlockSpec((hps, PEER_NKEYS, tb), lambda i, h: (h, 0, i))
    shape = lambda dt: jax.ShapeDtypeStruct((PEER_HEADS, PEER_NKEYS, n), dt)
    return pl.pallas_call(
        _route_kernel,
        grid=(n // tb, PEER_HEADS // hps),
        in_specs=[
            pl.BlockSpec((d, tb), lambda i, h: (0, i)),
            pl.BlockSpec((hps * 2 * PEER_HALF, d), lambda i, h: (h, 0)),
            pl.BlockSpec((hps, PEER_NKEYS, PEER_HALF), lambda i, h: (h, 0, 0)),
            pl.BlockSpec((hps, PEER_NKEYS, PEER_HALF), lambda i, h: (h + PEER_HEADS // hps, 0, 0)),
        ],
        out_specs=[per_head] * 4,
        out_shape=[shape(F32), shape(F32), shape(BF16), shape(BF16)],
        compiler_params=_params(("parallel", "parallel")),
        name="route",
    )(xnt, wqt, keys, keys)


I1_PER_TILE = 8
BF16_TILE = (8, 2 * LANE)
EXPERT_TILE = I1_PER_TILE * PEER_NKEYS


def _experts_kernel(xnt_ref, cnt1_ref, g1_ref, r2_ref, e2_ref, u_ref, vt_ref, x1_ref,
                    x2_ref, acc_ref, act_ref, wa_ref):
    e = pl.program_id(1)
    tb = xnt_ref.shape[1]

    @pl.when(e == 0)
    def _():
        acc_ref[...] = jnp.zeros_like(acc_ref)

    a = _dot(u_ref[...], xnt_ref[...]).astype(BF16)
    act = a * (1.0 + jnp.tanh(a * (GELU_C0 + GELU_C1 * (a * a))))
    act_ref[...] = act.reshape(act_ref.shape)
    sub, width = BF16_TILE[0], min(BF16_TILE[1], tb)
    assert sub == I1_PER_TILE
    i1_rows = pl.ds(pl.multiple_of(e * I1_PER_TILE, I1_PER_TILE), I1_PER_TILE)
    for lt in range(tb // width):
        lanes = pl.ds(lt * width, width)
        cnt_tile = [cnt1_ref[h, i1_rows, lanes] for h in range(PEER_HEADS)]
        g1_tile = [g1_ref[h, i1_rows, lanes] for h in range(PEER_HEADS)]
        for j in range(I1_PER_TILE):
            cnt_b = [jnp.broadcast_to(t[j:j + 1, :], (sub, width)).astype(BF16) for t in cnt_tile]
            g1_b = [jnp.broadcast_to(t[j:j + 1, :], (sub, width)).astype(BF16) for t in g1_tile]
            for rb in range(PEER_NKEYS // sub):
                krows = pl.ds(rb * sub, sub)
                w = None
                for h in range(PEER_HEADS):
                    mask = r2_ref[h, krows, lanes] < cnt_b[h]
                    term = jnp.where(mask, e2_ref[h, krows, lanes] * g1_b[h], jnp.zeros((), BF16))
                    w = term if w is None else w + term
                grp = j * (PEER_NKEYS // sub) + rb
                wa_ref[grp, :, lanes] = w * act_ref[grp, :, lanes]
    acc_ref[...] += _dot(vt_ref[...], wa_ref[...].reshape(EXPERT_TILE, tb))

    @pl.when(e == pl.num_programs(1) - 1)
    def _():
        x2_ref[...] = x1_ref[...] + acc_ref[...].T


def _experts(xnt, cnt1, g1, r2, e2, u_tab, vt_tab, x1):
    d, n = xnt.shape
    n_exp = u_tab.shape[0]
    tb = _token_block(n, 512)
    assert tb % LANE == 0 and n_exp % EXPERT_TILE == 0
    per_head = pl.BlockSpec((PEER_HEADS, PEER_NKEYS, tb), lambda i, e: (0, 0, i))
    tile_buf = pltpu.VMEM((EXPERT_TILE // BF16_TILE[0], BF16_TILE[0], tb), BF16)
    return pl.pallas_call(
        _experts_kernel,
        grid=(n // tb, n_exp // EXPERT_TILE),
        in_specs=[
            pl.BlockSpec((d, tb), lambda i, e: (0, i)),
            per_head, per_head, per_head, per_head,
            pl.BlockSpec((EXPERT_TILE, d), lambda i, e: (e, 0)),
            pl.BlockSpec((d, EXPERT_TILE), lambda i, e: (0, e)),
            pl.BlockSpec((tb, d), lambda i, e: (i, 0)),
        ],
        out_specs=pl.BlockSpec((tb, d), lambda i, e: (i, 0)),
        out_shape=jax.ShapeDtypeStruct((n, d), F32),
        scratch_shapes=[pltpu.VMEM((d, tb), F32), tile_buf, tile_buf],
        compiler_params=_params(("parallel", "arbitrary")),
        name="experts",
    )(xnt, cnt1, g1, r2, e2, u_tab, vt_tab, x1)


def _ple_kernel(x_ref, p_ref, wg_ref, wp_ref, nf_ref, y_ref, *, final_norm):
    x = x_ref[...]
    gate = _sigmoid(_dot(x.astype(BF16), wg_ref[...]))
    y = x + gate * _dot(p_ref[...].astype(BF16), wp_ref[...])
    if final_norm:
        y = _rmsnorm(y, nf_ref[...])
    y_ref[...] = y


def _ple(x, p, w_gate, w_ple, norm_f, final_norm):
    n, d = x.shape
    tb = _token_block(n, 512)
    rows = lambda i: (i, 0)
    const = lambda i: (0, 0)
    return pl.pallas_call(
        functools.partial(_ple_kernel, final_norm=final_norm),
        grid=(n // tb,),
        in_specs=[
            pl.BlockSpec((tb, d), rows),
            pl.BlockSpec((tb, p.shape[1]), rows),
            pl.BlockSpec(w_gate.shape, const),
            pl.BlockSpec(w_ple.shape, const),
            pl.BlockSpec((1, d), const),
        ],
        out_specs=pl.BlockSpec((tb, d), rows),
        out_shape=jax.ShapeDtypeStruct((n, d), F32),
        compiler_params=_params(("parallel",)),
        name="ple",
    )(x, p, w_gate, w_ple, norm_f)


def _prep_layer_weights(l, w_in, b_gate, mlstm_norm, w_pool, pool_scale, w_out, norm1, norm2,
                        peer_wq, peer_keys, peer_u, peer_v, w_ple, w_gate):
    w = MLSTM_W
    wi = w_in[l]
    d = wi.shape[0]
    wa = jnp.concatenate([wi[:, 0:w], wi[:, 2 * w:4 * w + POOL_W]], axis=1).astype(BF16)
    wkt = wi[:, w:2 * w].T.astype(BF16)
    wgt = wi[:, 4 * w + POOL_W:].T.astype(BF16)
    return dict(
        wa=wa, wkt=wkt, wgt=wgt,
        bg=b_gate[l].reshape(2 * HEADS, 1).astype(F32),
        mlstm_norm=mlstm_norm[l].reshape(1, w),
        w_pool=w_pool[l].astype(BF16),
        pool_scale=pool_scale[l].reshape(1, POOL_W),
        w_out=w_out[l].astype(BF16),
        norm1=norm1[l].reshape(1, d), norm2=norm2[l].reshape(1, d),
        wqt=peer_wq[l].T.astype(BF16),
        keys=peer_keys[l].reshape(2 * PEER_HEADS, PEER_NKEYS, PEER_HALF).astype(BF16),
        u_tab=peer_u[l].astype(BF16),
        vt_tab=peer_v[l].T.astype(BF16),
        w_ple=w_ple[l].astype(BF16), w_gate=w_gate[l].astype(BF16),
    )


def _ffn_and_embed(x1, xnt, p, lw, norm_f, final_norm):
    cnt1, g1, r2, e2 = _route(xnt, lw["wqt"], lw["keys"])
    x2 = _experts(xnt, cnt1, g1, r2, e2, lw["u_tab"], lw["vt_tab"], x1)
    return _ple(x2, p, lw["w_gate"], lw["w_ple"], norm_f, final_norm)


def _prompt_layer(x, p, lw, norm_f, final_norm, batch, t_len):
    q, kt, v, o, u, gt = _proj(x, lw["norm1"], lw["wa"], lw["wkt"], lw["wgt"], lw["bg"])
    gates3 = gt.reshape(2 * HEADS, (batch * t_len) // CHUNK, CHUNK)
    hm, cext, m8 = _mlstm(q, kt, v, o, gates3, lw["mlstm_norm"], batch, t_len)
    zp, buf = _pool(u, lw["w_pool"], lw["pool_scale"], batch, t_len)
    x1, xnt = _mix(x, hm, zp, lw["w_out"], lw["norm2"])
    y = _ffn_and_embed(x1, xnt, p, lw, norm_f, final_norm)
    return y, cext[..., :HEAD_DIM], cext[..., HEAD_DIM], m8[..., 0, 0], buf


def _sample_layer(x, p, c0, n0, m0, buf0, lw, norm_f, final_norm):
    q, kt, v, o, u, gt = _proj(x, lw["norm1"], lw["wa"], lw["wkt"], lw["wgt"], lw["bg"])
    hm, c1, n1, m1 = _mlstm1(q, kt.T, v, o, gt.T, c0, n0.reshape(-1, MLSTM_W), m0, lw["mlstm_norm"])
    n1 = n1.reshape(n0.shape)
    zp, buf_t = _pool1(u, jnp.swapaxes(buf0, 0, 1), lw["w_pool"], lw["pool_scale"])
    x1, xnt = _mix(x, hm, zp, lw["w_out"], lw["norm2"])
    y = _ffn_and_embed(x1, xnt, p, lw, norm_f, final_norm)
    return y, c1, n1, m1, jnp.swapaxes(buf_t, 0, 1)


def kernel(x_prompt, x_sample, p_prompt, p_sample, state_C, state_n, state_m, state_pool,
           w_in, b_gate, mlstm_norm, w_pool, pool_scale, w_out, norm1, norm2,
           peer_wq, peer_keys, peer_u, peer_v, w_ple, w_gate, norm_f):
    depth = w_in.shape[0]
    batch, t_len, d = x_prompt.shape
    dec_batch, dec_len, _ = x_sample.shape
    assert dec_len == 1 and t_len % CHUNK == 0
    nf = norm_f.reshape(1, d)

    xp = x_prompt.reshape(batch * t_len, d)
    xs = x_sample.reshape(dec_batch, d)
    outs_p, outs_s = [], []
    for l in range(depth):
        lw = _prep_layer_weights(l, w_in, b_gate, mlstm_norm, w_pool, pool_scale, w_out, norm1,
                                 norm2, peer_wq, peer_keys, peer_u, peer_v, w_ple, w_gate)
        final = l == depth - 1
        pp = p_prompt[l].reshape(batch * t_len, -1)
        ps = p_sample[l].reshape(dec_batch, -1)
        xp, c_p, n_p, m_p, buf_p = _prompt_layer(xp, pp, lw, nf, final, batch, t_len)
        xs, c_s, n_s, m_s, buf_s = _sample_layer(xs, ps, state_C[l], state_n[l], state_m[l],
                                                 state_pool[l], lw, nf, final)
        outs_p.append((c_p, n_p, m_p, buf_p))
        outs_s.append((c_s, n_s, m_s, buf_s))

    stack = lambda outs, i: jnp.stack([o[i] for o in outs])
    return (xp.reshape(batch, t_len, d), xs.reshape(dec_batch, dec_len, d),
            stack(outs_p, 0), stack(outs_p, 1), stack(outs_p, 2), stack(outs_p, 3),
            stack(outs_s, 0), stack(outs_s, 1), stack(outs_s, 2), stack(outs_s, 3))
```

```python
import functools
import math

import jax
import jax.numpy as jnp
from jax import lax
from jax.experimental import pallas as pl
from jax.experimental.pallas import tpu as pltpu

F32 = jnp.float32
BF16 = jnp.bfloat16
EPS = 1e-6
NEG_INF = float("-inf")

LANE = 128
V7X_VMEM_LIMIT = 56 * 1024 * 1024

HEADS = 4
HEAD_DIM = 128
MLSTM_W = HEADS * HEAD_DIM
POOL_WINDOWS = (2, 4, 8, 16)
POOL_GROUP = 128
POOL_W = POOL_GROUP * len(POOL_WINDOWS)
POOL_BUF = max(POOL_WINDOWS) - 1
CHUNK = 128
PEER_HEADS = 8
PEER_NKEYS = 128
PEER_HALF = 128
PEER_TOPK = 16
UNRANKED = 127.0
GELU_C0 = math.sqrt(2.0 / math.pi)
GELU_C1 = 0.044715 * GELU_C0


def _params(semantics):
    return pltpu.CompilerParams(dimension_semantics=semantics, vmem_limit_bytes=V7X_VMEM_LIMIT)


def _token_block(n, want):
    tb = min(n, want)
    assert n % tb == 0
    return tb


def _rmsnorm(x, g):
    return x * lax.rsqrt(jnp.mean(x * x, axis=-1, keepdims=True) + EPS) * g


def _log_sigmoid(x):
    return jnp.minimum(x, 0.0) - jnp.log(1.0 + jnp.exp(-jnp.abs(x)))


def _sigmoid(x):
    return 1.0 / (1.0 + jnp.exp(-x))


def _dot(a, b):
    return jnp.dot(a, b, preferred_element_type=F32)


def _dot_nt(a, b):
    return lax.dot_general(a, b, (((1,), (1,)), ((), ())), preferred_element_type=F32)


def _proj_kernel(x_ref, g_ref, wa_ref, wkt_ref, wgt_ref, bg_ref,
                 q_ref, kt_ref, v_ref, o_ref, u_ref, gt_ref):
    hn = _rmsnorm(x_ref[...], g_ref[...]).astype(BF16)
    pa = _dot(hn, wa_ref[...])
    w = MLSTM_W
    q_ref[...] = pa[:, 0:w].astype(BF16)
    v_ref[...] = pa[:, w:2 * w].astype(BF16)
    o_ref[...] = _sigmoid(pa[:, 2 * w:3 * w])
    u_ref[...] = pa[:, 3 * w:4 * w]
    kt = _dot_nt(wkt_ref[...], hn) * (HEAD_DIM ** -0.5)
    kt_ref[...] = kt.astype(BF16)
    gt_ref[...] = _dot_nt(wgt_ref[...], hn) + bg_ref[...]


def _proj(x, norm_g, wa, wkt, wgt, bg):
    n, d = x.shape
    tb = _token_block(n, 512)
    grid = (n // tb,)
    const = lambda i: (0, 0)
    return pl.pallas_call(
        _proj_kernel,
        grid=grid,
        in_specs=[
            pl.BlockSpec((tb, d), lambda i: (i, 0)),
            pl.BlockSpec((1, d), const),
            pl.BlockSpec(wa.shape, const),
            pl.BlockSpec(wkt.shape, const),
            pl.BlockSpec(wgt.shape, const),
            pl.BlockSpec(bg.shape, const),
        ],
        out_specs=[
            pl.BlockSpec((tb, MLSTM_W), lambda i: (i, 0)),
            pl.BlockSpec((MLSTM_W, tb), lambda i: (0, i)),
            pl.BlockSpec((tb, MLSTM_W), lambda i: (i, 0)),
            pl.BlockSpec((tb, MLSTM_W), lambda i: (i, 0)),
            pl.BlockSpec((tb, POOL_W), lambda i: (i, 0)),
            pl.BlockSpec((2 * HEADS, tb), lambda i: (0, i)),
        ],
        out_shape=[
            jax.ShapeDtypeStruct((n, MLSTM_W), BF16),
            jax.ShapeDtypeStruct((MLSTM_W, n), BF16),
            jax.ShapeDtypeStruct((n, MLSTM_W), BF16),
            jax.ShapeDtypeStruct((n, MLSTM_W), F32),
            jax.ShapeDtypeStruct((n, POOL_W), F32),
            jax.ShapeDtypeStruct((2 * HEADS, n), F32),
        ],
        compiler_params=_params(("parallel",)),
        name="proj",
    )(x, norm_g, wa, wkt, wgt, bg)


def _lane_cumsum(x):
    lane = lax.broadcasted_iota(jnp.int32, x.shape, 1)
    shift = 1
    while shift < x.shape[1]:
        x = x + jnp.where(lane >= shift, pltpu.roll(x, shift, axis=1), 0.0)
        shift *= 2
    return x


def _mlstm_kernel(q_ref, kt_ref, v_ref, o_ref, ig_ref, lf_ref, nw_ref,
                  hm_ref, cext_out_ref, m_out_ref, cext_ref):
    t_len = q_ref.shape[0]
    n_chunks = t_len // CHUNK
    ig2 = ig_ref[0]
    lf2 = _log_sigmoid(lf_ref[0])
    b2 = _lane_cumsum(lf2)
    a2 = ig2 - b2

    t_idx = lax.broadcasted_iota(jnp.int32, (CHUNK, CHUNK), 0)
    s_idx = lax.broadcasted_iota(jnp.int32, (CHUNK, CHUNK), 1)
    causal = s_idx <= t_idx
    ones_col = (lax.broadcasted_iota(jnp.int32, (CHUNK, HEAD_DIM), 1) == 0).astype(BF16)
    nw = nw_ref[...]

    cext_ref[...] = jnp.zeros_like(cext_ref)
    m = jnp.zeros((1, 1), F32)
    for c in range(n_chunks):
        rows = pl.ds(c * CHUNK, CHUNK)
        a_row = a2[c:c + 1, :]
        lf_row = lf2[c:c + 1, :]
        b_last = b2[c:c + 1, CHUNK - 1:CHUNK]
        b_col = jnp.sum(jnp.where(causal, lf_row, 0.0), axis=1, keepdims=True)
        amax_col = jnp.max(jnp.where(causal, a_row, NEG_INF), axis=1, keepdims=True)
        m_col = jnp.maximum(amax_col, m)
        decay_mat = jnp.where(causal, jnp.exp(a_row - m_col), 0.0)

        qc = q_ref[rows, :]
        ktc = kt_ref[:, rows]
        v_ext = jnp.concatenate([v_ref[rows, :], ones_col], axis=1)
        s = _dot(qc, ktc) * decay_mat
        intra = _dot(s.astype(BF16), v_ext)
        inter = _dot(qc, cext_ref[...].astype(BF16))
        tot = intra + jnp.exp(m - m_col) * inter
        num = tot[:, :HEAD_DIM]
        den = tot[:, HEAD_DIM:HEAD_DIM + 1]
        h = num / jnp.maximum(jnp.abs(den), jnp.exp(-(b_col + m_col)))
        h = h * lax.rsqrt(jnp.mean(h * h, axis=1, keepdims=True) + EPS)
        hm_ref[rows, :] = (h * nw * o_ref[rows, :]).astype(BF16)

        m_new = jnp.maximum(b_last + m, jnp.max(a_row, axis=1, keepdims=True) + b_last)
        ws_row = jnp.exp(a_row + (b_last - m_new))
        kws = (ktc.astype(F32) * ws_row).astype(BF16)
        cext_ref[...] = jnp.exp(b_last + m - m_new) * cext_ref[...] + _dot(kws, v_ext)
        m = m_new

    cext_out_ref[0, 0] = cext_ref[...]
    m_out_ref[0, 0] = jnp.broadcast_to(m, m_out_ref.shape[2:])


def _mlstm(q, kt, v, o, gates3, norm_w, batch, t_len):
    n = q.shape[0]
    n_chunks = t_len // CHUNK
    seq = lambda b, h: (b, h)
    return pl.pallas_call(
        _mlstm_kernel,
        grid=(batch, HEADS),
        in_specs=[
            pl.BlockSpec((t_len, HEAD_DIM), seq),
            pl.BlockSpec((HEAD_DIM, t_len), lambda b, h: (h, b)),
            pl.BlockSpec((t_len, HEAD_DIM), seq),
            pl.BlockSpec((t_len, HEAD_DIM), seq),
            pl.BlockSpec((1, n_chunks, CHUNK), lambda b, h: (h, b, 0)),
            pl.BlockSpec((1, n_chunks, CHUNK), lambda b, h: (h + HEADS, b, 0)),
            pl.BlockSpec((1, HEAD_DIM), lambda b, h: (0, h)),
        ],
        out_specs=[
            pl.BlockSpec((t_len, HEAD_DIM), seq),
            pl.BlockSpec((1, 1, HEAD_DIM, 2 * HEAD_DIM), lambda b, h: (b, h, 0, 0)),
            pl.BlockSpec((1, 1, 8, LANE), lambda b, h: (b, h, 0, 0)),
        ],
        out_shape=[
            jax.ShapeDtypeStruct((n, MLSTM_W), BF16),
            jax.ShapeDtypeStruct((batch, HEADS, HEAD_DIM, 2 * HEAD_DIM), F32),
            jax.ShapeDtypeStruct((batch, HEADS, 8, LANE), F32),
        ],
        scratch_shapes=[pltpu.VMEM((HEAD_DIM, 2 * HEAD_DIM), F32)],
        compiler_params=_params(("parallel", "parallel")),
        name="mlstm",
    )(q, kt, v, o, gates3, gates3, norm_w)


def _mlstm1_kernel(q_ref, k_ref, v_ref, o_ref, g_ref, c_ref, n_ref, m_ref, nw_ref,
                   hm_ref, c_out_ref, n_out_ref, m_out_ref):
    bb = q_ref.shape[0]
    row = lax.broadcasted_iota(jnp.int32, (bb, HEAD_DIM), 0)
    eye = (lax.broadcasted_iota(jnp.int32, (HEAD_DIM, HEAD_DIM), 0)
           == lax.broadcasted_iota(jnp.int32, (HEAD_DIM, HEAD_DIM), 1))
    lane_h = lax.broadcasted_iota(jnp.int32, (bb, HEADS), 1)
    g = g_ref[...]
    m_all = m_ref[...]
    m_out = jnp.zeros((bb, HEADS), F32)
    for h in range(HEADS):
        cols = slice(h * HEAD_DIM, (h + 1) * HEAD_DIM)
        qb = q_ref[:, cols]
        qf = qb.astype(F32)
        kf = k_ref[:, cols].astype(F32)
        vf = v_ref[:, cols].astype(F32)
        ig = g[:, h:h + 1]
        lf = _log_sigmoid(g[:, HEADS + h:HEADS + h + 1])
        m_old = m_all[:, h:h + 1]
        n_old = n_ref[:, cols]
        m_new = jnp.maximum(ig, lf + m_old)
        w_in = jnp.exp(ig - m_new)
        w_st = jnp.exp(lf + m_old - m_new)
        qc = jnp.zeros((bb, HEAD_DIM), F32)
        for j in range(bb):
            c_old = c_ref[j, h]
            res = _dot(qb, c_old.astype(BF16))
            qc = jnp.where(row == j, res, qc)
            k_col = jnp.sum(jnp.where(eye, kf[j:j + 1, :], 0.0), axis=1, keepdims=True)
            c_out_ref[j, h] = w_st[j:j + 1, :] * c_old + k_col * (w_in[j:j + 1, :] * vf[j:j + 1, :])
        s = jnp.sum(qf * kf, axis=1, keepdims=True) * w_in
        num = s * vf + w_st * qc
        den = s + w_st * jnp.sum(qf * n_old, axis=1, keepdims=True)
        hh = num / jnp.maximum(jnp.abs(den), jnp.exp(-m_new))
        hh = hh * lax.rsqrt(jnp.mean(hh * hh, axis=1, keepdims=True) + EPS)
        hm_ref[:, cols] = (hh * nw_ref[:, cols] * o_ref[:, cols]).astype(BF16)
        n_out_ref[:, cols] = w_st * n_old + w_in * kf
        m_out = jnp.where(lane_h == h, m_new, m_out)
    m_out_ref[...] = m_out


def _mlstm1(q, k, v, o, g, c0, n0, m0, norm_w):
    batch = q.shape[0]
    bb = _token_block(batch, 16)
    rows = lambda i: (i, 0)
    return pl.pallas_call(
        _mlstm1_kernel,
        grid=(batch // bb,),
        in_specs=[
            pl.BlockSpec((bb, MLSTM_W), rows),
            pl.BlockSpec((bb, MLSTM_W), rows),
            pl.BlockSpec((bb, MLSTM_W), rows),
            pl.BlockSpec((bb, MLSTM_W), rows),
            pl.BlockSpec((bb, 2 * HEADS), rows),
            pl.BlockSpec((bb, HEADS, HEAD_DIM, HEAD_DIM), lambda i: (i, 0, 0, 0)),
            pl.BlockSpec((bb, MLSTM_W), rows),
            pl.BlockSpec((bb, HEADS), rows),
            pl.BlockSpec((1, MLSTM_W), lambda i: (0, 0)),
        ],
        out_specs=[
            pl.BlockSpec((bb, MLSTM_W), rows),
            pl.BlockSpec((bb, HEADS, HEAD_DIM, HEAD_DIM), lambda i: (i, 0, 0, 0)),
            pl.BlockSpec((bb, MLSTM_W), rows),
            pl.BlockSpec((bb, HEADS), rows),
        ],
        out_shape=[
            jax.ShapeDtypeStruct((batch, MLSTM_W), BF16),
            jax.ShapeDtypeStruct(c0.shape, F32),
            jax.ShapeDtypeStruct(n0.shape, F32),
            jax.ShapeDtypeStruct(m0.shape, F32),
        ],
        compiler_params=_params(("parallel",)),
        name="mlstm1",
    )(q, k, v, o, g, c0, n0, m0, norm_w)


def _pool_kernel(u_ref, wp_ref, sc_ref, z_ref, buf_ref):
    t_len = u_ref.shape[0]
    t_idx = lax.broadcasted_iota(jnp.int32, (t_len, POOL_GROUP), 0)
    for g, w in enumerate(POOL_WINDOWS):
        cols = slice(g * POOL_GROUP, (g + 1) * POOL_GROUP)
        x = u_ref[:, cols]
        s = x
        k = 1
        while k < w:
            s = s + jnp.where(t_idx >= k, pltpu.roll(s, k, axis=0), 0.0)
            k *= 2
        cnt = jnp.minimum(t_idx + 1, w).astype(F32)
        r = s / cnt - x
        z = _dot(r.astype(BF16), wp_ref[g]) * sc_ref[:, cols]
        z_ref[:, cols] = z.astype(BF16)
    buf_ref[0] = u_ref[t_len - POOL_BUF:t_len, :]


def _pool(u, w_pool, scale, batch, t_len):
    n = u.shape[0]
    return pl.pallas_call(
        _pool_kernel,
        grid=(batch,),
        in_specs=[
            pl.BlockSpec((t_len, POOL_W), lambda b: (b, 0)),
            pl.BlockSpec(w_pool.shape, lambda b: (0, 0, 0)),
            pl.BlockSpec((1, POOL_W), lambda b: (0, 0)),
        ],
        out_specs=[
            pl.BlockSpec((t_len, POOL_W), lambda b: (b, 0)),
            pl.BlockSpec((1, POOL_BUF, POOL_W), lambda b: (b, 0, 0)),
        ],
        out_shape=[
            jax.ShapeDtypeStruct((n, POOL_W), BF16),
            jax.ShapeDtypeStruct((batch, POOL_BUF, POOL_W), F32),
        ],
        compiler_params=_params(("parallel",)),
        name="pool",
    )(u, w_pool, scale)


def _pool1_kernel(u_ref, buft_ref, wp_ref, sc_ref, z_ref, buft_out_ref):
    for g, w in enumerate(POOL_WINDOWS):
        cols = slice(g * POOL_GROUP, (g + 1) * POOL_GROUP)
        x = u_ref[:, cols]
        s = x
        for j in range(1, w):
            s = s + buft_ref[POOL_BUF - j, :, cols]
        r = s / float(w) - x
        z = _dot(r.astype(BF16), wp_ref[g]) * sc_ref[:, cols]
        z_ref[:, cols] = z.astype(BF16)
    for j in range(POOL_BUF - 1):
        buft_out_ref[j] = buft_ref[j + 1]
    buft_out_ref[POOL_BUF - 1] = u_ref[...]


def _pool1(u, buf_t, w_pool, scale):
    batch = u.shape[0]
    full2 = lambda i: (0, 0)
    full3 = lambda i: (0, 0, 0)
    return pl.pallas_call(
        _pool1_kernel,
        grid=(1,),
        in_specs=[
            pl.BlockSpec(u.shape, full2),
            pl.BlockSpec(buf_t.shape, full3),
            pl.BlockSpec(w_pool.shape, full3),
            pl.BlockSpec((1, POOL_W), full2),
        ],
        out_specs=[
            pl.BlockSpec((batch, POOL_W), full2),
            pl.BlockSpec(buf_t.shape, full3),
        ],
        out_shape=[
            jax.ShapeDtypeStruct((batch, POOL_W), BF16),
            jax.ShapeDtypeStruct(buf_t.shape, F32),
        ],
        compiler_params=_params(("arbitrary",)),
        name="pool1",
    )(u, buf_t, w_pool, scale)


def _mix_kernel(x_ref, hm_ref, zp_ref, wo_ref, g_ref, x1_ref, xnt_ref):
    mix = _dot(hm_ref[...], wo_ref[0:MLSTM_W, :]) + _dot(zp_ref[...], wo_ref[MLSTM_W:, :])
    x1 = x_ref[...] + mix
    x1_ref[...] = x1
    xnt_ref[...] = _rmsnorm(x1, g_ref[...]).T.astype(BF16)


def _mix(x, hm, zp, w_out, norm_g):
    n, d = x.shape
    tb = _token_block(n, 512)
    rows = lambda i: (i, 0)
    const = lambda i: (0, 0)
    return pl.pallas_call(
        _mix_kernel,
        grid=(n // tb,),
        in_specs=[
            pl.BlockSpec((tb, d), rows),
            pl.BlockSpec((tb, MLSTM_W), rows),
            pl.BlockSpec((tb, POOL_W), rows),
            pl.BlockSpec(w_out.shape, const),
            pl.BlockSpec((1, d), const),
        ],
        out_specs=[
            pl.BlockSpec((tb, d), rows),
            pl.BlockSpec((d, tb), lambda i: (0, i)),
        ],
        out_shape=[
            jax.ShapeDtypeStruct((n, d), F32),
            jax.ShapeDtypeStruct((d, n), BF16),
        ],
        compiler_params=_params(("parallel",)),
        name="mix",
    )(x, hm, zp, w_out, norm_g)


def _extract_max(vals, pos):
    m = vals[0]
    for v in vals[1:]:
        m = jnp.maximum(m, v)
    m = jnp.max(m, axis=0, keepdims=True)
    big = jnp.float32(1e9)
    idx = None
    for v, p in zip(vals, pos):
        cand = jnp.where(v == m, p, big)
        idx = cand if idx is None else jnp.minimum(idx, cand)
    idx = jnp.min(idx, axis=0, keepdims=True)
    return m, idx


def _top16(s):
    tb = s.shape[1]
    row = lax.broadcasted_iota(jnp.int32, s.shape, 0).astype(F32)
    krow = lax.broadcasted_iota(jnp.int32, (PEER_TOPK, tb), 0)
    rank = jnp.full(s.shape, UNRANKED, F32)
    sv = jnp.zeros((PEER_TOPK, tb), F32)
    for k in range(PEER_TOPK):
        m, idx = _extract_max([s], [row])
        hit = row == idx
        rank = jnp.where(hit, float(k), rank)
        s = jnp.where(hit, NEG_INF, s)
        sv = jnp.where(krow == k, m, sv)
    return sv, rank


def _candidate_groups(sv1, sv2):
    tb = sv1.shape[1]
    vals, pos = [], []
    r8 = lax.broadcasted_iota(jnp.int32, (8, tb), 0)
    vals.append(sv1[0:1, :] + sv2[8:16, :])
    pos.append((r8 + 8).astype(F32))
    vals.append(sv1[0:1, :] + sv2[0:8, :])
    pos.append(r8.astype(F32))
    for k1 in range(1, 8):
        lim = PEER_TOPK // (k1 + 1)
        v = sv1[k1:k1 + 1, :] + sv2[0:8, :]
        vals.append(jnp.where(r8 < lim, v, NEG_INF))
        pos.append((r8 + k1 * PEER_TOPK).astype(F32))
    vals.append(sv1[8:16, :] + sv2[0:1, :])
    pos.append(((r8 + 8) * PEER_TOPK).astype(F32))
    return vals, pos


def _sorting_pairs(n):
    pairs = []
    p = 1
    while p < n:
        k = p
        while k >= 1:
            for j in range(k % p, n - k, 2 * k):
                for i in range(min(k, n - j - k)):
                    if (i + j) // (2 * p) == (i + j + k) // (2 * p):
                        pairs.append((i + j, i + j + k))
            k //= 2
        p *= 2
    return pairs


def _sublane_allreduce(x, op):
    for d in (1, 2, 4):
        x = op(x, pltpu.roll(x, d, axis=0))
    return x


def _sorted_top16(groups):
    g = list(groups)
    for i, j in _sorting_pairs(len(g)):
        g[i], g[j] = jnp.maximum(g[i], g[j]), jnp.minimum(g[i], g[j])
    n = len(g)
    for d in (1, 2, 4):
        p = [pltpu.roll(x, d, axis=0) for x in g]
        g = [jnp.maximum(g[i], p[n - 1 - i]) for i in range(n)]
        stride = n // 2
        while stride >= 1:
            for i in range(n):
                if i & stride == 0:
                    g[i], g[i + stride] = (jnp.maximum(g[i], g[i + stride]),
                                           jnp.minimum(g[i], g[i + stride]))
            stride //= 2
    return g


def _rank_bits(x, sv):
    c8 = x < sv[7]
    t = jnp.where(c8, sv[11], sv[3])
    c4 = x < t
    t = jnp.where(c8, jnp.where(c4, sv[13], sv[9]), jnp.where(c4, sv[5], sv[1]))
    c2 = x < t
    t = jnp.where(c8,
                  jnp.where(c4, jnp.where(c2, sv[14], sv[12]), jnp.where(c2, sv[10], sv[8])),
                  jnp.where(c4, jnp.where(c2, sv[6], sv[4]), jnp.where(c2, sv[2], sv[0])))
    c1 = x < t
    return (c8, c4, c2, c1), x < sv[15]


def _select16(bits, rows):
    c8, c4, c2, c1 = bits
    lvl = [jnp.where(c1, rows[2 * i + 1], rows[2 * i]) for i in range(8)]
    lvl = [jnp.where(c2, lvl[2 * i + 1], lvl[2 * i]) for i in range(4)]
    lvl = [jnp.where(c4, lvl[2 * i + 1], lvl[2 * i]) for i in range(2)]
    return jnp.where(c8, lvl[1], lvl[0])


def _route_fast(s1, s2):
    tb = s1.shape[1]
    r8 = lax.broadcasted_iota(jnp.int32, (8, tb), 0)
    n_grp = PEER_NKEYS // 8
    g1 = [s1[v * 8:(v + 1) * 8, :] for v in range(n_grp)]
    g2 = [s2[v * 8:(v + 1) * 8, :] for v in range(n_grp)]
    sv1 = _sorted_top16(g1)
    sv2 = _sorted_top16(g2)

    bad = jnp.zeros((8, tb), F32)
    for sv, grp in ((sv1, g1), (sv2, g2)):
        for k in range(PEER_TOPK - 1):
            bad = jnp.where(sv[k] == sv[k + 1], 1.0, bad)
        n_in = jnp.zeros((8, tb), F32)
        for x in grp:
            n_in = n_in + jnp.where(x >= sv[PEER_TOPK - 1], 1.0, 0.0)
        n_in = _sublane_allreduce(n_in, jnp.add)
        bad = jnp.where(n_in != float(PEER_TOPK), 1.0, bad)

    def by_sublane(rows):
        out = jnp.zeros((8, tb), F32)
        for k, row in enumerate(rows):
            out = jnp.where(r8 == k, row, out)
        return out
    a2_lo, a2_hi, a1_hi = by_sublane(sv2[0:8]), by_sublane(sv2[8:16]), by_sublane(sv1[8:16])
    orig = [sv1[0] + a2_hi, sv1[0] + a2_lo]
    for k1 in range(1, 8):
        orig.append(jnp.where(r8 < PEER_TOPK // (k1 + 1), sv1[k1] + a2_lo, NEG_INF))
    orig.append(a1_hi + sv2[0])
    vals = list(orig)
    prev = None
    for _ in range(PEER_TOPK):
        m = vals[0]
        for v in vals[1:]:
            m = jnp.maximum(m, v)
        m = _sublane_allreduce(m, jnp.maximum)
        vals = [jnp.where(v == m, NEG_INF, v) for v in vals]
        if prev is not None:
            bad = jnp.where(m == prev, 1.0, bad)
        prev = m
    sel = [o >= prev for o in orig]
    cmax = sv1[0] + sv2[0]
    z = jnp.zeros((8, tb), F32)
    for o, sl in zip(orig, sel):
        z = z + jnp.where(sl, jnp.exp(o - cmax), 0.0)
    z = _sublane_allreduce(z, jnp.add)
    ones = [jnp.where(sl, 1.0, 0.0) for sl in sel]
    cnt = [_sublane_allreduce(ones[0] + ones[1], jnp.add)]
    cnt += [_sublane_allreduce(ones[k1 + 1], jnp.add) for k1 in range(1, 8)]
    cnt += [ones[9][j:j + 1, :] for j in range(8)]
    total = cnt[0]
    for c in cnt[1:]:
        total = total + c
    bad = jnp.where(total != float(PEER_TOPK), 1.0, bad)

    cnt1, r2 = [], []
    weights = (8.0, 4.0, 2.0, 1.0)
    for x in g1:
        bits, below = _rank_bits(x, sv1)
        cnt1.append(jnp.where(below, 0.0, _select16(bits, cnt)))
    for x in g2:
        bits, below = _rank_bits(x, sv2)
        rank = jnp.zeros((8, tb), F32)
        for b, wgt in zip(bits, weights):
            rank = rank + jnp.where(b, wgt, 0.0)
        r2.append(jnp.where(below, UNRANKED, rank))
    return cnt1, r2, sv1[0][0:1, :], sv2[0][0:1, :], z[0:1, :], bad


def _route_exact(s1, s2):
    sv1, r1 = _top16(s1)
    sv2, r2 = _top16(s2)

    vals, pos = _candidate_groups(sv1, sv2)
    orig = list(vals)
    sel = [jnp.zeros(v.shape, F32) for v in vals]
    for _ in range(PEER_TOPK):
        _, idx = _extract_max(vals, pos)
        for i in range(len(vals)):
            hit = pos[i] == idx
            sel[i] = jnp.where(hit, 1.0, sel[i])
            vals[i] = jnp.where(hit, NEG_INF, vals[i])

    cmax = orig[1][0:1, :]
    z = None
    for o, sl in zip(orig, sel):
        part = jnp.sum(jnp.where(sl > 0.0, jnp.exp(o - cmax), 0.0), axis=0, keepdims=True)
        z = part if z is None else z + part
    cnt = [jnp.sum(sel[0] + sel[1], axis=0, keepdims=True)]
    cnt += [jnp.sum(sel[k1 + 1], axis=0, keepdims=True) for k1 in range(1, 8)]
    cnt += [sel[9][j:j + 1, :] for j in range(8)]

    cnt1 = jnp.zeros(r1.shape, F32)
    for k1 in range(PEER_TOPK):
        cnt1 = jnp.where(r1 == float(k1), cnt[k1], cnt1)
    return cnt1, r2, sv1[0:1, :], sv2[0:1, :], z


ROUTE_HEADS_PER_STEP = 4


def _route_kernel(xnt_ref, wqt_ref, k1_ref, k2_ref, cnt1_ref, g1_ref, r2_ref, e2_ref, s_ref):
    qt_all = _dot(wqt_ref[...], xnt_ref[...]).astype(BF16)

    def emit_gates(hh, s1, s2, max1, max2, z):
        g1_ref[hh] = jnp.exp(s1 - max1) * (0.5 / z)
        e2_ref[hh] = jnp.exp(s2 - max2).astype(BF16)

    ties = []
    for hh in range(ROUTE_HEADS_PER_STEP):
        qt = qt_all[hh * 2 * PEER_HALF:(hh + 1) * 2 * PEER_HALF, :]
        s1 = _dot(k1_ref[hh], qt[0:PEER_HALF, :])
        s2 = _dot(k2_ref[hh], qt[PEER_HALF:, :])
        s_ref[hh, 0] = s1
        s_ref[hh, 1] = s2
        cnt1, r2, max1, max2, z, tie = _route_fast(s1, s2)
        for v in range(PEER_NKEYS // 8):
            cnt1_ref[hh, v * 8:(v + 1) * 8, :] = cnt1[v]
            r2_ref[hh, v * 8:(v + 1) * 8, :] = r2[v].astype(BF16)
        emit_gates(hh, s1, s2, max1, max2, z)
        ties.append(jnp.max(tie))

    for hh in range(ROUTE_HEADS_PER_STEP):
        @pl.when(ties[hh] > 0.0)
        def _(hh=hh):
            s1, s2 = s_ref[hh, 0], s_ref[hh, 1]
            cnt1, r2, max1, max2, z = _route_exact(s1, s2)
            cnt1_ref[hh] = cnt1
            r2_ref[hh] = r2.astype(BF16)
            emit_gates(hh, s1, s2, max1, max2, z)


def _route(xnt, wqt, keys):
    d, n = xnt.shape
    tb = _token_block(n, 256)
    hps = ROUTE_HEADS_PER_STEP
    per_head = pl.BlockSpec((hps, PEER_NKEYS, tb), lambda i, h: (h, 0, i))
    shape = lambda dt: jax.ShapeDtypeStruct((PEER_HEADS, PEER_NKEYS, n), dt)
    return pl.pallas_call(
        _route_kernel,
        grid=(n // tb, PEER_HEADS // hps),
        in_specs=[
            pl.BlockSpec((d, tb), lambda i, h: (0, i)),
            pl.BlockSpec((hps * 2 * PEER_HALF, d), lambda i, h: (h, 0)),
            pl.BlockSpec((hps, PEER_NKEYS, PEER_HALF), lambda i, h: (h, 0, 0)),
            pl.BlockSpec((hps, PEER_NKEYS, PEER_HALF), lambda i, h: (h + PEER_HEADS // hps, 0, 0)),
        ],
        out_specs=[per_head] * 4,
        out_shape=[shape(F32), shape(F32), shape(BF16), shape(BF16)],
        scratch_shapes=[pltpu.VMEM((hps, 2, PEER_NKEYS, tb), F32)],
        compiler_params=_params(("parallel", "parallel")),
        name="route",
    )(xnt, wqt, keys, keys)


I1_PER_TILE = 8
BF16_TILE = (8, 2 * LANE)
EXPERT_TILE = I1_PER_TILE * PEER_NKEYS


def _experts_kernel(xnt_ref, cnt1_ref, g1_ref, r2_ref, e2_ref, u_ref, vt_ref, x1_ref,
                    x2_ref, acc_ref, act_ref, wa_ref):
    e = pl.program_id(1)
    tb = xnt_ref.shape[1]

    @pl.when(e == 0)
    def _():
        acc_ref[...] = jnp.zeros_like(acc_ref)

    a = _dot(u_ref[...], xnt_ref[...]).astype(BF16)
    act = a * (1.0 + jnp.tanh(a * (GELU_C0 + GELU_C1 * (a * a))))
    act_ref[...] = act.reshape(act_ref.shape)
    sub, width = BF16_TILE[0], min(BF16_TILE[1], tb)
    assert sub == I1_PER_TILE
    i1_rows = pl.ds(pl.multiple_of(e * I1_PER_TILE, I1_PER_TILE), I1_PER_TILE)
    for lt in range(tb // width):
        lanes = pl.ds(lt * width, width)
        cnt_tile = [cnt1_ref[h, i1_rows, lanes] for h in range(PEER_HEADS)]
        g1_tile = [g1_ref[h, i1_rows, lanes] for h in range(PEER_HEADS)]
        for j in range(I1_PER_TILE):
            cnt_b = [jnp.broadcast_to(t[j:j + 1, :], (sub, width)).astype(BF16) for t in cnt_tile]
            g1_b = [jnp.broadcast_to(t[j:j + 1, :], (sub, width)).astype(BF16) for t in g1_tile]
            for rb in range(PEER_NKEYS // sub):
                krows = pl.ds(rb * sub, sub)
                w = None
                for h in range(PEER_HEADS):
                    mask = r2_ref[h, krows, lanes] < cnt_b[h]
                    term = jnp.where(mask, e2_ref[h, krows, lanes] * g1_b[h], jnp.zeros((), BF16))
                    w = term if w is None else w + term
                grp = j * (PEER_NKEYS // sub) + rb
                wa_ref[grp, :, lanes] = w * act_ref[grp, :, lanes]
    acc_ref[...] += _dot(vt_ref[...], wa_ref[...].reshape(EXPERT_TILE, tb))

    @pl.when(e == pl.num_programs(1) - 1)
    def _():
        x2_ref[...] = x1_ref[...] + acc_ref[...].T


def _experts(xnt, cnt1, g1, r2, e2, u_tab, vt_tab, x1):
    d, n = xnt.shape
    n_exp = u_tab.shape[0]
    tb = _token_block(n, 512)
    assert tb % LANE == 0 and n_exp % EXPERT_TILE == 0
    per_head = pl.BlockSpec((PEER_HEADS, PEER_NKEYS, tb), lambda i, e: (0, 0, i))
    tile_buf = pltpu.VMEM((EXPERT_TILE // BF16_TILE[0], BF16_TILE[0], tb), BF16)
    return pl.pallas_call(
        _experts_kernel,
        grid=(n // tb, n_exp // EXPERT_TILE),
        in_specs=[
            pl.BlockSpec((d, tb), lambda i, e: (0, i)),
            per_head, per_head, per_head, per_head,
            pl.BlockSpec((EXPERT_TILE, d), lambda i, e: (e, 0)),
            pl.BlockSpec((d, EXPERT_TILE), lambda i, e: (0, e)),
            pl.BlockSpec((tb, d), lambda i, e: (i, 0)),
        ],
        out_specs=pl.BlockSpec((tb, d), lambda i, e: (i, 0)),
        out_shape=jax.ShapeDtypeStruct((n, d), F32),
        scratch_shapes=[pltpu.VMEM((d, tb), F32), tile_buf, tile_buf],
        compiler_params=_params(("parallel", "arbitrary")),
        name="experts",
    )(xnt, cnt1, g1, r2, e2, u_tab, vt_tab, x1)


def _ple_kernel(x_ref, p_ref, wg_ref, wp_ref, nf_ref, y_ref, *, final_norm):
    x = x_ref[...]
    gate = _sigmoid(_dot(x.astype(BF16), wg_ref[...]))
    y = x + gate * _dot(p_ref[...].astype(BF16), wp_ref[...])
    if final_norm:
        y = _rmsnorm(y, nf_ref[...])
    y_ref[...] = y


def _ple(x, p, w_gate, w_ple, norm_f, final_norm):
    n, d = x.shape
    tb = _token_block(n, 512)
    rows = lambda i: (i, 0)
    const = lambda i: (0, 0)
    return pl.pallas_call(
        functools.partial(_ple_kernel, final_norm=final_norm),
        grid=(n // tb,),
        in_specs=[
            pl.BlockSpec((tb, d), rows),
            pl.BlockSpec((tb, p.shape[1]), rows),
            pl.BlockSpec(w_gate.shape, const),
            pl.BlockSpec(w_ple.shape, const),
            pl.BlockSpec((1, d), const),
        ],
        out_specs=pl.BlockSpec((tb, d), rows),
        out_shape=jax.ShapeDtypeStruct((n, d), F32),
        compiler_params=_params(("parallel",)),
        name="ple",
    )(x, p, w_gate, w_ple, norm_f)


def _prep_layer_weights(l, w_in, b_gate, mlstm_norm, w_pool, pool_scale, w_out, norm1, norm2,
                        peer_wq, peer_keys, peer_u, peer_v, w_ple, w_gate):
    w = MLSTM_W
    wi = w_in[l]
    d = wi.shape[0]
    wa = jnp.concatenate([wi[:, 0:w], wi[:, 2 * w:4 * w + POOL_W]], axis=1).astype(BF16)
    wkt = wi[:, w:2 * w].T.astype(BF16)
    wgt = wi[:, 4 * w + POOL_W:].T.astype(BF16)
    return dict(
        wa=wa, wkt=wkt, wgt=wgt,
        bg=b_gate[l].reshape(2 * HEADS, 1).astype(F32),
        mlstm_norm=mlstm_norm[l].reshape(1, w),
        w_pool=w_pool[l].astype(BF16),
        pool_scale=pool_scale[l].reshape(1, POOL_W),
        w_out=w_out[l].astype(BF16),
        norm1=norm1[l].reshape(1, d), norm2=norm2[l].reshape(1, d),
        wqt=peer_wq[l].T.astype(BF16),
        keys=peer_keys[l].reshape(2 * PEER_HEADS, PEER_NKEYS, PEER_HALF).astype(BF16),
        u_tab=peer_u[l].astype(BF16),
        vt_tab=peer_v[l].T.astype(BF16),
        w_ple=w_ple[l].astype(BF16), w_gate=w_gate[l].astype(BF16),
    )


def _ffn_and_embed(x1, xnt, p, lw, norm_f, final_norm):
    cnt1, g1, r2, e2 = _route(xnt, lw["wqt"], lw["keys"])
    x2 = _experts(xnt, cnt1, g1, r2, e2, lw["u_tab"], lw["vt_tab"], x1)
    return _ple(x2, p, lw["w_gate"], lw["w_ple"], norm_f, final_norm)


def _prompt_layer(x, p, lw, norm_f, final_norm, batch, t_len):
    q, kt, v, o, u, gt = _proj(x, lw["norm1"], lw["wa"], lw["wkt"], lw["wgt"], lw["bg"])
    gates3 = gt.reshape(2 * HEADS, (batch * t_len) // CHUNK, CHUNK)
    hm, cext, m8 = _mlstm(q, kt, v, o, gates3, lw["mlstm_norm"], batch, t_len)
    zp, buf = _pool(u, lw["w_pool"], lw["pool_scale"], batch, t_len)
    x1, xnt = _mix(x, hm, zp, lw["w_out"], lw["norm2"])
    y = _ffn_and_embed(x1, xnt, p, lw, norm_f, final_norm)
    return y, cext[..., :HEAD_DIM], cext[..., HEAD_DIM], m8[..., 0, 0], buf


def _sample_layer(x, p, c0, n0, m0, buf0, lw, norm_f, final_norm):
    q, kt, v, o, u, gt = _proj(x, lw["norm1"], lw["wa"], lw["wkt"], lw["wgt"], lw["bg"])
    hm, c1, n1, m1 = _mlstm1(q, kt.T, v, o, gt.T, c0, n0.reshape(-1, MLSTM_W), m0, lw["mlstm_norm"])
    n1 = n1.reshape(n0.shape)
    zp, buf_t = _pool1(u, jnp.swapaxes(buf0, 0, 1), lw["w_pool"], lw["pool_scale"])
    x1, xnt = _mix(x, hm, zp, lw["w_out"], lw["norm2"])
    y = _ffn_and_embed(x1, xnt, p, lw, norm_f, final_norm)
    return y, c1, n1, m1, jnp.swapaxes(buf_t, 0, 1)


def kernel(x_prompt, x_sample, p_prompt, p_sample, state_C, state_n, state_m, state_pool,
           w_in, b_gate, mlstm_norm, w_pool, pool_scale, w_out, norm1, norm2,
           peer_wq, peer_keys, peer_u, peer_v, w_ple, w_gate, norm_f):
    depth = w_in.shape[0]
    batch, t_len, d = x_prompt.shape
    dec_batch, dec_len, _ = x_sample.shape
    assert dec_len == 1 and t_len % CHUNK == 0
    nf = norm_f.reshape(1, d)

    xp = x_prompt.reshape(batch * t_len, d)
    xs = x_sample.reshape(dec_batch, d)
    outs_p, outs_s = [], []
    for l in range(depth):
        lw = _prep_layer_weights(l, w_in, b_gate, mlstm_norm, w_pool, pool_scale, w_out, norm1,
                                 norm2, peer_wq, peer_keys, peer_u, peer_v, w_ple, w_gate)
        final = l == depth - 1
        pp = p_prompt[l].reshape(batch * t_len, -1)
        ps = p_sample[l].reshape(dec_batch, -1)
        xp, c_p, n_p, m_p, buf_p = _prompt_layer(xp, pp, lw, nf, final, batch, t_len)
        xs, c_s, n_s, m_s, buf_s = _sample_layer(xs, ps, state_C[l], state_n[l], state_m[l],
                                                 state_pool[l], lw, nf, final)
        outs_p.append((c_p, n_p, m_p, buf_p))
        outs_s.append((c_s, n_s, m_s, buf_s))

    stack = lambda outs, i: jnp.stack([o[i] for o in outs])
    return (xp.reshape(batch, t_len, d), xs.reshape(dec_batch, dec_len, d),
            stack(outs_p, 0), stack(outs_p, 1), stack(outs_p, 2), stack(outs_p, 3),
            stack(outs_s, 0), stack(outs_s, 1), stack(outs_s, 2), stack(outs_s, 3))
```

```python
import functools
import math

import jax
import jax.numpy as jnp
from jax import lax
from jax.experimental import pallas as pl
from jax.experimental.pallas import tpu as pltpu

F32 = jnp.float32
BF16 = jnp.bfloat16
EPS = 1e-6
NEG_INF = float("-inf")

LANE = 128
V7X_VMEM_LIMIT = 56 * 1024 * 1024

HEADS = 4
HEAD_DIM = 128
MLSTM_W = HEADS * HEAD_DIM
POOL_WINDOWS = (2, 4, 8, 16)
POOL_GROUP = 128
POOL_W = POOL_GROUP * len(POOL_WINDOWS)
POOL_BUF = max(POOL_WINDOWS) - 1
CHUNK = 128
PEER_HEADS = 8
PEER_NKEYS = 128
PEER_HALF = 128
PEER_TOPK = 16
UNRANKED = 127.0
GELU_C0 = math.sqrt(2.0 / math.pi)
GELU_C1 = 0.044715 * GELU_C0


def _params(semantics):
    return pltpu.CompilerParams(dimension_semantics=semantics, vmem_limit_bytes=V7X_VMEM_LIMIT)


def _token_block(n, want):
    tb = min(n, want)
    assert n % tb == 0
    return tb


def _rmsnorm(x, g):
    return x * lax.rsqrt(jnp.mean(x * x, axis=-1, keepdims=True) + EPS) * g


def _log_sigmoid(x):
    return jnp.minimum(x, 0.0) - jnp.log(1.0 + jnp.exp(-jnp.abs(x)))


def _sigmoid(x):
    return 1.0 / (1.0 + jnp.exp(-x))


def _dot(a, b):
    return jnp.dot(a, b, preferred_element_type=F32)


def _dot_nt(a, b):
    return lax.dot_general(a, b, (((1,), (1,)), ((), ())), preferred_element_type=F32)


def _proj_kernel(x_ref, g_ref, wa_ref, wkt_ref, wgt_ref, bg_ref,
                 q_ref, kt_ref, v_ref, o_ref, u_ref, gt_ref):
    hn = _rmsnorm(x_ref[...], g_ref[...]).astype(BF16)
    pa = _dot(hn, wa_ref[...])
    w = MLSTM_W
    q_ref[...] = pa[:, 0:w].astype(BF16)
    v_ref[...] = pa[:, w:2 * w].astype(BF16)
    o_ref[...] = _sigmoid(pa[:, 2 * w:3 * w])
    u_ref[...] = pa[:, 3 * w:4 * w]
    kt = _dot_nt(wkt_ref[...], hn) * (HEAD_DIM ** -0.5)
    kt_ref[...] = kt.astype(BF16)
    gt_ref[...] = _dot_nt(wgt_ref[...], hn) + bg_ref[...]


def _proj(x, norm_g, wa, wkt, wgt, bg):
    n, d = x.shape
    tb = _token_block(n, 512)
    grid = (n // tb,)
    const = lambda i: (0, 0)
    return pl.pallas_call(
        _proj_kernel,
        grid=grid,
        in_specs=[
            pl.BlockSpec((tb, d), lambda i: (i, 0)),
            pl.BlockSpec((1, d), const),
            pl.BlockSpec(wa.shape, const),
            pl.BlockSpec(wkt.shape, const),
            pl.BlockSpec(wgt.shape, const),
            pl.BlockSpec(bg.shape, const),
        ],
        out_specs=[
            pl.BlockSpec((tb, MLSTM_W), lambda i: (i, 0)),
            pl.BlockSpec((MLSTM_W, tb), lambda i: (0, i)),
            pl.BlockSpec((tb, MLSTM_W), lambda i: (i, 0)),
            pl.BlockSpec((tb, MLSTM_W), lambda i: (i, 0)),
            pl.BlockSpec((tb, POOL_W), lambda i: (i, 0)),
            pl.BlockSpec((2 * HEADS, tb), lambda i: (0, i)),
        ],
        out_shape=[
            jax.ShapeDtypeStruct((n, MLSTM_W), BF16),
            jax.ShapeDtypeStruct((MLSTM_W, n), BF16),
            jax.ShapeDtypeStruct((n, MLSTM_W), BF16),
            jax.ShapeDtypeStruct((n, MLSTM_W), F32),
            jax.ShapeDtypeStruct((n, POOL_W), F32),
            jax.ShapeDtypeStruct((2 * HEADS, n), F32),
        ],
        compiler_params=_params(("parallel",)),
        name="proj",
    )(x, norm_g, wa, wkt, wgt, bg)


def _lane_cumsum(x):
    lane = lax.broadcasted_iota(jnp.int32, x.shape, 1)
    shift = 1
    while shift < x.shape[1]:
        x = x + jnp.where(lane >= shift, pltpu.roll(x, shift, axis=1), 0.0)
        shift *= 2
    return x


def _mlstm_kernel(q_ref, kt_ref, v_ref, o_ref, ig_ref, lf_ref, nw_ref,
                  hm_ref, cext_out_ref, m_out_ref, cext_ref):
    t_len = q_ref.shape[0]
    n_chunks = t_len // CHUNK
    ig2 = ig_ref[0]
    lf2 = _log_sigmoid(lf_ref[0])
    b2 = _lane_cumsum(lf2)
    a2 = ig2 - b2

    t_idx = lax.broadcasted_iota(jnp.int32, (CHUNK, CHUNK), 0)
    s_idx = lax.broadcasted_iota(jnp.int32, (CHUNK, CHUNK), 1)
    causal = s_idx <= t_idx
    ones_col = (lax.broadcasted_iota(jnp.int32, (CHUNK, HEAD_DIM), 1) == 0).astype(BF16)
    nw = nw_ref[...]

    cext_ref[...] = jnp.zeros_like(cext_ref)
    m = jnp.zeros((1, 1), F32)
    for c in range(n_chunks):
        rows = pl.ds(c * CHUNK, CHUNK)
        a_row = a2[c:c + 1, :]
        lf_row = lf2[c:c + 1, :]
        b_last = b2[c:c + 1, CHUNK - 1:CHUNK]
        b_col = jnp.sum(jnp.where(causal, lf_row, 0.0), axis=1, keepdims=True)
        amax_col = jnp.max(jnp.where(causal, a_row, NEG_INF), axis=1, keepdims=True)
        m_col = jnp.maximum(amax_col, m)
        decay_mat = jnp.where(causal, jnp.exp(a_row - m_col), 0.0)

        qc = q_ref[rows, :]
        ktc = kt_ref[:, rows]
        v_ext = jnp.concatenate([v_ref[rows, :], ones_col], axis=1)
        s = _dot(qc, ktc) * decay_mat
        intra = _dot(s.astype(BF16), v_ext)
        inter = _dot(qc, cext_ref[...].astype(BF16))
        tot = intra + jnp.exp(m - m_col) * inter
        num = tot[:, :HEAD_DIM]
        den = tot[:, HEAD_DIM:HEAD_DIM + 1]
        h = num / jnp.maximum(jnp.abs(den), jnp.exp(-(b_col + m_col)))
        h = h * lax.rsqrt(jnp.mean(h * h, axis=1, keepdims=True) + EPS)
        hm_ref[rows, :] = (h * nw * o_ref[rows, :]).astype(BF16)

        m_new = jnp.maximum(b_last + m, jnp.max(a_row, axis=1, keepdims=True) + b_last)
        ws_row = jnp.exp(a_row + (b_last - m_new))
        kws = (ktc.astype(F32) * ws_row).astype(BF16)
        cext_ref[...] = jnp.exp(b_last + m - m_new) * cext_ref[...] + _dot(kws, v_ext)
        m = m_new

    cext_out_ref[0, 0] = cext_ref[...]
    m_out_ref[0, 0] = jnp.broadcast_to(m, m_out_ref.shape[2:])


def _mlstm(q, kt, v, o, gates3, norm_w, batch, t_len):
    n = q.shape[0]
    n_chunks = t_len // CHUNK
    seq = lambda b, h: (b, h)
    return pl.pallas_call(
        _mlstm_kernel,
        grid=(batch, HEADS),
        in_specs=[
            pl.BlockSpec((t_len, HEAD_DIM), seq),
            pl.BlockSpec((HEAD_DIM, t_len), lambda b, h: (h, b)),
            pl.BlockSpec((t_len, HEAD_DIM), seq),
            pl.BlockSpec((t_len, HEAD_DIM), seq),
            pl.BlockSpec((1, n_chunks, CHUNK), lambda b, h: (h, b, 0)),
            pl.BlockSpec((1, n_chunks, CHUNK), lambda b, h: (h + HEADS, b, 0)),
            pl.BlockSpec((1, HEAD_DIM), lambda b, h: (0, h)),
        ],
        out_specs=[
            pl.BlockSpec((t_len, HEAD_DIM), seq),
            pl.BlockSpec((1, 1, HEAD_DIM, 2 * HEAD_DIM), lambda b, h: (b, h, 0, 0)),
            pl.BlockSpec((1, 1, 8, LANE), lambda b, h: (b, h, 0, 0)),
        ],
        out_shape=[
            jax.ShapeDtypeStruct((n, MLSTM_W), BF16),
            jax.ShapeDtypeStruct((batch, HEADS, HEAD_DIM, 2 * HEAD_DIM), F32),
            jax.ShapeDtypeStruct((batch, HEADS, 8, LANE), F32),
        ],
        scratch_shapes=[pltpu.VMEM((HEAD_DIM, 2 * HEAD_DIM), F32)],
        compiler_params=_params(("parallel", "parallel")),
        name="mlstm",
    )(q, kt, v, o, gates3, gates3, norm_w)


def _mlstm1_kernel(q_ref, k_ref, v_ref, o_ref, g_ref, c_ref, n_ref, m_ref, nw_ref,
                   hm_ref, c_out_ref, n_out_ref, m_out_ref):
    bb = q_ref.shape[0]
    row = lax.broadcasted_iota(jnp.int32, (bb, HEAD_DIM), 0)
    eye = (lax.broadcasted_iota(jnp.int32, (HEAD_DIM, HEAD_DIM), 0)
           == lax.broadcasted_iota(jnp.int32, (HEAD_DIM, HEAD_DIM), 1))
    lane_h = lax.broadcasted_iota(jnp.int32, (bb, HEADS), 1)
    g = g_ref[...]
    m_all = m_ref[...]
    m_out = jnp.zeros((bb, HEADS), F32)
    for h in range(HEADS):
        cols = slice(h * HEAD_DIM, (h + 1) * HEAD_DIM)
        qb = q_ref[:, cols]
        qf = qb.astype(F32)
        kf = k_ref[:, cols].astype(F32)
        vf = v_ref[:, cols].astype(F32)
        ig = g[:, h:h + 1]
        lf = _log_sigmoid(g[:, HEADS + h:HEADS + h + 1])
        m_old = m_all[:, h:h + 1]
        n_old = n_ref[:, cols]
        m_new = jnp.maximum(ig, lf + m_old)
        w_in = jnp.exp(ig - m_new)
        w_st = jnp.exp(lf + m_old - m_new)
        qc = jnp.zeros((bb, HEAD_DIM), F32)
        for j in range(bb):
            c_old = c_ref[j, h]
            res = _dot(qb, c_old.astype(BF16))
            qc = jnp.where(row == j, res, qc)
            k_col = jnp.sum(jnp.where(eye, kf[j:j + 1, :], 0.0), axis=1, keepdims=True)
            c_out_ref[j, h] = w_st[j:j + 1, :] * c_old + k_col * (w_in[j:j + 1, :] * vf[j:j + 1, :])
        s = jnp.sum(qf * kf, axis=1, keepdims=True) * w_in
        num = s * vf + w_st * qc
        den = s + w_st * jnp.sum(qf * n_old, axis=1, keepdims=True)
        hh = num / jnp.maximum(jnp.abs(den), jnp.exp(-m_new))
        hh = hh * lax.rsqrt(jnp.mean(hh * hh, axis=1, keepdims=True) + EPS)
        hm_ref[:, cols] = (hh * nw_ref[:, cols] * o_ref[:, cols]).astype(BF16)
        n_out_ref[:, cols] = w_st * n_old + w_in * kf
        m_out = jnp.where(lane_h == h, m_new, m_out)
    m_out_ref[...] = m_out


def _mlstm1(q, k, v, o, g, c0, n0, m0, norm_w):
    batch = q.shape[0]
    bb = _token_block(batch, 16)
    rows = lambda i: (i, 0)
    return pl.pallas_call(
        _mlstm1_kernel,
        grid=(batch // bb,),
        in_specs=[
            pl.BlockSpec((bb, MLSTM_W), rows),
            pl.BlockSpec((bb, MLSTM_W), rows),
            pl.BlockSpec((bb, MLSTM_W), rows),
            pl.BlockSpec((bb, MLSTM_W), rows),
            pl.BlockSpec((bb, 2 * HEADS), rows),
            pl.BlockSpec((bb, HEADS, HEAD_DIM, HEAD_DIM), lambda i: (i, 0, 0, 0)),
            pl.BlockSpec((bb, MLSTM_W), rows),
            pl.BlockSpec((bb, HEADS), rows),
            pl.BlockSpec((1, MLSTM_W), lambda i: (0, 0)),
        ],
        out_specs=[
            pl.BlockSpec((bb, MLSTM_W), rows),
            pl.BlockSpec((bb, HEADS, HEAD_DIM, HEAD_DIM), lambda i: (i, 0, 0, 0)),
            pl.BlockSpec((bb, MLSTM_W), rows),
            pl.BlockSpec((bb, HEADS), rows),
        ],
        out_shape=[
            jax.ShapeDtypeStruct((batch, MLSTM_W), BF16),
            jax.ShapeDtypeStruct(c0.shape, F32),
            jax.ShapeDtypeStruct(n0.shape, F32),
            jax.ShapeDtypeStruct(m0.shape, F32),
        ],
        compiler_params=_params(("parallel",)),
        name="mlstm1",
    )(q, k, v, o, g, c0, n0, m0, norm_w)


def _pool_kernel(u_ref, wp_ref, sc_ref, z_ref, buf_ref):
    t_len = u_ref.shape[0]
    t_idx = lax.broadcasted_iota(jnp.int32, (t_len, POOL_GROUP), 0)
    for g, w in enumerate(POOL_WINDOWS):
        cols = slice(g * POOL_GROUP, (g + 1) * POOL_GROUP)
        x = u_ref[:, cols]
        s = x
        k = 1
        while k < w:
            s = s + jnp.where(t_idx >= k, pltpu.roll(s, k, axis=0), 0.0)
            k *= 2
        cnt = jnp.minimum(t_idx + 1, w).astype(F32)
        r = s / cnt - x
        z = _dot(r.astype(BF16), wp_ref[g]) * sc_ref[:, cols]
        z_ref[:, cols] = z.astype(BF16)
    buf_ref[0] = u_ref[t_len - POOL_BUF:t_len, :]


def _pool(u, w_pool, scale, batch, t_len):
    n = u.shape[0]
    return pl.pallas_call(
        _pool_kernel,
        grid=(batch,),
        in_specs=[
            pl.BlockSpec((t_len, POOL_W), lambda b: (b, 0)),
            pl.BlockSpec(w_pool.shape, lambda b: (0, 0, 0)),
            pl.BlockSpec((1, POOL_W), lambda b: (0, 0)),
        ],
        out_specs=[
            pl.BlockSpec((t_len, POOL_W), lambda b: (b, 0)),
            pl.BlockSpec((1, POOL_BUF, POOL_W), lambda b: (b, 0, 0)),
        ],
        out_shape=[
            jax.ShapeDtypeStruct((n, POOL_W), BF16),
            jax.ShapeDtypeStruct((batch, POOL_BUF, POOL_W), F32),
        ],
        compiler_params=_params(("parallel",)),
        name="pool",
    )(u, w_pool, scale)


def _pool1_kernel(u_ref, buft_ref, wp_ref, sc_ref, z_ref, buft_out_ref):
    for g, w in enumerate(POOL_WINDOWS):
        cols = slice(g * POOL_GROUP, (g + 1) * POOL_GROUP)
        x = u_ref[:, cols]
        s = x
        for j in range(1, w):
            s = s + buft_ref[POOL_BUF - j, :, cols]
        r = s / float(w) - x
        z = _dot(r.astype(BF16), wp_ref[g]) * sc_ref[:, cols]
        z_ref[:, cols] = z.astype(BF16)
    for j in range(POOL_BUF - 1):
        buft_out_ref[j] = buft_ref[j + 1]
    buft_out_ref[POOL_BUF - 1] = u_ref[...]


def _pool1(u, buf_t, w_pool, scale):
    batch = u.shape[0]
    full2 = lambda i: (0, 0)
    full3 = lambda i: (0, 0, 0)
    return pl.pallas_call(
        _pool1_kernel,
        grid=(1,),
        in_specs=[
            pl.BlockSpec(u.shape, full2),
            pl.BlockSpec(buf_t.shape, full3),
            pl.BlockSpec(w_pool.shape, full3),
            pl.BlockSpec((1, POOL_W), full2),
        ],
        out_specs=[
            pl.BlockSpec((batch, POOL_W), full2),
            pl.BlockSpec(buf_t.shape, full3),
        ],
        out_shape=[
            jax.ShapeDtypeStruct((batch, POOL_W), BF16),
            jax.ShapeDtypeStruct(buf_t.shape, F32),
        ],
        compiler_params=_params(("arbitrary",)),
        name="pool1",
    )(u, buf_t, w_pool, scale)


def _mix_kernel(x_ref, hm_ref, zp_ref, wo_ref, g_ref, x1_ref, xnt_ref):
    mix = _dot(hm_ref[...], wo_ref[0:MLSTM_W, :]) + _dot(zp_ref[...], wo_ref[MLSTM_W:, :])
    x1 = x_ref[...] + mix
    x1_ref[...] = x1
    xnt_ref[...] = _rmsnorm(x1, g_ref[...]).T.astype(BF16)


def _mix(x, hm, zp, w_out, norm_g):
    n, d = x.shape
    tb = _token_block(n, 512)
    rows = lambda i: (i, 0)
    const = lambda i: (0, 0)
    return pl.pallas_call(
        _mix_kernel,
        grid=(n // tb,),
        in_specs=[
            pl.BlockSpec((tb, d), rows),
            pl.BlockSpec((tb, MLSTM_W), rows),
            pl.BlockSpec((tb, POOL_W), rows),
            pl.BlockSpec(w_out.shape, const),
            pl.BlockSpec((1, d), const),
        ],
        out_specs=[
            pl.BlockSpec((tb, d), rows),
            pl.BlockSpec((d, tb), lambda i: (0, i)),
        ],
        out_shape=[
            jax.ShapeDtypeStruct((n, d), F32),
            jax.ShapeDtypeStruct((d, n), BF16),
        ],
        compiler_params=_params(("parallel",)),
        name="mix",
    )(x, hm, zp, w_out, norm_g)


def _extract_max(vals, pos):
    m = vals[0]
    for v in vals[1:]:
        m = jnp.maximum(m, v)
    m = jnp.max(m, axis=0, keepdims=True)
    big = jnp.float32(1e9)
    idx = None
    for v, p in zip(vals, pos):
        cand = jnp.where(v == m, p, big)
        idx = cand if idx is None else jnp.minimum(idx, cand)
    idx = jnp.min(idx, axis=0, keepdims=True)
    return m, idx


def _top16(s):
    tb = s.shape[1]
    row = lax.broadcasted_iota(jnp.int32, s.shape, 0).astype(F32)
    krow = lax.broadcasted_iota(jnp.int32, (PEER_TOPK, tb), 0)
    rank = jnp.full(s.shape, UNRANKED, F32)
    sv = jnp.zeros((PEER_TOPK, tb), F32)
    for k in range(PEER_TOPK):
        m, idx = _extract_max([s], [row])
        hit = row == idx
        rank = jnp.where(hit, float(k), rank)
        s = jnp.where(hit, NEG_INF, s)
        sv = jnp.where(krow == k, m, sv)
    return sv, rank


def _candidate_groups(sv1, sv2):
    tb = sv1.shape[1]
    vals, pos = [], []
    r8 = lax.broadcasted_iota(jnp.int32, (8, tb), 0)
    vals.append(sv1[0:1, :] + sv2[8:16, :])
    pos.append((r8 + 8).astype(F32))
    vals.append(sv1[0:1, :] + sv2[0:8, :])
    pos.append(r8.astype(F32))
    for k1 in range(1, 8):
        lim = PEER_TOPK // (k1 + 1)
        v = sv1[k1:k1 + 1, :] + sv2[0:8, :]
        vals.append(jnp.where(r8 < lim, v, NEG_INF))
        pos.append((r8 + k1 * PEER_TOPK).astype(F32))
    vals.append(sv1[8:16, :] + sv2[0:1, :])
    pos.append(((r8 + 8) * PEER_TOPK).astype(F32))
    return vals, pos


def _sorting_pairs(n):
    pairs = []
    p = 1
    while p < n:
        k = p
        while k >= 1:
            for j in range(k % p, n - k, 2 * k):
                for i in range(min(k, n - j - k)):
                    if (i + j) // (2 * p) == (i + j + k) // (2 * p):
                        pairs.append((i + j, i + j + k))
            k //= 2
        p *= 2
    return pairs


def _sublane_allreduce(x, op):
    for d in (1, 2, 4):
        x = op(x, pltpu.roll(x, d, axis=0))
    return x


def _sorted_top16(groups):
    g = list(groups)
    for i, j in _sorting_pairs(len(g)):
        g[i], g[j] = jnp.maximum(g[i], g[j]), jnp.minimum(g[i], g[j])
    n = len(g)
    for d in (1, 2, 4):
        p = [pltpu.roll(x, d, axis=0) for x in g]
        g = [jnp.maximum(g[i], p[n - 1 - i]) for i in range(n)]
        stride = n // 2
        while stride >= 1:
            for i in range(n):
                if i & stride == 0:
                    g[i], g[i + stride] = (jnp.maximum(g[i], g[i + stride]),
                                           jnp.minimum(g[i], g[i + stride]))
            stride //= 2
    return g


def _rank_bits(x, sv):
    c8 = x < sv[7]
    t = jnp.where(c8, sv[11], sv[3])
    c4 = x < t
    t = jnp.where(c8, jnp.where(c4, sv[13], sv[9]), jnp.where(c4, sv[5], sv[1]))
    c2 = x < t
    t = jnp.where(c8,
                  jnp.where(c4, jnp.where(c2, sv[14], sv[12]), jnp.where(c2, sv[10], sv[8])),
                  jnp.where(c4, jnp.where(c2, sv[6], sv[4]), jnp.where(c2, sv[2], sv[0])))
    c1 = x < t
    return (c8, c4, c2, c1), x < sv[15]


def _select16(bits, rows):
    c8, c4, c2, c1 = bits
    lvl = [jnp.where(c1, rows[2 * i + 1], rows[2 * i]) for i in range(8)]
    lvl = [jnp.where(c2, lvl[2 * i + 1], lvl[2 * i]) for i in range(4)]
    lvl = [jnp.where(c4, lvl[2 * i + 1], lvl[2 * i]) for i in range(2)]
    return jnp.where(c8, lvl[1], lvl[0])


def _route_fast(s1, s2):
    tb = s1.shape[1]
    r8 = lax.broadcasted_iota(jnp.int32, (8, tb), 0)
    n_grp = PEER_NKEYS // 8
    g1 = [s1[v * 8:(v + 1) * 8, :] for v in range(n_grp)]
    g2 = [s2[v * 8:(v + 1) * 8, :] for v in range(n_grp)]
    sv1 = _sorted_top16(g1)
    sv2 = _sorted_top16(g2)

    bad = jnp.zeros((8, tb), F32)
    for sv, grp in ((sv1, g1), (sv2, g2)):
        for k in range(PEER_TOPK - 1):
            bad = jnp.where(sv[k] == sv[k + 1], 1.0, bad)
        n_in = jnp.zeros((8, tb), F32)
        for x in grp:
            n_in = n_in + jnp.where(x >= sv[PEER_TOPK - 1], 1.0, 0.0)
        n_in = _sublane_allreduce(n_in, jnp.add)
        bad = jnp.where(n_in != float(PEER_TOPK), 1.0, bad)

    def by_sublane(rows):
        out = jnp.zeros((8, tb), F32)
        for k, row in enumerate(rows):
            out = jnp.where(r8 == k, row, out)
        return out
    a2_lo, a2_hi, a1_hi = by_sublane(sv2[0:8]), by_sublane(sv2[8:16]), by_sublane(sv1[8:16])
    orig = [sv1[0] + a2_hi, sv1[0] + a2_lo]
    for k1 in range(1, 8):
        orig.append(jnp.where(r8 < PEER_TOPK // (k1 + 1), sv1[k1] + a2_lo, NEG_INF))
    orig.append(a1_hi + sv2[0])
    vals = list(orig)
    prev = None
    for _ in range(PEER_TOPK):
        m = vals[0]
        for v in vals[1:]:
            m = jnp.maximum(m, v)
        m = _sublane_allreduce(m, jnp.maximum)
        vals = [jnp.where(v == m, NEG_INF, v) for v in vals]
        if prev is not None:
            bad = jnp.where(m == prev, 1.0, bad)
        prev = m
    sel = [o >= prev for o in orig]
    cmax = sv1[0] + sv2[0]
    z = jnp.zeros((8, tb), F32)
    for o, sl in zip(orig, sel):
        z = z + jnp.where(sl, jnp.exp(o - cmax), 0.0)
    z = _sublane_allreduce(z, jnp.add)
    ones = [jnp.where(sl, 1.0, 0.0) for sl in sel]
    cnt = [_sublane_allreduce(ones[0] + ones[1], jnp.add)]
    cnt += [_sublane_allreduce(ones[k1 + 1], jnp.add) for k1 in range(1, 8)]
    cnt += [ones[9][j:j + 1, :] for j in range(8)]
    total = cnt[0]
    for c in cnt[1:]:
        total = total + c
    bad = jnp.where(total != float(PEER_TOPK), 1.0, bad)

    cnt1, r2 = [], []
    weights = (8.0, 4.0, 2.0, 1.0)
    for x in g1:
        bits, below = _rank_bits(x, sv1)
        cnt1.append(jnp.where(below, 0.0, _select16(bits, cnt)))
    for x in g2:
        bits, below = _rank_bits(x, sv2)
        rank = jnp.zeros((8, tb), F32)
        for b, wgt in zip(bits, weights):
            rank = rank + jnp.where(b, wgt, 0.0)
        r2.append(jnp.where(below, UNRANKED, rank))
    return cnt1, r2, sv1[0][0:1, :], sv2[0][0:1, :], z[0:1, :], bad


def _route_exact(s1, s2):
    sv1, r1 = _top16(s1)
    sv2, r2 = _top16(s2)

    vals, pos = _candidate_groups(sv1, sv2)
    orig = list(vals)
    sel = [jnp.zeros(v.shape, F32) for v in vals]
    for _ in range(PEER_TOPK):
        _, idx = _extract_max(vals, pos)
        for i in range(len(vals)):
            hit = pos[i] == idx
            sel[i] = jnp.where(hit, 1.0, sel[i])
            vals[i] = jnp.where(hit, NEG_INF, vals[i])

    cmax = orig[1][0:1, :]
    z = None
    for o, sl in zip(orig, sel):
        part = jnp.sum(jnp.where(sl > 0.0, jnp.exp(o - cmax), 0.0), axis=0, keepdims=True)
        z = part if z is None else z + part
    cnt = [jnp.sum(sel[0] + sel[1], axis=0, keepdims=True)]
    cnt += [jnp.sum(sel[k1 + 1], axis=0, keepdims=True) for k1 in range(1, 8)]
    cnt += [sel[9][j:j + 1, :] for j in range(8)]

    cnt1 = jnp.zeros(r1.shape, F32)
    for k1 in range(PEER_TOPK):
        cnt1 = jnp.where(r1 == float(k1), cnt[k1], cnt1)
    return cnt1, r2, sv1[0:1, :], sv2[0:1, :], z


ROUTE_HEADS_PER_STEP = 4


def _route_kernel(xnt_ref, wqt_ref, k1_ref, k2_ref, cnt1_ref, g1_ref, r2_ref, e2_ref, s_ref):
    qt_all = _dot(wqt_ref[...], xnt_ref[...]).astype(BF16)

    def emit_gates(hh, s1, s2, max1, max2, z):
        g1_ref[hh] = jnp.exp(s1 - max1) * (0.5 / z)
        e2_ref[hh] = jnp.exp(s2 - max2).astype(BF16)

    ties = []
    for hh in range(ROUTE_HEADS_PER_STEP):
        qt = qt_all[hh * 2 * PEER_HALF:(hh + 1) * 2 * PEER_HALF, :]
        s1 = _dot(k1_ref[hh], qt[0:PEER_HALF, :])
        s2 = _dot(k2_ref[hh], qt[PEER_HALF:, :])
        s_ref[hh, 0] = s1
        s_ref[hh, 1] = s2
        cnt1, r2, max1, max2, z, tie = _route_fast(s1, s2)
        for v in range(PEER_NKEYS // 8):
            cnt1_ref[hh, v * 8:(v + 1) * 8, :] = cnt1[v]
            r2_ref[hh, v * 8:(v + 1) * 8, :] = r2[v].astype(BF16)
        emit_gates(hh, s1, s2, max1, max2, z)
        ties.append(jnp.max(tie))

    for hh in range(ROUTE_HEADS_PER_STEP):
        @pl.when(ties[hh] > 0.0)
        def _(hh=hh):
            s1, s2 = s_ref[hh, 0], s_ref[hh, 1]
            cnt1, r2, max1, max2, z = _route_exact(s1, s2)
            cnt1_ref[hh] = cnt1
            r2_ref[hh] = r2.astype(BF16)
            emit_gates(hh, s1, s2, max1, max2, z)


def _route(xnt, wqt, keys):
    d, n = xnt.shape
    tb = _token_block(n, 256)
    hps = ROUTE_HEADS_PER_STEP
    per_head = pl.BlockSpec((hps, PEER_NKEYS, tb), lambda i, h: (h, 0, i))
    shape = lambda dt: jax.ShapeDtypeStruct((PEER_HEADS, PEER_NKEYS, n), dt)
    return pl.pallas_call(
        _route_kernel,
        grid=(n // tb, PEER_HEADS // hps),
        in_specs=[
            pl.BlockSpec((d, tb), lambda i, h: (0, i)),
            pl.BlockSpec((hps * 2 * PEER_HALF, d), lambda i, h: (h, 0)),
            pl.BlockSpec((hps, PEER_NKEYS, PEER_HALF), lambda i, h: (h, 0, 0)),
            pl.BlockSpec((hps, PEER_NKEYS, PEER_HALF), lambda i, h: (h + PEER_HEADS // hps, 0, 0)),
        ],
        out_specs=[per_head] * 4,
        out_shape=[shape(F32), shape(F32), shape(BF16), shape(BF16)],
        scratch_shapes=[pltpu.VMEM((hps, 2, PEER_NKEYS, tb), F32)],
        compiler_params=_params(("parallel", "parallel")),
        name="route",
    )(xnt, wqt, keys, keys)


I1_PER_TILE = 16
BF16_TILE = (8, 2 * LANE)
EXPERT_TILE = I1_PER_TILE * PEER_NKEYS


def _experts_kernel(xnt_ref, cnt1_ref, g1_ref, r2_ref, e2_ref, u_ref, vt_ref, x1_ref,
                    x2_ref, acc_ref, act_ref, wa_ref):
    e = pl.program_id(1)
    tb = xnt_ref.shape[1]

    @pl.when(e == 0)
    def _():
        acc_ref[...] = jnp.zeros_like(acc_ref)

    a = _dot(u_ref[...], xnt_ref[...]).astype(BF16)
    act = a * (1.0 + jnp.tanh(a * (GELU_C0 + GELU_C1 * (a * a))))
    act_ref[...] = act.reshape(act_ref.shape)
    sub, width = BF16_TILE[0], min(BF16_TILE[1], tb)
    assert I1_PER_TILE % sub == 0
    i1_rows = pl.ds(pl.multiple_of(e * I1_PER_TILE, I1_PER_TILE), I1_PER_TILE)
    for lt in range(tb // width):
        lanes = pl.ds(lt * width, width)
        cnt_tile = [cnt1_ref[h, i1_rows, lanes] for h in range(PEER_HEADS)]
        g1_tile = [g1_ref[h, i1_rows, lanes] for h in range(PEER_HEADS)]
        for j in range(I1_PER_TILE):
            cnt_b = [jnp.broadcast_to(t[j:j + 1, :], (sub, width)).astype(BF16) for t in cnt_tile]
            g1_b = [jnp.broadcast_to(t[j:j + 1, :], (sub, width)).astype(BF16) for t in g1_tile]
            for rb in range(PEER_NKEYS // sub):
                krows = pl.ds(rb * sub, sub)
                w = None
                for h in range(PEER_HEADS):
                    mask = r2_ref[h, krows, lanes] < cnt_b[h]
                    term = jnp.where(mask, e2_ref[h, krows, lanes] * g1_b[h], jnp.zeros((), BF16))
                    w = term if w is None else w + term
                grp = j * (PEER_NKEYS // sub) + rb
                wa_ref[grp, :, lanes] = w * act_ref[grp, :, lanes]
    acc_ref[...] += _dot(vt_ref[...], wa_ref[...].reshape(EXPERT_TILE, tb))

    @pl.when(e == pl.num_programs(1) - 1)
    def _():
        x2_ref[...] = x1_ref[...] + acc_ref[...].T


def _experts(xnt, cnt1, g1, r2, e2, u_tab, vt_tab, x1):
    d, n = xnt.shape
    n_exp = u_tab.shape[0]
    tb = _token_block(n, 512)
    assert tb % LANE == 0 and n_exp % EXPERT_TILE == 0
    per_head = pl.BlockSpec((PEER_HEADS, PEER_NKEYS, tb), lambda i, e: (0, 0, i))
    tile_buf = pltpu.VMEM((EXPERT_TILE // BF16_TILE[0], BF16_TILE[0], tb), BF16)
    return pl.pallas_call(
        _experts_kernel,
        grid=(n // tb, n_exp // EXPERT_TILE),
        in_specs=[
            pl.BlockSpec((d, tb), lambda i, e: (0, i)),
            per_head, per_head, per_head, per_head,
            pl.BlockSpec((EXPERT_TILE, d), lambda i, e: (e, 0)),
            pl.BlockSpec((d, EXPERT_TILE), lambda i, e: (0, e)),
            pl.BlockSpec((tb, d), lambda i, e: (i, 0)),
        ],
        out_specs=pl.BlockSpec((tb, d), lambda i, e: (i, 0)),
        out_shape=jax.ShapeDtypeStruct((n, d), F32),
        scratch_shapes=[pltpu.VMEM((d, tb), F32), tile_buf, tile_buf],
        compiler_params=_params(("parallel", "arbitrary")),
        name="experts",
    )(xnt, cnt1, g1, r2, e2, u_tab, vt_tab, x1)


def _ple_kernel(x_ref, p_ref, wg_ref, wp_ref, nf_ref, y_ref, *, final_norm):
    x = x_ref[...]
    gate = _sigmoid(_dot(x.astype(BF16), wg_ref[...]))
    y = x + gate * _dot(p_ref[...].astype(BF16), wp_ref[...])
    if final_norm:
        y = _rmsnorm(y, nf_ref[...])
    y_ref[...] = y


def _ple(x, p, w_gate, w_ple, norm_f, final_norm):
    n, d = x.shape
    tb = _token_block(n, 512)
    rows = lambda i: (i, 0)
    const = lambda i: (0, 0)
    return pl.pallas_call(
        functools.partial(_ple_kernel, final_norm=final_norm),
        grid=(n // tb,),
        in_specs=[
            pl.BlockSpec((tb, d), rows),
            pl.BlockSpec((tb, p.shape[1]), rows),
            pl.BlockSpec(w_gate.shape, const),
            pl.BlockSpec(w_ple.shape, const),
            pl.BlockSpec((1, d), const),
        ],
        out_specs=pl.BlockSpec((tb, d), rows),
        out_shape=jax.ShapeDtypeStruct((n, d), F32),
        compiler_params=_params(("parallel",)),
        name="ple",
    )(x, p, w_gate, w_ple, norm_f)


def _prep_layer_weights(l, w_in, b_gate, mlstm_norm, w_pool, pool_scale, w_out, norm1, norm2,
                        peer_wq, peer_keys, peer_u, peer_v, w_ple, w_gate):
    w = MLSTM_W
    wi = w_in[l]
    d = wi.shape[0]
    wa = jnp.concatenate([wi[:, 0:w], wi[:, 2 * w:4 * w + POOL_W]], axis=1).astype(BF16)
    wkt = wi[:, w:2 * w].T.astype(BF16)
    wgt = wi[:, 4 * w + POOL_W:].T.astype(BF16)
    return dict(
        wa=wa, wkt=wkt, wgt=wgt,
        bg=b_gate[l].reshape(2 * HEADS, 1).astype(F32),
        mlstm_norm=mlstm_norm[l].reshape(1, w),
        w_pool=w_pool[l].astype(BF16),
        pool_scale=pool_scale[l].reshape(1, POOL_W),
        w_out=w_out[l].astype(BF16),
        norm1=norm1[l].reshape(1, d), norm2=norm2[l].reshape(1, d),
        wqt=peer_wq[l].T.astype(BF16),
        keys=peer_keys[l].reshape(2 * PEER_HEADS, PEER_NKEYS, PEER_HALF).astype(BF16),
        u_tab=peer_u[l].astype(BF16),
        vt_tab=peer_v[l].T.astype(BF16),
        w_ple=w_ple[l].astype(BF16), w_gate=w_gate[l].astype(BF16),
    )


def _ffn_and_embed(x1, xnt, p, lw, norm_f, final_norm):
    cnt1, g1, r2, e2 = _route(xnt, lw["wqt"], lw["keys"])
    x2 = _experts(xnt, cnt1, g1, r2, e2, lw["u_tab"], lw["vt_tab"], x1)
    return _ple(x2, p, lw["w_gate"], lw["w_ple"], norm_f, final_norm)


def _prompt_layer(x, p, lw, norm_f, final_norm, batch, t_len):
    q, kt, v, o, u, gt = _proj(x, lw["norm1"], lw["wa"], lw["wkt"], lw["wgt"], lw["bg"])
    gates3 = gt.reshape(2 * HEADS, (batch * t_len) // CHUNK, CHUNK)
    hm, cext, m8 = _mlstm(q, kt, v, o, gates3, lw["mlstm_norm"], batch, t_len)
    zp, buf = _pool(u, lw["w_pool"], lw["pool_scale"], batch, t_len)
    x1, xnt = _mix(x, hm, zp, lw["w_out"], lw["norm2"])
    y = _ffn_and_embed(x1, xnt, p, lw, norm_f, final_norm)
    return y, cext[..., :HEAD_DIM], cext[..., HEAD_DIM], m8[..., 0, 0], buf


def _sample_layer(x, p, c0, n0, m0, buf0, lw, norm_f, final_norm):
    q, kt, v, o, u, gt = _proj(x, lw["norm1"], lw["wa"], lw["wkt"], lw["wgt"], lw["bg"])
    hm, c1, n1, m1 = _mlstm1(q, kt.T, v, o, gt.T, c0, n0.reshape(-1, MLSTM_W), m0, lw["mlstm_norm"])
    n1 = n1.reshape(n0.shape)
    zp, buf_t = _pool1(u, jnp.swapaxes(buf0, 0, 1), lw["w_pool"], lw["pool_scale"])
    x1, xnt = _mix(x, hm, zp, lw["w_out"], lw["norm2"])
    y = _ffn_and_embed(x1, xnt, p, lw, norm_f, final_norm)
    return y, c1, n1, m1, jnp.swapaxes(buf_t, 0, 1)


def kernel(x_prompt, x_sample, p_prompt, p_sample, state_C, state_n, state_m, state_pool,
           w_in, b_gate, mlstm_norm, w_pool, pool_scale, w_out, norm1, norm2,
           peer_wq, peer_keys, peer_u, peer_v, w_ple, w_gate, norm_f):
    depth = w_in.shape[0]
    batch, t_len, d = x_prompt.shape
    dec_batch, dec_len, _ = x_sample.shape
    assert dec_len == 1 and t_len % CHUNK == 0
    nf = norm_f.reshape(1, d)

    xp = x_prompt.reshape(batch * t_len, d)
    xs = x_sample.reshape(dec_batch, d)
    outs_p, outs_s = [], []
    for l in range(depth):
        lw = _prep_layer_weights(l, w_in, b_gate, mlstm_norm, w_pool, pool_scale, w_out, norm1,
                                 norm2, peer_wq, peer_keys, peer_u, peer_v, w_ple, w_gate)
        final = l == depth - 1
        pp = p_prompt[l].reshape(batch * t_len, -1)
        ps = p_sample[l].reshape(dec_batch, -1)
        xp, c_p, n_p, m_p, buf_p = _prompt_layer(xp, pp, lw, nf, final, batch, t_len)
        xs, c_s, n_s, m_s, buf_s = _sample_layer(xs, ps, state_C[l], state_n[l], state_m[l],
                                                 state_pool[l], lw, nf, final)
        outs_p.append((c_p, n_p, m_p, buf_p))
        outs_s.append((c_s, n_s, m_s, buf_s))

    stack = lambda outs, i: jnp.stack([o[i] for o in outs])
    return (xp.reshape(batch, t_len, d), xs.reshape(dec_batch, dec_len, d),
            stack(outs_p, 0), stack(outs_p, 1), stack(outs_p, 2), stack(outs_p, 3),
            stack(outs_s, 0), stack(outs_s, 1), stack(outs_s, 2), stack(outs_s, 3))
```

```python
import functools
import math

import jax
import jax.numpy as jnp
from jax import lax
from jax.experimental import pallas as pl
from jax.experimental.pallas import tpu as pltpu

F32 = jnp.float32
BF16 = jnp.bfloat16
EPS = 1e-6
NEG_INF = float("-inf")

LANE = 128
V7X_VMEM_LIMIT = 56 * 1024 * 1024

HEADS = 4
HEAD_DIM = 128
MLSTM_W = HEADS * HEAD_DIM
POOL_WINDOWS = (2, 4, 8, 16)
POOL_GROUP = 128
POOL_W = POOL_GROUP * len(POOL_WINDOWS)
POOL_BUF = max(POOL_WINDOWS) - 1
CHUNK = 128
PEER_HEADS = 8
PEER_NKEYS = 128
PEER_HALF = 128
PEER_TOPK = 16
UNRANKED = 127.0
GELU_C0 = math.sqrt(2.0 / math.pi)
GELU_C1 = 0.044715 * GELU_C0


def _params(semantics):
    return pltpu.CompilerParams(dimension_semantics=semantics, vmem_limit_bytes=V7X_VMEM_LIMIT)


def _token_block(n, want):
    tb = min(n, want)
    assert n % tb == 0
    return tb


def _rmsnorm(x, g):
    return x * lax.rsqrt(jnp.mean(x * x, axis=-1, keepdims=True) + EPS) * g


def _log_sigmoid(x):
    return jnp.minimum(x, 0.0) - jnp.log(1.0 + jnp.exp(-jnp.abs(x)))


def _sigmoid(x):
    return 1.0 / (1.0 + jnp.exp(-x))


def _dot(a, b):
    return jnp.dot(a, b, preferred_element_type=F32)


def _dot_nt(a, b):
    return lax.dot_general(a, b, (((1,), (1,)), ((), ())), preferred_element_type=F32)


def _proj_kernel(x_ref, g_ref, wa_ref, wkt_ref, wgt_ref, bg_ref,
                 q_ref, kt_ref, v_ref, o_ref, u_ref, gt_ref):
    hn = _rmsnorm(x_ref[...], g_ref[...]).astype(BF16)
    pa = _dot(hn, wa_ref[...])
    w = MLSTM_W
    q_ref[...] = pa[:, 0:w].astype(BF16)
    v_ref[...] = pa[:, w:2 * w].astype(BF16)
    o_ref[...] = _sigmoid(pa[:, 2 * w:3 * w])
    u_ref[...] = pa[:, 3 * w:4 * w]
    kt = _dot_nt(wkt_ref[...], hn) * (HEAD_DIM ** -0.5)
    kt_ref[...] = kt.astype(BF16)
    gt_ref[...] = _dot_nt(wgt_ref[...], hn) + bg_ref[...]


def _proj(x, norm_g, wa, wkt, wgt, bg):
    n, d = x.shape
    tb = _token_block(n, 512)
    grid = (n // tb,)
    const = lambda i: (0, 0)
    return pl.pallas_call(
        _proj_kernel,
        grid=grid,
        in_specs=[
            pl.BlockSpec((tb, d), lambda i: (i, 0)),
            pl.BlockSpec((1, d), const),
            pl.BlockSpec(wa.shape, const),
            pl.BlockSpec(wkt.shape, const),
            pl.BlockSpec(wgt.shape, const),
            pl.BlockSpec(bg.shape, const),
        ],
        out_specs=[
            pl.BlockSpec((tb, MLSTM_W), lambda i: (i, 0)),
            pl.BlockSpec((MLSTM_W, tb), lambda i: (0, i)),
            pl.BlockSpec((tb, MLSTM_W), lambda i: (i, 0)),
            pl.BlockSpec((tb, MLSTM_W), lambda i: (i, 0)),
            pl.BlockSpec((tb, POOL_W), lambda i: (i, 0)),
            pl.BlockSpec((2 * HEADS, tb), lambda i: (0, i)),
        ],
        out_shape=[
            jax.ShapeDtypeStruct((n, MLSTM_W), BF16),
            jax.ShapeDtypeStruct((MLSTM_W, n), BF16),
            jax.ShapeDtypeStruct((n, MLSTM_W), BF16),
            jax.ShapeDtypeStruct((n, MLSTM_W), F32),
            jax.ShapeDtypeStruct((n, POOL_W), F32),
            jax.ShapeDtypeStruct((2 * HEADS, n), F32),
        ],
        compiler_params=_params(("parallel",)),
        name="proj",
    )(x, norm_g, wa, wkt, wgt, bg)


def _lane_cumsum(x):
    lane = lax.broadcasted_iota(jnp.int32, x.shape, 1)
    shift = 1
    while shift < x.shape[1]:
        x = x + jnp.where(lane >= shift, pltpu.roll(x, shift, axis=1), 0.0)
        shift *= 2
    return x


def _mlstm_kernel(q_ref, kt_ref, v_ref, o_ref, ig_ref, lf_ref, nw_ref,
                  hm_ref, cext_out_ref, m_out_ref, cext_ref):
    t_len = q_ref.shape[0]
    n_chunks = t_len // CHUNK
    ig2 = ig_ref[0]
    lf2 = _log_sigmoid(lf_ref[0])
    b2 = _lane_cumsum(lf2)
    a2 = ig2 - b2

    t_idx = lax.broadcasted_iota(jnp.int32, (CHUNK, CHUNK), 0)
    s_idx = lax.broadcasted_iota(jnp.int32, (CHUNK, CHUNK), 1)
    causal = s_idx <= t_idx
    ones_col = (lax.broadcasted_iota(jnp.int32, (CHUNK, HEAD_DIM), 1) == 0).astype(BF16)
    nw = nw_ref[...]

    cext_ref[...] = jnp.zeros_like(cext_ref)
    m = jnp.zeros((1, 1), F32)
    for c in range(n_chunks):
        rows = pl.ds(c * CHUNK, CHUNK)
        a_row = a2[c:c + 1, :]
        lf_row = lf2[c:c + 1, :]
        b_last = b2[c:c + 1, CHUNK - 1:CHUNK]
        b_col = jnp.sum(jnp.where(causal, lf_row, 0.0), axis=1, keepdims=True)
        amax_col = jnp.max(jnp.where(causal, a_row, NEG_INF), axis=1, keepdims=True)
        m_col = jnp.maximum(amax_col, m)
        decay_mat = jnp.where(causal, jnp.exp(a_row - m_col), 0.0)

        qc = q_ref[rows, :]
        ktc = kt_ref[:, rows]
        v_ext = jnp.concatenate([v_ref[rows, :], ones_col], axis=1)
        s = _dot(qc, ktc) * decay_mat
        intra = _dot(s.astype(BF16), v_ext)
        inter = _dot(qc, cext_ref[...].astype(BF16))
        tot = intra + jnp.exp(m - m_col) * inter
        num = tot[:, :HEAD_DIM]
        den = tot[:, HEAD_DIM:HEAD_DIM + 1]
        h = num / jnp.maximum(jnp.abs(den), jnp.exp(-(b_col + m_col)))
        h = h * lax.rsqrt(jnp.mean(h * h, axis=1, keepdims=True) + EPS)
        hm_ref[rows, :] = (h * nw * o_ref[rows, :]).astype(BF16)

        m_new = jnp.maximum(b_last + m, jnp.max(a_row, axis=1, keepdims=True) + b_last)
        ws_row = jnp.exp(a_row + (b_last - m_new))
        kws = (ktc.astype(F32) * ws_row).astype(BF16)
        cext_ref[...] = jnp.exp(b_last + m - m_new) * cext_ref[...] + _dot(kws, v_ext)
        m = m_new

    cext_out_ref[0, 0] = cext_ref[...]
    m_out_ref[0, 0] = jnp.broadcast_to(m, m_out_ref.shape[2:])


def _mlstm(q, kt, v, o, gates3, norm_w, batch, t_len):
    n = q.shape[0]
    n_chunks = t_len // CHUNK
    seq = lambda b, h: (b, h)
    return pl.pallas_call(
        _mlstm_kernel,
        grid=(batch, HEADS),
        in_specs=[
            pl.BlockSpec((t_len, HEAD_DIM), seq),
            pl.BlockSpec((HEAD_DIM, t_len), lambda b, h: (h, b)),
            pl.BlockSpec((t_len, HEAD_DIM), seq),
            pl.BlockSpec((t_len, HEAD_DIM), seq),
            pl.BlockSpec((1, n_chunks, CHUNK), lambda b, h: (h, b, 0)),
            pl.BlockSpec((1, n_chunks, CHUNK), lambda b, h: (h + HEADS, b, 0)),
            pl.BlockSpec((1, HEAD_DIM), lambda b, h: (0, h)),
        ],
        out_specs=[
            pl.BlockSpec((t_len, HEAD_DIM), seq),
            pl.BlockSpec((1, 1, HEAD_DIM, 2 * HEAD_DIM), lambda b, h: (b, h, 0, 0)),
            pl.BlockSpec((1, 1, 8, LANE), lambda b, h: (b, h, 0, 0)),
        ],
        out_shape=[
            jax.ShapeDtypeStruct((n, MLSTM_W), BF16),
            jax.ShapeDtypeStruct((batch, HEADS, HEAD_DIM, 2 * HEAD_DIM), F32),
            jax.ShapeDtypeStruct((batch, HEADS, 8, LANE), F32),
        ],
        scratch_shapes=[pltpu.VMEM((HEAD_DIM, 2 * HEAD_DIM), F32)],
        compiler_params=_params(("parallel", "parallel")),
        name="mlstm",
    )(q, kt, v, o, gates3, gates3, norm_w)


def _mlstm1_kernel(q_ref, k_ref, v_ref, o_ref, g_ref, c_ref, n_ref, m_ref, nw_ref,
                   hm_ref, c_out_ref, n_out_ref, m_out_ref):
    bb = q_ref.shape[0]
    row = lax.broadcasted_iota(jnp.int32, (bb, HEAD_DIM), 0)
    eye = (lax.broadcasted_iota(jnp.int32, (HEAD_DIM, HEAD_DIM), 0)
           == lax.broadcasted_iota(jnp.int32, (HEAD_DIM, HEAD_DIM), 1))
    lane_h = lax.broadcasted_iota(jnp.int32, (bb, HEADS), 1)
    g = g_ref[...]
    m_all = m_ref[...]
    m_out = jnp.zeros((bb, HEADS), F32)
    for h in range(HEADS):
        cols = slice(h * HEAD_DIM, (h + 1) * HEAD_DIM)
        qb = q_ref[:, cols]
        qf = qb.astype(F32)
        kf = k_ref[:, cols].astype(F32)
        vf = v_ref[:, cols].astype(F32)
        ig = g[:, h:h + 1]
        lf = _log_sigmoid(g[:, HEADS + h:HEADS + h + 1])
        m_old = m_all[:, h:h + 1]
        n_old = n_ref[:, cols]
        m_new = jnp.maximum(ig, lf + m_old)
        w_in = jnp.exp(ig - m_new)
        w_st = jnp.exp(lf + m_old - m_new)
        qc = jnp.zeros((bb, HEAD_DIM), F32)
        for j in range(bb):
            c_old = c_ref[j, h]
            res = _dot(qb, c_old.astype(BF16))
            qc = jnp.where(row == j, res, qc)
            k_col = jnp.sum(jnp.where(eye, kf[j:j + 1, :], 0.0), axis=1, keepdims=True)
            c_out_ref[j, h] = w_st[j:j + 1, :] * c_old + k_col * (w_in[j:j + 1, :] * vf[j:j + 1, :])
        s = jnp.sum(qf * kf, axis=1, keepdims=True) * w_in
        num = s * vf + w_st * qc
        den = s + w_st * jnp.sum(qf * n_old, axis=1, keepdims=True)
        hh = num / jnp.maximum(jnp.abs(den), jnp.exp(-m_new))
        hh = hh * lax.rsqrt(jnp.mean(hh * hh, axis=1, keepdims=True) + EPS)
        hm_ref[:, cols] = (hh * nw_ref[:, cols] * o_ref[:, cols]).astype(BF16)
        n_out_ref[:, cols] = w_st * n_old + w_in * kf
        m_out = jnp.where(lane_h == h, m_new, m_out)
    m_out_ref[...] = m_out


def _mlstm1(q, k, v, o, g, c0, n0, m0, norm_w):
    batch = q.shape[0]
    bb = _token_block(batch, 16)
    rows = lambda i: (i, 0)
    return pl.pallas_call(
        _mlstm1_kernel,
        grid=(batch // bb,),
        in_specs=[
            pl.BlockSpec((bb, MLSTM_W), rows),
            pl.BlockSpec((bb, MLSTM_W), rows),
            pl.BlockSpec((bb, MLSTM_W), rows),
            pl.BlockSpec((bb, MLSTM_W), rows),
            pl.BlockSpec((bb, 2 * HEADS), rows),
            pl.BlockSpec((bb, HEADS, HEAD_DIM, HEAD_DIM), lambda i: (i, 0, 0, 0)),
            pl.BlockSpec((bb, MLSTM_W), rows),
            pl.BlockSpec((bb, HEADS), rows),
            pl.BlockSpec((1, MLSTM_W), lambda i: (0, 0)),
        ],
        out_specs=[
            pl.BlockSpec((bb, MLSTM_W), rows),
            pl.BlockSpec((bb, HEADS, HEAD_DIM, HEAD_DIM), lambda i: (i, 0, 0, 0)),
            pl.BlockSpec((bb, MLSTM_W), rows),
            pl.BlockSpec((bb, HEADS), rows),
        ],
        out_shape=[
            jax.ShapeDtypeStruct((batch, MLSTM_W), BF16),
            jax.ShapeDtypeStruct(c0.shape, F32),
            jax.ShapeDtypeStruct(n0.shape, F32),
            jax.ShapeDtypeStruct(m0.shape, F32),
        ],
        compiler_params=_params(("parallel",)),
        name="mlstm1",
    )(q, k, v, o, g, c0, n0, m0, norm_w)


def _pool_kernel(u_ref, wp_ref, sc_ref, z_ref, buf_ref):
    t_len = u_ref.shape[0]
    t_idx = lax.broadcasted_iota(jnp.int32, (t_len, POOL_GROUP), 0)
    for g, w in enumerate(POOL_WINDOWS):
        cols = slice(g * POOL_GROUP, (g + 1) * POOL_GROUP)
        x = u_ref[:, cols]
        s = x
        k = 1
        while k < w:
            s = s + jnp.where(t_idx >= k, pltpu.roll(s, k, axis=0), 0.0)
            k *= 2
        cnt = jnp.minimum(t_idx + 1, w).astype(F32)
        r = s / cnt - x
        z = _dot(r.astype(BF16), wp_ref[g]) * sc_ref[:, cols]
        z_ref[:, cols] = z.astype(BF16)
    buf_ref[0] = u_ref[t_len - POOL_BUF:t_len, :]


def _pool(u, w_pool, scale, batch, t_len):
    n = u.shape[0]
    return pl.pallas_call(
        _pool_kernel,
        grid=(batch,),
        in_specs=[
            pl.BlockSpec((t_len, POOL_W), lambda b: (b, 0)),
            pl.BlockSpec(w_pool.shape, lambda b: (0, 0, 0)),
            pl.BlockSpec((1, POOL_W), lambda b: (0, 0)),
        ],
        out_specs=[
            pl.BlockSpec((t_len, POOL_W), lambda b: (b, 0)),
            pl.BlockSpec((1, POOL_BUF, POOL_W), lambda b: (b, 0, 0)),
        ],
        out_shape=[
            jax.ShapeDtypeStruct((n, POOL_W), BF16),
            jax.ShapeDtypeStruct((batch, POOL_BUF, POOL_W), F32),
        ],
        compiler_params=_params(("parallel",)),
        name="pool",
    )(u, w_pool, scale)


def _pool1_kernel(u_ref, buft_ref, wp_ref, sc_ref, z_ref, buft_out_ref):
    for g, w in enumerate(POOL_WINDOWS):
        cols = slice(g * POOL_GROUP, (g + 1) * POOL_GROUP)
        x = u_ref[:, cols]
        s = x
        for j in range(1, w):
            s = s + buft_ref[POOL_BUF - j, :, cols]
        r = s / float(w) - x
        z = _dot(r.astype(BF16), wp_ref[g]) * sc_ref[:, cols]
        z_ref[:, cols] = z.astype(BF16)
    for j in range(POOL_BUF - 1):
        buft_out_ref[j] = buft_ref[j + 1]
    buft_out_ref[POOL_BUF - 1] = u_ref[...]


def _pool1(u, buf_t, w_pool, scale):
    batch = u.shape[0]
    full2 = lambda i: (0, 0)
    full3 = lambda i: (0, 0, 0)
    return pl.pallas_call(
        _pool1_kernel,
        grid=(1,),
        in_specs=[
            pl.BlockSpec(u.shape, full2),
            pl.BlockSpec(buf_t.shape, full3),
            pl.BlockSpec(w_pool.shape, full3),
            pl.BlockSpec((1, POOL_W), full2),
        ],
        out_specs=[
            pl.BlockSpec((batch, POOL_W), full2),
            pl.BlockSpec(buf_t.shape, full3),
        ],
        out_shape=[
            jax.ShapeDtypeStruct((batch, POOL_W), BF16),
            jax.ShapeDtypeStruct(buf_t.shape, F32),
        ],
        compiler_params=_params(("arbitrary",)),
        name="pool1",
    )(u, buf_t, w_pool, scale)


def _mix_kernel(x_ref, hm_ref, zp_ref, wo_ref, g_ref, x1_ref, xnt_ref):
    mix = _dot(hm_ref[...], wo_ref[0:MLSTM_W, :]) + _dot(zp_ref[...], wo_ref[MLSTM_W:, :])
    x1 = x_ref[...] + mix
    x1_ref[...] = x1
    xnt_ref[...] = _rmsnorm(x1, g_ref[...]).T.astype(BF16)


def _mix(x, hm, zp, w_out, norm_g):
    n, d = x.shape
    tb = _token_block(n, 512)
    rows = lambda i: (i, 0)
    const = lambda i: (0, 0)
    return pl.pallas_call(
        _mix_kernel,
        grid=(n // tb,),
        in_specs=[
            pl.BlockSpec((tb, d), rows),
            pl.BlockSpec((tb, MLSTM_W), rows),
            pl.BlockSpec((tb, POOL_W), rows),
            pl.BlockSpec(w_out.shape, const),
            pl.BlockSpec((1, d), const),
        ],
        out_specs=[
            pl.BlockSpec((tb, d), rows),
            pl.BlockSpec((d, tb), lambda i: (0, i)),
        ],
        out_shape=[
            jax.ShapeDtypeStruct((n, d), F32),
            jax.ShapeDtypeStruct((d, n), BF16),
        ],
        compiler_params=_params(("parallel",)),
        name="mix",
    )(x, hm, zp, w_out, norm_g)


def _extract_max(vals, pos):
    m = vals[0]
    for v in vals[1:]:
        m = jnp.maximum(m, v)
    m = jnp.max(m, axis=0, keepdims=True)
    big = jnp.float32(1e9)
    idx = None
    for v, p in zip(vals, pos):
        cand = jnp.where(v == m, p, big)
        idx = cand if idx is None else jnp.minimum(idx, cand)
    idx = jnp.min(idx, axis=0, keepdims=True)
    return m, idx


def _top16(s):
    tb = s.shape[1]
    row = lax.broadcasted_iota(jnp.int32, s.shape, 0).astype(F32)
    krow = lax.broadcasted_iota(jnp.int32, (PEER_TOPK, tb), 0)
    rank = jnp.full(s.shape, UNRANKED, F32)
    sv = jnp.zeros((PEER_TOPK, tb), F32)
    for k in range(PEER_TOPK):
        m, idx = _extract_max([s], [row])
        hit = row == idx
        rank = jnp.where(hit, float(k), rank)
        s = jnp.where(hit, NEG_INF, s)
        sv = jnp.where(krow == k, m, sv)
    return sv, rank


def _candidate_groups(sv1, sv2):
    tb = sv1.shape[1]
    vals, pos = [], []
    r8 = lax.broadcasted_iota(jnp.int32, (8, tb), 0)
    vals.append(sv1[0:1, :] + sv2[8:16, :])
    pos.append((r8 + 8).astype(F32))
    vals.append(sv1[0:1, :] + sv2[0:8, :])
    pos.append(r8.astype(F32))
    for k1 in range(1, 8):
        lim = PEER_TOPK // (k1 + 1)
        v = sv1[k1:k1 + 1, :] + sv2[0:8, :]
        vals.append(jnp.where(r8 < lim, v, NEG_INF))
        pos.append((r8 + k1 * PEER_TOPK).astype(F32))
    vals.append(sv1[8:16, :] + sv2[0:1, :])
    pos.append(((r8 + 8) * PEER_TOPK).astype(F32))
    return vals, pos


def _sorting_pairs(n):
    pairs = []
    p = 1
    while p < n:
        k = p
        while k >= 1:
            for j in range(k % p, n - k, 2 * k):
                for i in range(min(k, n - j - k)):
                    if (i + j) // (2 * p) == (i + j + k) // (2 * p):
                        pairs.append((i + j, i + j + k))
            k //= 2
        p *= 2
    return pairs


def _sublane_allreduce(x, op):
    for d in (1, 2, 4):
        x = op(x, pltpu.roll(x, d, axis=0))
    return x


def _sorted_top16(groups):
    g = list(groups)
    for i, j in _sorting_pairs(len(g)):
        g[i], g[j] = jnp.maximum(g[i], g[j]), jnp.minimum(g[i], g[j])
    n = len(g)
    for d in (1, 2, 4):
        p = [pltpu.roll(x, d, axis=0) for x in g]
        g = [jnp.maximum(g[i], p[n - 1 - i]) for i in range(n)]
        stride = n // 2
        while stride >= 1:
            for i in range(n):
                if i & stride == 0:
                    g[i], g[i + stride] = (jnp.maximum(g[i], g[i + stride]),
                                           jnp.minimum(g[i], g[i + stride]))
            stride //= 2
    return g


def _rank_bits(x, sv):
    c8 = x < sv[7]
    t = jnp.where(c8, sv[11], sv[3])
    c4 = x < t
    t = jnp.where(c8, jnp.where(c4, sv[13], sv[9]), jnp.where(c4, sv[5], sv[1]))
    c2 = x < t
    t = jnp.where(c8,
                  jnp.where(c4, jnp.where(c2, sv[14], sv[12]), jnp.where(c2, sv[10], sv[8])),
                  jnp.where(c4, jnp.where(c2, sv[6], sv[4]), jnp.where(c2, sv[2], sv[0])))
    c1 = x < t
    return (c8, c4, c2, c1), x < sv[15]


def _select16(bits, rows):
    c8, c4, c2, c1 = bits
    lvl = [jnp.where(c1, rows[2 * i + 1], rows[2 * i]) for i in range(8)]
    lvl = [jnp.where(c2, lvl[2 * i + 1], lvl[2 * i]) for i in range(4)]
    lvl = [jnp.where(c4, lvl[2 * i + 1], lvl[2 * i]) for i in range(2)]
    return jnp.where(c8, lvl[1], lvl[0])


def _route_fast(s1, s2):
    tb = s1.shape[1]
    r8 = lax.broadcasted_iota(jnp.int32, (8, tb), 0)
    n_grp = PEER_NKEYS // 8
    g1 = [s1[v * 8:(v + 1) * 8, :] for v in range(n_grp)]
    g2 = [s2[v * 8:(v + 1) * 8, :] for v in range(n_grp)]
    sv1 = _sorted_top16(g1)
    sv2 = _sorted_top16(g2)

    bad = jnp.zeros((8, tb), F32)
    for sv, grp in ((sv1, g1), (sv2, g2)):
        for k in range(PEER_TOPK - 1):
            bad = jnp.where(sv[k] == sv[k + 1], 1.0, bad)
        n_in = jnp.zeros((8, tb), F32)
        for x in grp:
            n_in = n_in + jnp.where(x >= sv[PEER_TOPK - 1], 1.0, 0.0)
        n_in = _sublane_allreduce(n_in, jnp.add)
        bad = jnp.where(n_in != float(PEER_TOPK), 1.0, bad)

    def by_sublane(rows):
        out = jnp.zeros((8, tb), F32)
        for k, row in enumerate(rows):
            out = jnp.where(r8 == k, row, out)
        return out
    a2_lo, a2_hi, a1_hi = by_sublane(sv2[0:8]), by_sublane(sv2[8:16]), by_sublane(sv1[8:16])
    orig = [sv1[0] + a2_hi, sv1[0] + a2_lo]
    for k1 in range(1, 8):
        orig.append(jnp.where(r8 < PEER_TOPK // (k1 + 1), sv1[k1] + a2_lo, NEG_INF))
    orig.append(a1_hi + sv2[0])
    vals = list(orig)
    prev = None
    for _ in range(PEER_TOPK):
        m = vals[0]
        for v in vals[1:]:
            m = jnp.maximum(m, v)
        m = _sublane_allreduce(m, jnp.maximum)
        vals = [jnp.where(v == m, NEG_INF, v) for v in vals]
        if prev is not None:
            bad = jnp.where(m == prev, 1.0, bad)
        prev = m
    sel = [o >= prev for o in orig]
    cmax = sv1[0] + sv2[0]
    z = jnp.zeros((8, tb), F32)
    for o, sl in zip(orig, sel):
        z = z + jnp.where(sl, jnp.exp(o - cmax), 0.0)
    z = _sublane_allreduce(z, jnp.add)
    ones = [jnp.where(sl, 1.0, 0.0) for sl in sel]
    cnt = [_sublane_allreduce(ones[0] + ones[1], jnp.add)]
    cnt += [_sublane_allreduce(ones[k1 + 1], jnp.add) for k1 in range(1, 8)]
    cnt += [ones[9][j:j + 1, :] for j in range(8)]
    total = cnt[0]
    for c in cnt[1:]:
        total = total + c
    bad = jnp.where(total != float(PEER_TOPK), 1.0, bad)

    cnt1, r2 = [], []
    weights = (8.0, 4.0, 2.0, 1.0)
    for x in g1:
        bits, below = _rank_bits(x, sv1)
        cnt1.append(jnp.where(below, 0.0, _select16(bits, cnt)))
    for x in g2:
        bits, below = _rank_bits(x, sv2)
        rank = jnp.zeros((8, tb), F32)
        for b, wgt in zip(bits, weights):
            rank = rank + jnp.where(b, wgt, 0.0)
        r2.append(jnp.where(below, UNRANKED, rank))
    return cnt1, r2, sv1[0][0:1, :], sv2[0][0:1, :], z[0:1, :], bad


def _route_exact(s1, s2):
    sv1, r1 = _top16(s1)
    sv2, r2 = _top16(s2)

    vals, pos = _candidate_groups(sv1, sv2)
    orig = list(vals)
    sel = [jnp.zeros(v.shape, F32) for v in vals]
    for _ in range(PEER_TOPK):
        _, idx = _extract_max(vals, pos)
        for i in range(len(vals)):
            hit = pos[i] == idx
            sel[i] = jnp.where(hit, 1.0, sel[i])
            vals[i] = jnp.where(hit, NEG_INF, vals[i])

    cmax = orig[1][0:1, :]
    z = None
    for o, sl in zip(orig, sel):
        part = jnp.sum(jnp.where(sl > 0.0, jnp.exp(o - cmax), 0.0), axis=0, keepdims=True)
        z = part if z is None else z + part
    cnt = [jnp.sum(sel[0] + sel[1], axis=0, keepdims=True)]
    cnt += [jnp.sum(sel[k1 + 1], axis=0, keepdims=True) for k1 in range(1, 8)]
    cnt += [sel[9][j:j + 1, :] for j in range(8)]

    cnt1 = jnp.zeros(r1.shape, F32)
    for k1 in range(PEER_TOPK):
        cnt1 = jnp.where(r1 == float(k1), cnt[k1], cnt1)
    return cnt1, r2, sv1[0:1, :], sv2[0:1, :], z


ROUTE_HEADS_PER_STEP = 4


def _route_kernel(xnt_ref, wqt_ref, k1_ref, k2_ref, cnt1_ref, g1_ref, r2_ref, e2_ref, s_ref):
    qt_all = _dot(wqt_ref[...], xnt_ref[...]).astype(BF16)

    def emit_gates(hh, s1, s2, max1, max2, z):
        g1_ref[hh] = jnp.exp(s1 - max1) * (0.5 / z)
        e2_ref[hh] = jnp.exp(s2 - max2).astype(BF16)

    ties = []
    for hh in range(ROUTE_HEADS_PER_STEP):
        qt = qt_all[hh * 2 * PEER_HALF:(hh + 1) * 2 * PEER_HALF, :]
        s1 = _dot(k1_ref[hh], qt[0:PEER_HALF, :])
        s2 = _dot(k2_ref[hh], qt[PEER_HALF:, :])
        s_ref[hh, 0] = s1
        s_ref[hh, 1] = s2
        cnt1, r2, max1, max2, z, tie = _route_fast(s1, s2)
        for v in range(PEER_NKEYS // 8):
            cnt1_ref[hh, v * 8:(v + 1) * 8, :] = cnt1[v]
            r2_ref[hh, v * 8:(v + 1) * 8, :] = r2[v].astype(BF16)
        emit_gates(hh, s1, s2, max1, max2, z)
        ties.append(jnp.max(tie))

    for hh in range(ROUTE_HEADS_PER_STEP):
        @pl.when(ties[hh] > 0.0)
        def _(hh=hh):
            s1, s2 = s_ref[hh, 0], s_ref[hh, 1]
            cnt1, r2, max1, max2, z = _route_exact(s1, s2)
            cnt1_ref[hh] = cnt1
            r2_ref[hh] = r2.astype(BF16)
            emit_gates(hh, s1, s2, max1, max2, z)


def _route(xnt, wqt, keys):
    d, n = xnt.shape
    tb = _token_block(n, 256)
    hps = ROUTE_HEADS_PER_STEP
    per_head = pl.BlockSpec((hps, PEER_NKEYS, tb), lambda i, h: (h, 0, i))
    shape = lambda dt: jax.ShapeDtypeStruct((PEER_HEADS, PEER_NKEYS, n), dt)
    return pl.pallas_call(
        _route_kernel,
        grid=(n // tb, PEER_HEADS // hps),
        in_specs=[
            pl.BlockSpec((d, tb), lambda i, h: (0, i)),
            pl.BlockSpec((hps * 2 * PEER_HALF, d), lambda i, h: (h, 0)),
            pl.BlockSpec((hps, PEER_NKEYS, PEER_HALF), lambda i, h: (h, 0, 0)),
            pl.BlockSpec((hps, PEER_NKEYS, PEER_HALF), lambda i, h: (h + PEER_HEADS // hps, 0, 0)),
        ],
        out_specs=[per_head] * 4,
        out_shape=[shape(F32), shape(F32), shape(BF16), shape(BF16)],
        scratch_shapes=[pltpu.VMEM((hps, 2, PEER_NKEYS, tb), F32)],
        compiler_params=_params(("parallel", "parallel")),
        name="route",
    )(xnt, wqt, keys, keys)


I1_PER_TILE = 16
BF16_TILE = (8, 2 * LANE)
EXPERT_TILE = I1_PER_TILE * PEER_NKEYS


def _experts_kernel(xnt_ref, cnt1_ref, g1_ref, r2_ref, e2_ref, u_ref, vt_ref, x1_ref,
                    x2_ref, acc_ref, act_ref, wa_ref):
    e = pl.program_id(1)
    tb = xnt_ref.shape[1]

    @pl.when(e == 0)
    def _():
        acc_ref[...] = jnp.zeros_like(acc_ref)

    a = _dot(u_ref[...], xnt_ref[...]).astype(BF16)
    act_ref[...] = a * (1.0 + jnp.tanh(a * (GELU_C0 + GELU_C1 * (a * a))))
    sub, width = BF16_TILE[0], min(BF16_TILE[1], tb)
    assert I1_PER_TILE % sub == 0
    i1_rows = pl.ds(pl.multiple_of(e * I1_PER_TILE, I1_PER_TILE), I1_PER_TILE)
    for lt in range(tb // width):
        lanes = pl.ds(lt * width, width)
        cnt_tile = [cnt1_ref[h, i1_rows, lanes] for h in range(PEER_HEADS)]
        g1_tile = [g1_ref[h, i1_rows, lanes] for h in range(PEER_HEADS)]
        for j in range(I1_PER_TILE):
            cnt_b = [jnp.broadcast_to(t[j:j + 1, :], (sub, width)).astype(BF16) for t in cnt_tile]
            g1_b = [jnp.broadcast_to(t[j:j + 1, :], (sub, width)).astype(BF16) for t in g1_tile]
            for rb in range(PEER_NKEYS // sub):
                krows = pl.ds(rb * sub, sub)
                w = None
                for h in range(PEER_HEADS):
                    mask = r2_ref[h, krows, lanes] < cnt_b[h]
                    term = jnp.where(mask, e2_ref[h, krows, lanes] * g1_b[h], jnp.zeros((), BF16))
                    w = term if w is None else w + term
                erows = pl.ds(j * PEER_NKEYS + rb * sub, sub)
                wa_ref[erows, lanes] = w * act_ref[erows, lanes]
    acc_ref[...] += _dot(vt_ref[...], wa_ref[...])

    @pl.when(e == pl.num_programs(1) - 1)
    def _():
        x2_ref[...] = x1_ref[...] + acc_ref[...].T


def _experts(xnt, cnt1, g1, r2, e2, u_tab, vt_tab, x1):
    d, n = xnt.shape
    n_exp = u_tab.shape[0]
    tb = _token_block(n, 512)
    assert tb % LANE == 0 and n_exp % EXPERT_TILE == 0
    per_head = pl.BlockSpec((PEER_HEADS, PEER_NKEYS, tb), lambda i, e: (0, 0, i))
    tile_buf = pltpu.VMEM((EXPERT_TILE, tb), BF16)
    return pl.pallas_call(
        _experts_kernel,
        grid=(n // tb, n_exp // EXPERT_TILE),
        in_specs=[
            pl.BlockSpec((d, tb), lambda i, e: (0, i)),
            per_head, per_head, per_head, per_head,
            pl.BlockSpec((EXPERT_TILE, d), lambda i, e: (e, 0)),
            pl.BlockSpec((d, EXPERT_TILE), lambda i, e: (0, e)),
            pl.BlockSpec((tb, d), lambda i, e: (i, 0)),
        ],
        out_specs=pl.BlockSpec((tb, d), lambda i, e: (i, 0)),
        out_shape=jax.ShapeDtypeStruct((n, d), F32),
        scratch_shapes=[pltpu.VMEM((d, tb), F32), tile_buf, tile_buf],
        compiler_params=_params(("parallel", "arbitrary")),
        name="experts",
    )(xnt, cnt1, g1, r2, e2, u_tab, vt_tab, x1)


def _ple_kernel(x_ref, p_ref, wg_ref, wp_ref, nf_ref, y_ref, *, final_norm):
    x = x_ref[...]
    gate = _sigmoid(_dot(x.astype(BF16), wg_ref[...]))
    y = x + gate * _dot(p_ref[...].astype(BF16), wp_ref[...])
    if final_norm:
        y = _rmsnorm(y, nf_ref[...])
    y_ref[...] = y


def _ple(x, p, w_gate, w_ple, norm_f, final_norm):
    n, d = x.shape
    tb = _token_block(n, 512)
    rows = lambda i: (i, 0)
    const = lambda i: (0, 0)
    return pl.pallas_call(
        functools.partial(_ple_kernel, final_norm=final_norm),
        grid=(n // tb,),
        in_specs=[
            pl.BlockSpec((tb, d), rows),
            pl.BlockSpec((tb, p.shape[1]), rows),
            pl.BlockSpec(w_gate.shape, const),
            pl.BlockSpec(w_ple.shape, const),
            pl.BlockSpec((1, d), const),
        ],
        out_specs=pl.BlockSpec((tb, d), rows),
        out_shape=jax.ShapeDtypeStruct((n, d), F32),
        compiler_params=_params(("parallel",)),
        name="ple",
    )(x, p, w_gate, w_ple, norm_f)


def _prep_layer_weights(l, w_in, b_gate, mlstm_norm, w_pool, pool_scale, w_out, norm1, norm2,
                        peer_wq, peer_keys, peer_u, peer_v, w_ple, w_gate):
    w = MLSTM_W
    wi = w_in[l]
    d = wi.shape[0]
    wa = jnp.concatenate([wi[:, 0:w], wi[:, 2 * w:4 * w + POOL_W]], axis=1).astype(BF16)
    wkt = wi[:, w:2 * w].T.astype(BF16)
    wgt = wi[:, 4 * w + POOL_W:].T.astype(BF16)
    return dict(
        wa=wa, wkt=wkt, wgt=wgt,
        bg=b_gate[l].reshape(2 * HEADS, 1).astype(F32),
        mlstm_norm=mlstm_norm[l].reshape(1, w),
        w_pool=w_pool[l].astype(BF16),
        pool_scale=pool_scale[l].reshape(1, POOL_W),
        w_out=w_out[l].astype(BF16),
        norm1=norm1[l].reshape(1, d), norm2=norm2[l].reshape(1, d),
        wqt=peer_wq[l].T.astype(BF16),
        keys=peer_keys[l].reshape(2 * PEER_HEADS, PEER_NKEYS, PEER_HALF).astype(BF16),
        u_tab=peer_u[l].astype(BF16),
        vt_tab=peer_v[l].T.astype(BF16),
        w_ple=w_ple[l].astype(BF16), w_gate=w_gate[l].astype(BF16),
    )


def _ffn_and_embed(x1, xnt, p, lw, norm_f, final_norm):
    cnt1, g1, r2, e2 = _route(xnt, lw["wqt"], lw["keys"])
    x2 = _experts(xnt, cnt1, g1, r2, e2, lw["u_tab"], lw["vt_tab"], x1)
    return _ple(x2, p, lw["w_gate"], lw["w_ple"], norm_f, final_norm)


def _prompt_layer(x, p, lw, norm_f, final_norm, batch, t_len):
    q, kt, v, o, u, gt = _proj(x, lw["norm1"], lw["wa"], lw["wkt"], lw["wgt"], lw["bg"])
    gates3 = gt.reshape(2 * HEADS, (batch * t_len) // CHUNK, CHUNK)
    hm, cext, m8 = _mlstm(q, kt, v, o, gates3, lw["mlstm_norm"], batch, t_len)
    zp, buf = _pool(u, lw["w_pool"], lw["pool_scale"], batch, t_len)
    x1, xnt = _mix(x, hm, zp, lw["w_out"], lw["norm2"])
    y = _ffn_and_embed(x1, xnt, p, lw, norm_f, final_norm)
    return y, cext[..., :HEAD_DIM], cext[..., HEAD_DIM], m8[..., 0, 0], buf


def _sample_layer(x, p, c0, n0, m0, buf0, lw, norm_f, final_norm):
    q, kt, v, o, u, gt = _proj(x, lw["norm1"], lw["wa"], lw["wkt"], lw["wgt"], lw["bg"])
    hm, c1, n1, m1 = _mlstm1(q, kt.T, v, o, gt.T, c0, n0.reshape(-1, MLSTM_W), m0, lw["mlstm_norm"])
    n1 = n1.reshape(n0.shape)
    zp, buf_t = _pool1(u, jnp.swapaxes(buf0, 0, 1), lw["w_pool"], lw["pool_scale"])
    x1, xnt = _mix(x, hm, zp, lw["w_out"], lw["norm2"])
    y = _ffn_and_embed(x1, xnt, p, lw, norm_f, final_norm)
    return y, c1, n1, m1, jnp.swapaxes(buf_t, 0, 1)


def kernel(x_prompt, x_sample, p_prompt, p_sample, state_C, state_n, state_m, state_pool,
           w_in, b_gate, mlstm_norm, w_pool, pool_scale, w_out, norm1, norm2,
           peer_wq, peer_keys, peer_u, peer_v, w_ple, w_gate, norm_f):
    depth = w_in.shape[0]
    batch, t_len, d = x_prompt.shape
    dec_batch, dec_len, _ = x_sample.shape
    assert dec_len == 1 and t_len % CHUNK == 0
    nf = norm_f.reshape(1, d)

    xp = x_prompt.reshape(batch * t_len, d)
    xs = x_sample.reshape(dec_batch, d)
    outs_p, outs_s = [], []
    for l in range(depth):
        lw = _prep_layer_weights(l, w_in, b_gate, mlstm_norm, w_pool, pool_scale, w_out, norm1,
                                 norm2, peer_wq, peer_keys, peer_u, peer_v, w_ple, w_gate)
        final = l == depth - 1
        pp = p_prompt[l].reshape(batch * t_len, -1)
        ps = p_sample[l].reshape(dec_batch, -1)
        xp, c_p, n_p, m_p, buf_p = _prompt_layer(xp, pp, lw, nf, final, batch, t_len)
        xs, c_s, n_s, m_s, buf_s = _sample_layer(xs, ps, state_C[l], state_n[l], state_m[l],
                                                 state_pool[l], lw, nf, final)
        outs_p.append((c_p, n_p, m_p, buf_p))
        outs_s.append((c_s, n_s, m_s, buf_s))

    stack = lambda outs, i: jnp.stack([o[i] for o in outs])
    return (xp.reshape(batch, t_len, d), xs.reshape(dec_batch, dec_len, d),
            stack(outs_p, 0), stack(outs_p, 1), stack(outs_p, 2), stack(outs_p, 3),
            stack(outs_s, 0), stack(outs_s, 1), stack(outs_s, 2), stack(outs_s, 3))
```

```python
import functools
import math

import jax
import jax.numpy as jnp
from jax import lax
from jax.experimental import pallas as pl
from jax.experimental.pallas import tpu as pltpu

F32 = jnp.float32
BF16 = jnp.bfloat16
EPS = 1e-6
NEG_INF = float("-inf")

LANE = 128
V7X_VMEM_LIMIT = 56 * 1024 * 1024

HEADS = 4
HEAD_DIM = 128
MLSTM_W = HEADS * HEAD_DIM
POOL_WINDOWS = (2, 4, 8, 16)
POOL_GROUP = 128
POOL_W = POOL_GROUP * len(POOL_WINDOWS)
POOL_BUF = max(POOL_WINDOWS) - 1
CHUNK = 128
PEER_HEADS = 8
PEER_NKEYS = 128
PEER_HALF = 128
PEER_TOPK = 16
UNRANKED = 127.0
GELU_C0 = math.sqrt(2.0 / math.pi)
GELU_C1 = 0.044715 * GELU_C0


def _params(semantics):
    return pltpu.CompilerParams(dimension_semantics=semantics, vmem_limit_bytes=V7X_VMEM_LIMIT)


def _token_block(n, want):
    tb = min(n, want)
    assert n % tb == 0
    return tb


def _rmsnorm(x, g):
    return x * lax.rsqrt(jnp.mean(x * x, axis=-1, keepdims=True) + EPS) * g


def _log_sigmoid(x):
    return jnp.minimum(x, 0.0) - jnp.log(1.0 + jnp.exp(-jnp.abs(x)))


def _sigmoid(x):
    return 1.0 / (1.0 + jnp.exp(-x))


def _dot(a, b):
    return jnp.dot(a, b, preferred_element_type=F32)


def _dot_nt(a, b):
    return lax.dot_general(a, b, (((1,), (1,)), ((), ())), preferred_element_type=F32)


def _proj_kernel(x_ref, g_ref, wa_ref, wkt_ref, wgt_ref, bg_ref,
                 q_ref, kt_ref, v_ref, o_ref, u_ref, gt_ref):
    hn = _rmsnorm(x_ref[...], g_ref[...]).astype(BF16)
    pa = _dot(hn, wa_ref[...])
    w = MLSTM_W
    q_ref[...] = pa[:, 0:w].astype(BF16)
    v_ref[...] = pa[:, w:2 * w].astype(BF16)
    o_ref[...] = _sigmoid(pa[:, 2 * w:3 * w])
    u_ref[...] = pa[:, 3 * w:4 * w]
    kt = _dot_nt(wkt_ref[...], hn) * (HEAD_DIM ** -0.5)
    kt_ref[...] = kt.astype(BF16)
    gt_ref[...] = _dot_nt(wgt_ref[...], hn) + bg_ref[...]


def _proj(x, norm_g, wa, wkt, wgt, bg):
    n, d = x.shape
    tb = _token_block(n, 512)
    grid = (n // tb,)
    const = lambda i: (0, 0)
    return pl.pallas_call(
        _proj_kernel,
        grid=grid,
        in_specs=[
            pl.BlockSpec((tb, d), lambda i: (i, 0)),
            pl.BlockSpec((1, d), const),
            pl.BlockSpec(wa.shape, const),
            pl.BlockSpec(wkt.shape, const),
            pl.BlockSpec(wgt.shape, const),
            pl.BlockSpec(bg.shape, const),
        ],
        out_specs=[
            pl.BlockSpec((tb, MLSTM_W), lambda i: (i, 0)),
            pl.BlockSpec((MLSTM_W, tb), lambda i: (0, i)),
            pl.BlockSpec((tb, MLSTM_W), lambda i: (i, 0)),
            pl.BlockSpec((tb, MLSTM_W), lambda i: (i, 0)),
            pl.BlockSpec((tb, POOL_W), lambda i: (i, 0)),
            pl.BlockSpec((2 * HEADS, tb), lambda i: (0, i)),
        ],
        out_shape=[
            jax.ShapeDtypeStruct((n, MLSTM_W), BF16),
            jax.ShapeDtypeStruct((MLSTM_W, n), BF16),
            jax.ShapeDtypeStruct((n, MLSTM_W), BF16),
            jax.ShapeDtypeStruct((n, MLSTM_W), F32),
            jax.ShapeDtypeStruct((n, POOL_W), F32),
            jax.ShapeDtypeStruct((2 * HEADS, n), F32),
        ],
        compiler_params=_params(("parallel",)),
        name="proj",
    )(x, norm_g, wa, wkt, wgt, bg)


def _lane_cumsum(x):
    lane = lax.broadcasted_iota(jnp.int32, x.shape, 1)
    shift = 1
    while shift < x.shape[1]:
        x = x + jnp.where(lane >= shift, pltpu.roll(x, shift, axis=1), 0.0)
        shift *= 2
    return x


def _mlstm_kernel(q_ref, kt_ref, v_ref, o_ref, ig_ref, lf_ref, nw_ref,
                  hm_ref, cext_out_ref, m_out_ref, cext_ref):
    t_len = q_ref.shape[0]
    n_chunks = t_len // CHUNK
    ig2 = ig_ref[0]
    lf2 = _log_sigmoid(lf_ref[0])
    b2 = _lane_cumsum(lf2)
    a2 = ig2 - b2

    t_idx = lax.broadcasted_iota(jnp.int32, (CHUNK, CHUNK), 0)
    s_idx = lax.broadcasted_iota(jnp.int32, (CHUNK, CHUNK), 1)
    causal = s_idx <= t_idx
    ones_col = (lax.broadcasted_iota(jnp.int32, (CHUNK, HEAD_DIM), 1) == 0).astype(BF16)
    nw = nw_ref[...]

    cext_ref[...] = jnp.zeros_like(cext_ref)
    m = jnp.zeros((1, 1), F32)
    for c in range(n_chunks):
        rows = pl.ds(c * CHUNK, CHUNK)
        a_row = a2[c:c + 1, :]
        lf_row = lf2[c:c + 1, :]
        b_last = b2[c:c + 1, CHUNK - 1:CHUNK]
        b_col = jnp.sum(jnp.where(causal, lf_row, 0.0), axis=1, keepdims=True)
        amax_col = jnp.max(jnp.where(causal, a_row, NEG_INF), axis=1, keepdims=True)
        m_col = jnp.maximum(amax_col, m)
        decay_mat = jnp.where(causal, jnp.exp(a_row - m_col), 0.0)

        qc = q_ref[rows, :]
        ktc = kt_ref[:, rows]
        v_ext = jnp.concatenate([v_ref[rows, :], ones_col], axis=1)
        s = _dot(qc, ktc) * decay_mat
        intra = _dot(s.astype(BF16), v_ext)
        inter = _dot(qc, cext_ref[...].astype(BF16))
        tot = intra + jnp.exp(m - m_col) * inter
        num = tot[:, :HEAD_DIM]
        den = tot[:, HEAD_DIM:HEAD_DIM + 1]
        h = num / jnp.maximum(jnp.abs(den), jnp.exp(-(b_col + m_col)))
        h = h * lax.rsqrt(jnp.mean(h * h, axis=1, keepdims=True) + EPS)
        hm_ref[rows, :] = (h * nw * o_ref[rows, :]).astype(BF16)

        m_new = jnp.maximum(b_last + m, jnp.max(a_row, axis=1, keepdims=True) + b_last)
        ws_row = jnp.exp(a_row + (b_last - m_new))
        kws = (ktc.astype(F32) * ws_row).astype(BF16)
        cext_ref[...] = jnp.exp(b_last + m - m_new) * cext_ref[...] + _dot(kws, v_ext)
        m = m_new

    cext_out_ref[0, 0] = cext_ref[...]
    m_out_ref[0, 0] = jnp.broadcast_to(m, m_out_ref.shape[2:])


def _mlstm(q, kt, v, o, gates3, norm_w, batch, t_len):
    n = q.shape[0]
    n_chunks = t_len // CHUNK
    seq = lambda b, h: (b, h)
    return pl.pallas_call(
        _mlstm_kernel,
        grid=(batch, HEADS),
        in_specs=[
            pl.BlockSpec((t_len, HEAD_DIM), seq),
            pl.BlockSpec((HEAD_DIM, t_len), lambda b, h: (h, b)),
            pl.BlockSpec((t_len, HEAD_DIM), seq),
            pl.BlockSpec((t_len, HEAD_DIM), seq),
            pl.BlockSpec((1, n_chunks, CHUNK), lambda b, h: (h, b, 0)),
            pl.BlockSpec((1, n_chunks, CHUNK), lambda b, h: (h + HEADS, b, 0)),
            pl.BlockSpec((1, HEAD_DIM), lambda b, h: (0, h)),
        ],
        out_specs=[
            pl.BlockSpec((t_len, HEAD_DIM), seq),
            pl.BlockSpec((1, 1, HEAD_DIM, 2 * HEAD_DIM), lambda b, h: (b, h, 0, 0)),
            pl.BlockSpec((1, 1, 8, LANE), lambda b, h: (b, h, 0, 0)),
        ],
        out_shape=[
            jax.ShapeDtypeStruct((n, MLSTM_W), BF16),
            jax.ShapeDtypeStruct((batch, HEADS, HEAD_DIM, 2 * HEAD_DIM), F32),
            jax.ShapeDtypeStruct((batch, HEADS, 8, LANE), F32),
        ],
        scratch_shapes=[pltpu.VMEM((HEAD_DIM, 2 * HEAD_DIM), F32)],
        compiler_params=_params(("parallel", "parallel")),
        name="mlstm",
    )(q, kt, v, o, gates3, gates3, norm_w)


def _mlstm1_kernel(q_ref, k_ref, v_ref, o_ref, g_ref, c_ref, n_ref, m_ref, nw_ref,
                   hm_ref, c_out_ref, n_out_ref, m_out_ref):
    bb = q_ref.shape[0]
    row = lax.broadcasted_iota(jnp.int32, (bb, HEAD_DIM), 0)
    eye = (lax.broadcasted_iota(jnp.int32, (HEAD_DIM, HEAD_DIM), 0)
           == lax.broadcasted_iota(jnp.int32, (HEAD_DIM, HEAD_DIM), 1))
    lane_h = lax.broadcasted_iota(jnp.int32, (bb, HEADS), 1)
    g = g_ref[...]
    m_all = m_ref[...]
    m_out = jnp.zeros((bb, HEADS), F32)
    for h in range(HEADS):
        cols = slice(h * HEAD_DIM, (h + 1) * HEAD_DIM)
        qb = q_ref[:, cols]
        qf = qb.astype(F32)
        kf = k_ref[:, cols].astype(F32)
        vf = v_ref[:, cols].astype(F32)
        ig = g[:, h:h + 1]
        lf = _log_sigmoid(g[:, HEADS + h:HEADS + h + 1])
        m_old = m_all[:, h:h + 1]
        n_old = n_ref[:, cols]
        m_new = jnp.maximum(ig, lf + m_old)
        w_in = jnp.exp(ig - m_new)
        w_st = jnp.exp(lf + m_old - m_new)
        qc = jnp.zeros((bb, HEAD_DIM), F32)
        for j in range(bb):
            c_old = c_ref[j, h]
            res = _dot(qb, c_old.astype(BF16))
            qc = jnp.where(row == j, res, qc)
            k_col = jnp.sum(jnp.where(eye, kf[j:j + 1, :], 0.0), axis=1, keepdims=True)
            c_out_ref[j, h] = w_st[j:j + 1, :] * c_old + k_col * (w_in[j:j + 1, :] * vf[j:j + 1, :])
        s = jnp.sum(qf * kf, axis=1, keepdims=True) * w_in
        num = s * vf + w_st * qc
        den = s + w_st * jnp.sum(qf * n_old, axis=1, keepdims=True)
        hh = num / jnp.maximum(jnp.abs(den), jnp.exp(-m_new))
        hh = hh * lax.rsqrt(jnp.mean(hh * hh, axis=1, keepdims=True) + EPS)
        hm_ref[:, cols] = (hh * nw_ref[:, cols] * o_ref[:, cols]).astype(BF16)
        n_out_ref[:, cols] = w_st * n_old + w_in * kf
        m_out = jnp.where(lane_h == h, m_new, m_out)
    m_out_ref[...] = m_out


def _mlstm1(q, k, v, o, g, c_all, layer, n0, m0, norm_w):
    batch = q.shape[0]
    bb = _token_block(batch, 16)
    rows = lambda i: (i, 0)
    return pl.pallas_call(
        _mlstm1_kernel,
        grid=(batch // bb,),
        in_specs=[
            pl.BlockSpec((bb, MLSTM_W), rows),
            pl.BlockSpec((bb, MLSTM_W), rows),
            pl.BlockSpec((bb, MLSTM_W), rows),
            pl.BlockSpec((bb, MLSTM_W), rows),
            pl.BlockSpec((bb, 2 * HEADS), rows),
            pl.BlockSpec((None, bb, HEADS, HEAD_DIM, HEAD_DIM), lambda i: (layer, i, 0, 0, 0)),
            pl.BlockSpec((bb, MLSTM_W), rows),
            pl.BlockSpec((bb, HEADS), rows),
            pl.BlockSpec((1, MLSTM_W), lambda i: (0, 0)),
        ],
        out_specs=[
            pl.BlockSpec((bb, MLSTM_W), rows),
            pl.BlockSpec((bb, HEADS, HEAD_DIM, HEAD_DIM), lambda i: (i, 0, 0, 0)),
            pl.BlockSpec((bb, MLSTM_W), rows),
            pl.BlockSpec((bb, HEADS), rows),
        ],
        out_shape=[
            jax.ShapeDtypeStruct((batch, MLSTM_W), BF16),
            jax.ShapeDtypeStruct(c_all.shape[1:], F32),
            jax.ShapeDtypeStruct(n0.shape, F32),
            jax.ShapeDtypeStruct(m0.shape, F32),
        ],
        compiler_params=_params(("parallel",)),
        name="mlstm1",
    )(q, k, v, o, g, c_all, n0, m0, norm_w)


def _pool_kernel(u_ref, wp_ref, sc_ref, z_ref, buf_ref):
    t_len = u_ref.shape[0]
    t_idx = lax.broadcasted_iota(jnp.int32, (t_len, POOL_GROUP), 0)
    for g, w in enumerate(POOL_WINDOWS):
        cols = slice(g * POOL_GROUP, (g + 1) * POOL_GROUP)
        x = u_ref[:, cols]
        s = x
        k = 1
        while k < w:
            s = s + jnp.where(t_idx >= k, pltpu.roll(s, k, axis=0), 0.0)
            k *= 2
        cnt = jnp.minimum(t_idx + 1, w).astype(F32)
        r = s / cnt - x
        z = _dot(r.astype(BF16), wp_ref[g]) * sc_ref[:, cols]
        z_ref[:, cols] = z.astype(BF16)
    buf_ref[0] = u_ref[t_len - POOL_BUF:t_len, :]


def _pool(u, w_pool, scale, batch, t_len):
    n = u.shape[0]
    return pl.pallas_call(
        _pool_kernel,
        grid=(batch,),
        in_specs=[
            pl.BlockSpec((t_len, POOL_W), lambda b: (b, 0)),
            pl.BlockSpec(w_pool.shape, lambda b: (0, 0, 0)),
            pl.BlockSpec((1, POOL_W), lambda b: (0, 0)),
        ],
        out_specs=[
            pl.BlockSpec((t_len, POOL_W), lambda b: (b, 0)),
            pl.BlockSpec((1, POOL_BUF, POOL_W), lambda b: (b, 0, 0)),
        ],
        out_shape=[
            jax.ShapeDtypeStruct((n, POOL_W), BF16),
            jax.ShapeDtypeStruct((batch, POOL_BUF, POOL_W), F32),
        ],
        compiler_params=_params(("parallel",)),
        name="pool",
    )(u, w_pool, scale)


def _pool1_kernel(u_ref, buft_ref, wp_ref, sc_ref, z_ref, buft_out_ref):
    for g, w in enumerate(POOL_WINDOWS):
        cols = slice(g * POOL_GROUP, (g + 1) * POOL_GROUP)
        x = u_ref[:, cols]
        s = x
        for j in range(1, w):
            s = s + buft_ref[POOL_BUF - j, :, cols]
        r = s / float(w) - x
        z = _dot(r.astype(BF16), wp_ref[g]) * sc_ref[:, cols]
        z_ref[:, cols] = z.astype(BF16)
    for j in range(POOL_BUF - 1):
        buft_out_ref[j] = buft_ref[j + 1]
    buft_out_ref[POOL_BUF - 1] = u_ref[...]


def _pool1(u, buf_t, w_pool, scale):
    batch = u.shape[0]
    full2 = lambda i: (0, 0)
    full3 = lambda i: (0, 0, 0)
    return pl.pallas_call(
        _pool1_kernel,
        grid=(1,),
        in_specs=[
            pl.BlockSpec(u.shape, full2),
            pl.BlockSpec(buf_t.shape, full3),
            pl.BlockSpec(w_pool.shape, full3),
            pl.BlockSpec((1, POOL_W), full2),
        ],
        out_specs=[
            pl.BlockSpec((batch, POOL_W), full2),
            pl.BlockSpec(buf_t.shape, full3),
        ],
        out_shape=[
            jax.ShapeDtypeStruct((batch, POOL_W), BF16),
            jax.ShapeDtypeStruct(buf_t.shape, F32),
        ],
        compiler_params=_params(("arbitrary",)),
        name="pool1",
    )(u, buf_t, w_pool, scale)


def _mix_kernel(x_ref, hm_ref, zp_ref, wo_ref, g_ref, x1_ref, xnt_ref):
    mix = _dot(hm_ref[...], wo_ref[0:MLSTM_W, :]) + _dot(zp_ref[...], wo_ref[MLSTM_W:, :])
    x1 = x_ref[...] + mix
    x1_ref[...] = x1
    xnt_ref[...] = _rmsnorm(x1, g_ref[...]).T.astype(BF16)


def _mix(x, hm, zp, w_out, norm_g):
    n, d = x.shape
    tb = _token_block(n, 512)
    rows = lambda i: (i, 0)
    const = lambda i: (0, 0)
    return pl.pallas_call(
        _mix_kernel,
        grid=(n // tb,),
        in_specs=[
            pl.BlockSpec((tb, d), rows),
            pl.BlockSpec((tb, MLSTM_W), rows),
            pl.BlockSpec((tb, POOL_W), rows),
            pl.BlockSpec(w_out.shape, const),
            pl.BlockSpec((1, d), const),
        ],
        out_specs=[
            pl.BlockSpec((tb, d), rows),
            pl.BlockSpec((d, tb), lambda i: (0, i)),
        ],
        out_shape=[
            jax.ShapeDtypeStruct((n, d), F32),
            jax.ShapeDtypeStruct((d, n), BF16),
        ],
        compiler_params=_params(("parallel",)),
        name="mix",
    )(x, hm, zp, w_out, norm_g)


def _extract_max(vals, pos):
    m = vals[0]
    for v in vals[1:]:
        m = jnp.maximum(m, v)
    m = jnp.max(m, axis=0, keepdims=True)
    big = jnp.float32(1e9)
    idx = None
    for v, p in zip(vals, pos):
        cand = jnp.where(v == m, p, big)
        idx = cand if idx is None else jnp.minimum(idx, cand)
    idx = jnp.min(idx, axis=0, keepdims=True)
    return m, idx


def _top16(s):
    tb = s.shape[1]
    row = lax.broadcasted_iota(jnp.int32, s.shape, 0).astype(F32)
    krow = lax.broadcasted_iota(jnp.int32, (PEER_TOPK, tb), 0)
    rank = jnp.full(s.shape, UNRANKED, F32)
    sv = jnp.zeros((PEER_TOPK, tb), F32)
    for k in range(PEER_TOPK):
        m, idx = _extract_max([s], [row])
        hit = row == idx
        rank = jnp.where(hit, float(k), rank)
        s = jnp.where(hit, NEG_INF, s)
        sv = jnp.where(krow == k, m, sv)
    return sv, rank


def _candidate_groups(sv1, sv2):
    tb = sv1.shape[1]
    vals, pos = [], []
    r8 = lax.broadcasted_iota(jnp.int32, (8, tb), 0)
    vals.append(sv1[0:1, :] + sv2[8:16, :])
    pos.append((r8 + 8).astype(F32))
    vals.append(sv1[0:1, :] + sv2[0:8, :])
    pos.append(r8.astype(F32))
    for k1 in range(1, 8):
        lim = PEER_TOPK // (k1 + 1)
        v = sv1[k1:k1 + 1, :] + sv2[0:8, :]
        vals.append(jnp.where(r8 < lim, v, NEG_INF))
        pos.append((r8 + k1 * PEER_TOPK).astype(F32))
    vals.append(sv1[8:16, :] + sv2[0:1, :])
    pos.append(((r8 + 8) * PEER_TOPK).astype(F32))
    return vals, pos


def _sorting_pairs(n):
    pairs = []
    p = 1
    while p < n:
        k = p
        while k >= 1:
            for j in range(k % p, n - k, 2 * k):
                for i in range(min(k, n - j - k)):
                    if (i + j) // (2 * p) == (i + j + k) // (2 * p):
                        pairs.append((i + j, i + j + k))
            k //= 2
        p *= 2
    return pairs


def _sublane_allreduce(x, op):
    for d in (1, 2, 4):
        x = op(x, pltpu.roll(x, d, axis=0))
    return x


def _sorted_top16(groups):
    g = list(groups)
    for i, j in _sorting_pairs(len(g)):
        g[i], g[j] = jnp.maximum(g[i], g[j]), jnp.minimum(g[i], g[j])
    n = len(g)
    for d in (1, 2, 4):
        p = [pltpu.roll(x, d, axis=0) for x in g]
        g = [jnp.maximum(g[i], p[n - 1 - i]) for i in range(n)]
        stride = n // 2
        while stride >= 1:
            for i in range(n):
                if i & stride == 0:
                    g[i], g[i + stride] = (jnp.maximum(g[i], g[i + stride]),
                                           jnp.minimum(g[i], g[i + stride]))
            stride //= 2
    return g


def _rank_bits(x, sv):
    c8 = x < sv[7]
    t = jnp.where(c8, sv[11], sv[3])
    c4 = x < t
    t = jnp.where(c8, jnp.where(c4, sv[13], sv[9]), jnp.where(c4, sv[5], sv[1]))
    c2 = x < t
    t = jnp.where(c8,
                  jnp.where(c4, jnp.where(c2, sv[14], sv[12]), jnp.where(c2, sv[10], sv[8])),
                  jnp.where(c4, jnp.where(c2, sv[6], sv[4]), jnp.where(c2, sv[2], sv[0])))
    c1 = x < t
    return (c8, c4, c2, c1), x < sv[15]


def _select16(bits, rows):
    c8, c4, c2, c1 = bits
    lvl = [jnp.where(c1, rows[2 * i + 1], rows[2 * i]) for i in range(8)]
    lvl = [jnp.where(c2, lvl[2 * i + 1], lvl[2 * i]) for i in range(4)]
    lvl = [jnp.where(c4, lvl[2 * i + 1], lvl[2 * i]) for i in range(2)]
    return jnp.where(c8, lvl[1], lvl[0])


def _route_fast(s1, s2):
    tb = s1.shape[1]
    r8 = lax.broadcasted_iota(jnp.int32, (8, tb), 0)
    n_grp = PEER_NKEYS // 8
    g1 = [s1[v * 8:(v + 1) * 8, :] for v in range(n_grp)]
    g2 = [s2[v * 8:(v + 1) * 8, :] for v in range(n_grp)]
    sv1 = _sorted_top16(g1)
    sv2 = _sorted_top16(g2)

    bad = jnp.zeros((8, tb), F32)
    for sv, grp in ((sv1, g1), (sv2, g2)):
        for k in range(PEER_TOPK - 1):
            bad = jnp.where(sv[k] == sv[k + 1], 1.0, bad)
        n_in = jnp.zeros((8, tb), F32)
        for x in grp:
            n_in = n_in + jnp.where(x >= sv[PEER_TOPK - 1], 1.0, 0.0)
        n_in = _sublane_allreduce(n_in, jnp.add)
        bad = jnp.where(n_in != float(PEER_TOPK), 1.0, bad)

    def by_sublane(rows):
        out = jnp.zeros((8, tb), F32)
        for k, row in enumerate(rows):
            out = jnp.where(r8 == k, row, out)
        return out
    a2_lo, a2_hi, a1_hi = by_sublane(sv2[0:8]), by_sublane(sv2[8:16]), by_sublane(sv1[8:16])
    orig = [sv1[0] + a2_hi, sv1[0] + a2_lo]
    for k1 in range(1, 8):
        orig.append(jnp.where(r8 < PEER_TOPK // (k1 + 1), sv1[k1] + a2_lo, NEG_INF))
    orig.append(a1_hi + sv2[0])
    vals = list(orig)
    prev = None
    for _ in range(PEER_TOPK):
        m = vals[0]
        for v in vals[1:]:
            m = jnp.maximum(m, v)
        m = _sublane_allreduce(m, jnp.maximum)
        vals = [jnp.where(v == m, NEG_INF, v) for v in vals]
        if prev is not None:
            bad = jnp.where(m == prev, 1.0, bad)
        prev = m
    sel = [o >= prev for o in orig]
    cmax = sv1[0] + sv2[0]
    z = jnp.zeros((8, tb), F32)
    for o, sl in zip(orig, sel):
        z = z + jnp.where(sl, jnp.exp(o - cmax), 0.0)
    z = _sublane_allreduce(z, jnp.add)
    ones = [jnp.where(sl, 1.0, 0.0) for sl in sel]
    cnt = [_sublane_allreduce(ones[0] + ones[1], jnp.add)]
    cnt += [_sublane_allreduce(ones[k1 + 1], jnp.add) for k1 in range(1, 8)]
    cnt += [ones[9][j:j + 1, :] for j in range(8)]
    total = cnt[0]
    for c in cnt[1:]:
        total = total + c
    bad = jnp.where(total != float(PEER_TOPK), 1.0, bad)

    cnt1, r2 = [], []
    weights = (8.0, 4.0, 2.0, 1.0)
    for x in g1:
        bits, below = _rank_bits(x, sv1)
        cnt1.append(jnp.where(below, 0.0, _select16(bits, cnt)))
    for x in g2:
        bits, below = _rank_bits(x, sv2)
        rank = jnp.zeros((8, tb), F32)
        for b, wgt in zip(bits, weights):
            rank = rank + jnp.where(b, wgt, 0.0)
        r2.append(jnp.where(below, UNRANKED, rank))
    return cnt1, r2, sv1[0][0:1, :], sv2[0][0:1, :], z[0:1, :], bad


def _route_exact(s1, s2):
    sv1, r1 = _top16(s1)
    sv2, r2 = _top16(s2)

    vals, pos = _candidate_groups(sv1, sv2)
    orig = list(vals)
    sel = [jnp.zeros(v.shape, F32) for v in vals]
    for _ in range(PEER_TOPK):
        _, idx = _extract_max(vals, pos)
        for i in range(len(vals)):
            hit = pos[i] == idx
            sel[i] = jnp.where(hit, 1.0, sel[i])
            vals[i] = jnp.where(hit, NEG_INF, vals[i])

    cmax = orig[1][0:1, :]
    z = None
    for o, sl in zip(orig, sel):
        part = jnp.sum(jnp.where(sl > 0.0, jnp.exp(o - cmax), 0.0), axis=0, keepdims=True)
        z = part if z is None else z + part
    cnt = [jnp.sum(sel[0] + sel[1], axis=0, keepdims=True)]
    cnt += [jnp.sum(sel[k1 + 1], axis=0, keepdims=True) for k1 in range(1, 8)]
    cnt += [sel[9][j:j + 1, :] for j in range(8)]

    cnt1 = jnp.zeros(r1.shape, F32)
    for k1 in range(PEER_TOPK):
        cnt1 = jnp.where(r1 == float(k1), cnt[k1], cnt1)
    return cnt1, r2, sv1[0:1, :], sv2[0:1, :], z


ROUTE_HEADS_PER_STEP = 4


def _route_kernel(xnt_ref, wqt_ref, k1_ref, k2_ref, cnt1_ref, g1_ref, r2_ref, e2_ref, s_ref):
    qt_all = _dot(wqt_ref[...], xnt_ref[...]).astype(BF16)

    def emit_gates(hh, s1, s2, max1, max2, z):
        g1_ref[hh] = jnp.exp(s1 - max1) * (0.5 / z)
        e2_ref[hh] = jnp.exp(s2 - max2).astype(BF16)

    ties = []
    for hh in range(ROUTE_HEADS_PER_STEP):
        qt = qt_all[hh * 2 * PEER_HALF:(hh + 1) * 2 * PEER_HALF, :]
        s1 = _dot(k1_ref[hh], qt[0:PEER_HALF, :])
        s2 = _dot(k2_ref[hh], qt[PEER_HALF:, :])
        s_ref[hh, 0] = s1
        s_ref[hh, 1] = s2
        cnt1, r2, max1, max2, z, tie = _route_fast(s1, s2)
        for v in range(PEER_NKEYS // 8):
            cnt1_ref[hh, v * 8:(v + 1) * 8, :] = cnt1[v]
            r2_ref[hh, v * 8:(v + 1) * 8, :] = r2[v].astype(BF16)
        emit_gates(hh, s1, s2, max1, max2, z)
        ties.append(jnp.max(tie))

    for hh in range(ROUTE_HEADS_PER_STEP):
        @pl.when(ties[hh] > 0.0)
        def _(hh=hh):
            s1, s2 = s_ref[hh, 0], s_ref[hh, 1]
            cnt1, r2, max1, max2, z = _route_exact(s1, s2)
            cnt1_ref[hh] = cnt1
            r2_ref[hh] = r2.astype(BF16)
            emit_gates(hh, s1, s2, max1, max2, z)


def _route(xnt, wqt, keys):
    d, n = xnt.shape
    tb = _token_block(n, 256)
    hps = ROUTE_HEADS_PER_STEP
    per_head = pl.BlockSpec((hps, PEER_NKEYS, tb), lambda i, h: (h, 0, i))
    shape = lambda dt: jax.ShapeDtypeStruct((PEER_HEADS, PEER_NKEYS, n), dt)
    return pl.pallas_call(
        _route_kernel,
        grid=(n // tb, PEER_HEADS // hps),
        in_specs=[
            pl.BlockSpec((d, tb), lambda i, h: (0, i)),
            pl.BlockSpec((hps * 2 * PEER_HALF, d), lambda i, h: (h, 0)),
            pl.BlockSpec((hps, PEER_NKEYS, PEER_HALF), lambda i, h: (h, 0, 0)),
            pl.BlockSpec((hps, PEER_NKEYS, PEER_HALF), lambda i, h: (h + PEER_HEADS // hps, 0, 0)),
        ],
        out_specs=[per_head] * 4,
        out_shape=[shape(F32), shape(F32), shape(BF16), shape(BF16)],
        scratch_shapes=[pltpu.VMEM((hps, 2, PEER_NKEYS, tb), F32)],
        compiler_params=_params(("parallel", "parallel")),
        name="route",
    )(xnt, wqt, keys, keys)


I1_PER_TILE = 16
BF16_TILE = (8, 2 * LANE)
EXPERT_TILE = I1_PER_TILE * PEER_NKEYS


def _experts_kernel(xnt_ref, cnt1_ref, g1_ref, r2_ref, e2_ref, u_ref, vt_ref, x1_ref,
                    x2_ref, acc_ref, act_ref, wa_ref):
    e = pl.program_id(1)
    tb = xnt_ref.shape[1]

    @pl.when(e == 0)
    def _():
        acc_ref[...] = jnp.zeros_like(acc_ref)

    a = _dot(u_ref[...], xnt_ref[...]).astype(BF16)
    act_ref[...] = a * (1.0 + jnp.tanh(a * (GELU_C0 + GELU_C1 * (a * a))))
    sub, width = BF16_TILE[0], min(BF16_TILE[1], tb)
    assert I1_PER_TILE % sub == 0
    i1_rows = pl.ds(pl.multiple_of(e * I1_PER_TILE, I1_PER_TILE), I1_PER_TILE)
    for lt in range(tb // width):
        lanes = pl.ds(lt * width, width)
        cnt_tile = [cnt1_ref[h, i1_rows, lanes] for h in range(PEER_HEADS)]
        g1_tile = [g1_ref[h, i1_rows, lanes] for h in range(PEER_HEADS)]
        for j in range(I1_PER_TILE):
            cnt_b = [jnp.broadcast_to(t[j:j + 1, :], (sub, width)).astype(BF16) for t in cnt_tile]
            g1_b = [jnp.broadcast_to(t[j:j + 1, :], (sub, width)).astype(BF16) for t in g1_tile]
            for rb in range(PEER_NKEYS // sub):
                krows = pl.ds(rb * sub, sub)
                w = None
                for h in range(PEER_HEADS):
                    mask = r2_ref[h, krows, lanes] < cnt_b[h]
                    term = jnp.where(mask, e2_ref[h, krows, lanes] * g1_b[h], jnp.zeros((), BF16))
                    w = term if w is None else w + term
                erows = pl.ds(j * PEER_NKEYS + rb * sub, sub)
                wa_ref[erows, lanes] = w * act_ref[erows, lanes]
    acc_ref[...] += _dot(vt_ref[...], wa_ref[...])

    @pl.when(e == pl.num_programs(1) - 1)
    def _():
        x2_ref[...] = x1_ref[...] + acc_ref[...].T


def _experts(xnt, cnt1, g1, r2, e2, u_tabs, vt_tabs, layer, x1):
    d, n = xnt.shape
    n_exp = u_tabs.shape[1]
    tb = _token_block(n, 512)
    assert tb % LANE == 0 and n_exp % EXPERT_TILE == 0
    per_head = pl.BlockSpec((PEER_HEADS, PEER_NKEYS, tb), lambda i, e: (0, 0, i))
    tile_buf = pltpu.VMEM((EXPERT_TILE, tb), BF16)
    return pl.pallas_call(
        _experts_kernel,
        grid=(n // tb, n_exp // EXPERT_TILE),
        in_specs=[
            pl.BlockSpec((d, tb), lambda i, e: (0, i)),
            per_head, per_head, per_head, per_head,
            pl.BlockSpec((None, EXPERT_TILE, d), lambda i, e: (layer, e, 0)),
            pl.BlockSpec((None, d, EXPERT_TILE), lambda i, e: (layer, 0, e)),
            pl.BlockSpec((tb, d), lambda i, e: (i, 0)),
        ],
        out_specs=pl.BlockSpec((tb, d), lambda i, e: (i, 0)),
        out_shape=jax.ShapeDtypeStruct((n, d), F32),
        scratch_shapes=[pltpu.VMEM((d, tb), F32), tile_buf, tile_buf],
        compiler_params=_params(("parallel", "arbitrary")),
        name="experts",
    )(xnt, cnt1, g1, r2, e2, u_tabs, vt_tabs, x1)


def _ple_kernel(x_ref, p_ref, wg_ref, wp_ref, nf_ref, y_ref, *, final_norm):
    x = x_ref[...]
    gate = _sigmoid(_dot(x.astype(BF16), wg_ref[...]))
    y = x + gate * _dot(p_ref[...].astype(BF16), wp_ref[...])
    if final_norm:
        y = _rmsnorm(y, nf_ref[...])
    y_ref[...] = y


def _ple(x, p_all, layer, w_gate, w_ple, norm_f, final_norm):
    n, d = x.shape
    tb = _token_block(n, 512)
    rows = lambda i: (i, 0)
    const = lambda i: (0, 0)
    return pl.pallas_call(
        functools.partial(_ple_kernel, final_norm=final_norm),
        grid=(n // tb,),
        in_specs=[
            pl.BlockSpec((tb, d), rows),
            pl.BlockSpec((None, tb, p_all.shape[2]), lambda i: (layer, i, 0)),
            pl.BlockSpec(w_gate.shape, const),
            pl.BlockSpec(w_ple.shape, const),
            pl.BlockSpec((1, d), const),
        ],
        out_specs=pl.BlockSpec((tb, d), rows),
        out_shape=jax.ShapeDtypeStruct((n, d), F32),
        compiler_params=_params(("parallel",)),
        name="ple",
    )(x, p_all, w_gate, w_ple, norm_f)


def _prep_layer_weights(l, w_in, b_gate, mlstm_norm, w_pool, pool_scale, w_out, norm1, norm2,
                        peer_wq, peer_keys, peer_u, peer_v, w_ple, w_gate):
    w = MLSTM_W
    wi = w_in[l]
    d = wi.shape[0]
    wa = jnp.concatenate([wi[:, 0:w], wi[:, 2 * w:4 * w + POOL_W]], axis=1).astype(BF16)
    wkt = wi[:, w:2 * w].T.astype(BF16)
    wgt = wi[:, 4 * w + POOL_W:].T.astype(BF16)
    return dict(
        wa=wa, wkt=wkt, wgt=wgt,
        bg=b_gate[l].reshape(2 * HEADS, 1).astype(F32),
        mlstm_norm=mlstm_norm[l].reshape(1, w),
        w_pool=w_pool[l].astype(BF16),
        pool_scale=pool_scale[l].reshape(1, POOL_W),
        w_out=w_out[l].astype(BF16),
        norm1=norm1[l].reshape(1, d), norm2=norm2[l].reshape(1, d),
        wqt=peer_wq[l].T.astype(BF16),
        keys=peer_keys[l].reshape(2 * PEER_HEADS, PEER_NKEYS, PEER_HALF).astype(BF16),
        w_ple=w_ple[l].astype(BF16), w_gate=w_gate[l].astype(BF16),
        layer=l,
    )


def _ffn_and_embed(x1, xnt, p, lw, norm_f, final_norm):
    cnt1, g1, r2, e2 = _route(xnt, lw["wqt"], lw["keys"])
    x2 = _experts(xnt, cnt1, g1, r2, e2, lw["u_tabs"], lw["vt_tabs"], lw["layer"], x1)
    return _ple(x2, p, lw["layer"], lw["w_gate"], lw["w_ple"], norm_f, final_norm)


def _prompt_layer(x, p, lw, norm_f, final_norm, batch, t_len):
    q, kt, v, o, u, gt = _proj(x, lw["norm1"], lw["wa"], lw["wkt"], lw["wgt"], lw["bg"])
    gates3 = gt.reshape(2 * HEADS, (batch * t_len) // CHUNK, CHUNK)
    hm, cext, m8 = _mlstm(q, kt, v, o, gates3, lw["mlstm_norm"], batch, t_len)
    zp, buf = _pool(u, lw["w_pool"], lw["pool_scale"], batch, t_len)
    x1, xnt = _mix(x, hm, zp, lw["w_out"], lw["norm2"])
    y = _ffn_and_embed(x1, xnt, p, lw, norm_f, final_norm)
    return y, cext[..., :HEAD_DIM], cext[..., HEAD_DIM], m8[..., 0, 0], buf


def _sample_layer(x, p, c_all, n0, m0, buf0, lw, norm_f, final_norm):
    q, kt, v, o, u, gt = _proj(x, lw["norm1"], lw["wa"], lw["wkt"], lw["wgt"], lw["bg"])
    hm, c1, n1, m1 = _mlstm1(q, kt.T, v, o, gt.T, c_all, lw["layer"], n0.reshape(-1, MLSTM_W), m0,
                             lw["mlstm_norm"])
    n1 = n1.reshape(n0.shape)
    zp, buf_t = _pool1(u, jnp.swapaxes(buf0, 0, 1), lw["w_pool"], lw["pool_scale"])
    x1, xnt = _mix(x, hm, zp, lw["w_out"], lw["norm2"])
    y = _ffn_and_embed(x1, xnt, p, lw, norm_f, final_norm)
    return y, c1, n1, m1, jnp.swapaxes(buf_t, 0, 1)


def kernel(x_prompt, x_sample, p_prompt, p_sample, state_C, state_n, state_m, state_pool,
           w_in, b_gate, mlstm_norm, w_pool, pool_scale, w_out, norm1, norm2,
           peer_wq, peer_keys, peer_u, peer_v, w_ple, w_gate, norm_f):
    depth = w_in.shape[0]
    batch, t_len, d = x_prompt.shape
    dec_batch, dec_len, _ = x_sample.shape
    assert dec_len == 1 and t_len % CHUNK == 0
    nf = norm_f.reshape(1, d)

    xp = x_prompt.reshape(batch * t_len, d)
    xs = x_sample.reshape(dec_batch, d)
    outs_p, outs_s = [], []
    u_tabs = peer_u.astype(BF16)
    vt_tabs = jnp.swapaxes(peer_v, 1, 2).astype(BF16)
    for l in range(depth):
        lw = _prep_layer_weights(l, w_in, b_gate, mlstm_norm, w_pool, pool_scale, w_out, norm1,
                                 norm2, peer_wq, peer_keys, peer_u, peer_v, w_ple, w_gate)
        lw.update(u_tabs=u_tabs, vt_tabs=vt_tabs)
        final = l == depth - 1
        pp = p_prompt.reshape(depth, batch * t_len, -1)
        ps = p_sample.reshape(depth, dec_batch, -1)
        xp, c_p, n_p, m_p, buf_p = _prompt_layer(xp, pp, lw, nf, final, batch, t_len)
        xs, c_s, n_s, m_s, buf_s = _sample_layer(xs, ps, state_C, state_n[l], state_m[l],
                                                 state_pool[l], lw, nf, final)
        outs_p.append((c_p, n_p, m_p, buf_p))
        outs_s.append((c_s, n_s, m_s, buf_s))

    stack = lambda outs, i: jnp.stack([o[i] for o in outs])
    return (xp.reshape(batch, t_len, d), xs.reshape(dec_batch, dec_len, d),
            stack(outs_p, 0), stack(outs_p, 1), stack(outs_p, 2), stack(outs_p, 3),
            stack(outs_s, 0), stack(outs_s, 1), stack(outs_s, 2), stack(outs_s, 3))
```

```python
import functools
import math

import jax
import jax.numpy as jnp
from jax import lax
from jax.experimental import pallas as pl
from jax.experimental.pallas import tpu as pltpu

F32 = jnp.float32
BF16 = jnp.bfloat16
EPS = 1e-6
NEG_INF = float("-inf")

LANE = 128
V7X_VMEM_LIMIT = 56 * 1024 * 1024

HEADS = 4
HEAD_DIM = 128
MLSTM_W = HEADS * HEAD_DIM
POOL_WINDOWS = (2, 4, 8, 16)
POOL_GROUP = 128
POOL_W = POOL_GROUP * len(POOL_WINDOWS)
POOL_BUF = max(POOL_WINDOWS) - 1
CHUNK = 128
PEER_HEADS = 8
PEER_NKEYS = 128
PEER_HALF = 128
PEER_TOPK = 16
UNRANKED = 127.0
GELU_C0 = math.sqrt(2.0 / math.pi)
GELU_C1 = 0.044715 * GELU_C0


def _params(semantics):
    return pltpu.CompilerParams(dimension_semantics=semantics, vmem_limit_bytes=V7X_VMEM_LIMIT)


def _token_block(n, want):
    tb = min(n, want)
    assert n % tb == 0
    return tb


def _rmsnorm(x, g):
    return x * lax.rsqrt(jnp.mean(x * x, axis=-1, keepdims=True) + EPS) * g


def _log_sigmoid(x):
    return jnp.minimum(x, 0.0) - jnp.log(1.0 + jnp.exp(-jnp.abs(x)))


def _sigmoid(x):
    return 1.0 / (1.0 + jnp.exp(-x))


def _dot(a, b):
    return jnp.dot(a, b, preferred_element_type=F32)


def _dot_nt(a, b):
    return lax.dot_general(a, b, (((1,), (1,)), ((), ())), preferred_element_type=F32)


def _proj_kernel(x_ref, g_ref, wa_ref, wkt_ref, wgt_ref, bg_ref,
                 q_ref, kt_ref, v_ref, o_ref, u_ref, gt_ref):
    hn = _rmsnorm(x_ref[...], g_ref[...]).astype(BF16)
    pa = _dot(hn, wa_ref[...])
    w = MLSTM_W
    q_ref[...] = pa[:, 0:w].astype(BF16)
    v_ref[...] = pa[:, w:2 * w].astype(BF16)
    o_ref[...] = _sigmoid(pa[:, 2 * w:3 * w])
    u_ref[...] = pa[:, 3 * w:4 * w]
    kt = _dot_nt(wkt_ref[...], hn) * (HEAD_DIM ** -0.5)
    kt_ref[...] = kt.astype(BF16)
    gt_ref[...] = _dot_nt(wgt_ref[...], hn) + bg_ref[...]


def _proj(x, norm_g, wa, wkt, wgt, bg):
    n, d = x.shape
    tb = _token_block(n, 512)
    grid = (n // tb,)
    const = lambda i: (0, 0)
    return pl.pallas_call(
        _proj_kernel,
        grid=grid,
        in_specs=[
            pl.BlockSpec((tb, d), lambda i: (i, 0)),
            pl.BlockSpec((1, d), const),
            pl.BlockSpec(wa.shape, const),
            pl.BlockSpec(wkt.shape, const),
            pl.BlockSpec(wgt.shape, const),
            pl.BlockSpec(bg.shape, const),
        ],
        out_specs=[
            pl.BlockSpec((tb, MLSTM_W), lambda i: (i, 0)),
            pl.BlockSpec((MLSTM_W, tb), lambda i: (0, i)),
            pl.BlockSpec((tb, MLSTM_W), lambda i: (i, 0)),
            pl.BlockSpec((tb, MLSTM_W), lambda i: (i, 0)),
            pl.BlockSpec((tb, POOL_W), lambda i: (i, 0)),
            pl.BlockSpec((2 * HEADS, tb), lambda i: (0, i)),
        ],
        out_shape=[
            jax.ShapeDtypeStruct((n, MLSTM_W), BF16),
            jax.ShapeDtypeStruct((MLSTM_W, n), BF16),
            jax.ShapeDtypeStruct((n, MLSTM_W), BF16),
            jax.ShapeDtypeStruct((n, MLSTM_W), F32),
            jax.ShapeDtypeStruct((n, POOL_W), F32),
            jax.ShapeDtypeStruct((2 * HEADS, n), F32),
        ],
        compiler_params=_params(("parallel",)),
        name="proj",
    )(x, norm_g, wa, wkt, wgt, bg)


def _lane_cumsum(x):
    lane = lax.broadcasted_iota(jnp.int32, x.shape, 1)
    shift = 1
    while shift < x.shape[1]:
        x = x + jnp.where(lane >= shift, pltpu.roll(x, shift, axis=1), 0.0)
        shift *= 2
    return x


def _mlstm_kernel(q_ref, kt_ref, v_ref, o_ref, ig_ref, lf_ref, nw_ref,
                  hm_ref, cext_out_ref, m_out_ref, cext_ref):
    t_len = q_ref.shape[0]
    n_chunks = t_len // CHUNK
    ig2 = ig_ref[0]
    lf2 = _log_sigmoid(lf_ref[0])
    b2 = _lane_cumsum(lf2)
    a2 = ig2 - b2

    t_idx = lax.broadcasted_iota(jnp.int32, (CHUNK, CHUNK), 0)
    s_idx = lax.broadcasted_iota(jnp.int32, (CHUNK, CHUNK), 1)
    causal = s_idx <= t_idx
    ones_col = (lax.broadcasted_iota(jnp.int32, (CHUNK, HEAD_DIM), 1) == 0).astype(BF16)
    nw = nw_ref[...]

    cext_ref[...] = jnp.zeros_like(cext_ref)
    m = jnp.zeros((1, 1), F32)
    for c in range(n_chunks):
        rows = pl.ds(c * CHUNK, CHUNK)
        a_row = a2[c:c + 1, :]
        lf_row = lf2[c:c + 1, :]
        b_last = b2[c:c + 1, CHUNK - 1:CHUNK]
        b_col = jnp.sum(jnp.where(causal, lf_row, 0.0), axis=1, keepdims=True)
        amax_col = jnp.max(jnp.where(causal, a_row, NEG_INF), axis=1, keepdims=True)
        m_col = jnp.maximum(amax_col, m)
        decay_mat = jnp.where(causal, jnp.exp(a_row - m_col), 0.0)

        qc = q_ref[rows, :]
        ktc = kt_ref[:, rows]
        v_ext = jnp.concatenate([v_ref[rows, :], ones_col], axis=1)
        s = _dot(qc, ktc) * decay_mat
        intra = _dot(s.astype(BF16), v_ext)
        inter = _dot(qc, cext_ref[...].astype(BF16))
        tot = intra + jnp.exp(m - m_col) * inter
        num = tot[:, :HEAD_DIM]
        den = tot[:, HEAD_DIM:HEAD_DIM + 1]
        h = num / jnp.maximum(jnp.abs(den), jnp.exp(-(b_col + m_col)))
        h = h * lax.rsqrt(jnp.mean(h * h, axis=1, keepdims=True) + EPS)
        hm_ref[rows, :] = (h * nw * o_ref[rows, :]).astype(BF16)

        m_new = jnp.maximum(b_last + m, jnp.max(a_row, axis=1, keepdims=True) + b_last)
        ws_row = jnp.exp(a_row + (b_last - m_new))
        kws = (ktc.astype(F32) * ws_row).astype(BF16)
        cext_ref[...] = jnp.exp(b_last + m - m_new) * cext_ref[...] + _dot(kws, v_ext)
        m = m_new

    cext_out_ref[0, 0] = cext_ref[...]
    m_out_ref[0, 0] = jnp.broadcast_to(m, m_out_ref.shape[2:])


def _mlstm(q, kt, v, o, gates3, norm_w, batch, t_len):
    n = q.shape[0]
    n_chunks = t_len // CHUNK
    seq = lambda b, h: (b, h)
    return pl.pallas_call(
        _mlstm_kernel,
        grid=(batch, HEADS),
        in_specs=[
            pl.BlockSpec((t_len, HEAD_DIM), seq),
            pl.BlockSpec((HEAD_DIM, t_len), lambda b, h: (h, b)),
            pl.BlockSpec((t_len, HEAD_DIM), seq),
            pl.BlockSpec((t_len, HEAD_DIM), seq),
            pl.BlockSpec((1, n_chunks, CHUNK), lambda b, h: (h, b, 0)),
            pl.BlockSpec((1, n_chunks, CHUNK), lambda b, h: (h + HEADS, b, 0)),
            pl.BlockSpec((1, HEAD_DIM), lambda b, h: (0, h)),
        ],
        out_specs=[
            pl.BlockSpec((t_len, HEAD_DIM), seq),
            pl.BlockSpec((1, 1, HEAD_DIM, 2 * HEAD_DIM), lambda b, h: (b, h, 0, 0)),
            pl.BlockSpec((1, 1, 8, LANE), lambda b, h: (b, h, 0, 0)),
        ],
        out_shape=[
            jax.ShapeDtypeStruct((n, MLSTM_W), BF16),
            jax.ShapeDtypeStruct((batch, HEADS, HEAD_DIM, 2 * HEAD_DIM), F32),
            jax.ShapeDtypeStruct((batch, HEADS, 8, LANE), F32),
        ],
        scratch_shapes=[pltpu.VMEM((HEAD_DIM, 2 * HEAD_DIM), F32)],
        compiler_params=_params(("parallel", "parallel")),
        name="mlstm",
    )(q, kt, v, o, gates3, gates3, norm_w)


def _mlstm1_kernel(q_ref, k_ref, v_ref, o_ref, g_ref, c_ref, n_ref, m_ref, nw_ref,
                   hm_ref, c_out_ref, n_out_ref, m_out_ref):
    bb = q_ref.shape[0]
    row = lax.broadcasted_iota(jnp.int32, (bb, HEAD_DIM), 0)
    eye = (lax.broadcasted_iota(jnp.int32, (HEAD_DIM, HEAD_DIM), 0)
           == lax.broadcasted_iota(jnp.int32, (HEAD_DIM, HEAD_DIM), 1))
    lane_h = lax.broadcasted_iota(jnp.int32, (bb, HEADS), 1)
    g = g_ref[...]
    m_all = m_ref[...]
    m_out = jnp.zeros((bb, HEADS), F32)
    for h in range(HEADS):
        cols = slice(h * HEAD_DIM, (h + 1) * HEAD_DIM)
        qb = q_ref[:, cols]
        qf = qb.astype(F32)
        kf = k_ref[:, cols].astype(F32)
        vf = v_ref[:, cols].astype(F32)
        ig = g[:, h:h + 1]
        lf = _log_sigmoid(g[:, HEADS + h:HEADS + h + 1])
        m_old = m_all[:, h:h + 1]
        n_old = n_ref[:, cols]
        m_new = jnp.maximum(ig, lf + m_old)
        w_in = jnp.exp(ig - m_new)
        w_st = jnp.exp(lf + m_old - m_new)
        qc = jnp.zeros((bb, HEAD_DIM), F32)
        for j in range(bb):
            c_old = c_ref[j, h]
            res = _dot(qb, c_old.astype(BF16))
            qc = jnp.where(row == j, res, qc)
            k_col = jnp.sum(jnp.where(eye, kf[j:j + 1, :], 0.0), axis=1, keepdims=True)
            c_out_ref[j, h] = w_st[j:j + 1, :] * c_old + k_col * (w_in[j:j + 1, :] * vf[j:j + 1, :])
        s = jnp.sum(qf * kf, axis=1, keepdims=True) * w_in
        num = s * vf + w_st * qc
        den = s + w_st * jnp.sum(qf * n_old, axis=1, keepdims=True)
        hh = num / jnp.maximum(jnp.abs(den), jnp.exp(-m_new))
        hh = hh * lax.rsqrt(jnp.mean(hh * hh, axis=1, keepdims=True) + EPS)
        hm_ref[:, cols] = (hh * nw_ref[:, cols] * o_ref[:, cols]).astype(BF16)
        n_out_ref[:, cols] = w_st * n_old + w_in * kf
        m_out = jnp.where(lane_h == h, m_new, m_out)
    m_out_ref[...] = m_out


def _mlstm1(q, k, v, o, g, c_all, layer, n0, m0, norm_w):
    batch = q.shape[0]
    bb = _token_block(batch, 16)
    rows = lambda i: (i, 0)
    return pl.pallas_call(
        _mlstm1_kernel,
        grid=(batch // bb,),
        in_specs=[
            pl.BlockSpec((bb, MLSTM_W), rows),
            pl.BlockSpec((bb, MLSTM_W), rows),
            pl.BlockSpec((bb, MLSTM_W), rows),
            pl.BlockSpec((bb, MLSTM_W), rows),
            pl.BlockSpec((bb, 2 * HEADS), rows),
            pl.BlockSpec((None, bb, HEADS, HEAD_DIM, HEAD_DIM), lambda i: (layer, i, 0, 0, 0)),
            pl.BlockSpec((bb, MLSTM_W), rows),
            pl.BlockSpec((bb, HEADS), rows),
            pl.BlockSpec((1, MLSTM_W), lambda i: (0, 0)),
        ],
        out_specs=[
            pl.BlockSpec((bb, MLSTM_W), rows),
            pl.BlockSpec((bb, HEADS, HEAD_DIM, HEAD_DIM), lambda i: (i, 0, 0, 0)),
            pl.BlockSpec((bb, MLSTM_W), rows),
            pl.BlockSpec((bb, HEADS), rows),
        ],
        out_shape=[
            jax.ShapeDtypeStruct((batch, MLSTM_W), BF16),
            jax.ShapeDtypeStruct(c_all.shape[1:], F32),
            jax.ShapeDtypeStruct(n0.shape, F32),
            jax.ShapeDtypeStruct(m0.shape, F32),
        ],
        compiler_params=_params(("parallel",)),
        name="mlstm1",
    )(q, k, v, o, g, c_all, n0, m0, norm_w)


def _pool_kernel(u_ref, wp_ref, sc_ref, z_ref, buf_ref):
    t_len = u_ref.shape[0]
    t_idx = lax.broadcasted_iota(jnp.int32, (t_len, POOL_GROUP), 0)
    for g, w in enumerate(POOL_WINDOWS):
        cols = slice(g * POOL_GROUP, (g + 1) * POOL_GROUP)
        x = u_ref[:, cols]
        s = x
        k = 1
        while k < w:
            s = s + jnp.where(t_idx >= k, pltpu.roll(s, k, axis=0), 0.0)
            k *= 2
        cnt = jnp.minimum(t_idx + 1, w).astype(F32)
        r = s / cnt - x
        z = _dot(r.astype(BF16), wp_ref[g]) * sc_ref[:, cols]
        z_ref[:, cols] = z.astype(BF16)
    buf_ref[0] = u_ref[t_len - POOL_BUF:t_len, :]


def _pool(u, w_pool, scale, batch, t_len):
    n = u.shape[0]
    return pl.pallas_call(
        _pool_kernel,
        grid=(batch,),
        in_specs=[
            pl.BlockSpec((t_len, POOL_W), lambda b: (b, 0)),
            pl.BlockSpec(w_pool.shape, lambda b: (0, 0, 0)),
            pl.BlockSpec((1, POOL_W), lambda b: (0, 0)),
        ],
        out_specs=[
            pl.BlockSpec((t_len, POOL_W), lambda b: (b, 0)),
            pl.BlockSpec((1, POOL_BUF, POOL_W), lambda b: (b, 0, 0)),
        ],
        out_shape=[
            jax.ShapeDtypeStruct((n, POOL_W), BF16),
            jax.ShapeDtypeStruct((batch, POOL_BUF, POOL_W), F32),
        ],
        compiler_params=_params(("parallel",)),
        name="pool",
    )(u, w_pool, scale)


def _pool1_kernel(u_ref, buft_ref, wp_ref, sc_ref, z_ref, buft_out_ref):
    for g, w in enumerate(POOL_WINDOWS):
        cols = slice(g * POOL_GROUP, (g + 1) * POOL_GROUP)
        x = u_ref[:, cols]
        s = x
        for j in range(1, w):
            s = s + buft_ref[POOL_BUF - j, :, cols]
        r = s / float(w) - x
        z = _dot(r.astype(BF16), wp_ref[g]) * sc_ref[:, cols]
        z_ref[:, cols] = z.astype(BF16)
    for j in range(POOL_BUF - 1):
        buft_out_ref[j] = buft_ref[j + 1]
    buft_out_ref[POOL_BUF - 1] = u_ref[...]


def _pool1(u, buf_t, w_pool, scale):
    batch = u.shape[0]
    full2 = lambda i: (0, 0)
    full3 = lambda i: (0, 0, 0)
    return pl.pallas_call(
        _pool1_kernel,
        grid=(1,),
        in_specs=[
            pl.BlockSpec(u.shape, full2),
            pl.BlockSpec(buf_t.shape, full3),
            pl.BlockSpec(w_pool.shape, full3),
            pl.BlockSpec((1, POOL_W), full2),
        ],
        out_specs=[
            pl.BlockSpec((batch, POOL_W), full2),
            pl.BlockSpec(buf_t.shape, full3),
        ],
        out_shape=[
            jax.ShapeDtypeStruct((batch, POOL_W), BF16),
            jax.ShapeDtypeStruct(buf_t.shape, F32),
        ],
        compiler_params=_params(("arbitrary",)),
        name="pool1",
    )(u, buf_t, w_pool, scale)


def _mix_kernel(x_ref, hm_ref, zp_ref, wo_ref, g_ref, x1_ref, xnt_ref):
    mix = _dot(hm_ref[...], wo_ref[0:MLSTM_W, :]) + _dot(zp_ref[...], wo_ref[MLSTM_W:, :])
    x1 = x_ref[...] + mix
    x1_ref[...] = x1
    xnt_ref[...] = _rmsnorm(x1, g_ref[...]).T.astype(BF16)


def _mix(x, hm, zp, w_out, norm_g):
    n, d = x.shape
    tb = _token_block(n, 512)
    rows = lambda i: (i, 0)
    const = lambda i: (0, 0)
    return pl.pallas_call(
        _mix_kernel,
        grid=(n // tb,),
        in_specs=[
            pl.BlockSpec((tb, d), rows),
            pl.BlockSpec((tb, MLSTM_W), rows),
            pl.BlockSpec((tb, POOL_W), rows),
            pl.BlockSpec(w_out.shape, const),
            pl.BlockSpec((1, d), const),
        ],
        out_specs=[
            pl.BlockSpec((tb, d), rows),
            pl.BlockSpec((d, tb), lambda i: (0, i)),
        ],
        out_shape=[
            jax.ShapeDtypeStruct((n, d), F32),
            jax.ShapeDtypeStruct((d, n), BF16),
        ],
        compiler_params=_params(("parallel",)),
        name="mix",
    )(x, hm, zp, w_out, norm_g)


def _extract_max(vals, pos):
    m = vals[0]
    for v in vals[1:]:
        m = jnp.maximum(m, v)
    m = jnp.max(m, axis=0, keepdims=True)
    big = jnp.float32(1e9)
    idx = None
    for v, p in zip(vals, pos):
        cand = jnp.where(v == m, p, big)
        idx = cand if idx is None else jnp.minimum(idx, cand)
    idx = jnp.min(idx, axis=0, keepdims=True)
    return m, idx


def _top16(s):
    tb = s.shape[1]
    row = lax.broadcasted_iota(jnp.int32, s.shape, 0).astype(F32)
    krow = lax.broadcasted_iota(jnp.int32, (PEER_TOPK, tb), 0)
    rank = jnp.full(s.shape, UNRANKED, F32)
    sv = jnp.zeros((PEER_TOPK, tb), F32)
    for k in range(PEER_TOPK):
        m, idx = _extract_max([s], [row])
        hit = row == idx
        rank = jnp.where(hit, float(k), rank)
        s = jnp.where(hit, NEG_INF, s)
        sv = jnp.where(krow == k, m, sv)
    return sv, rank


def _candidate_groups(sv1, sv2):
    tb = sv1.shape[1]
    vals, pos = [], []
    r8 = lax.broadcasted_iota(jnp.int32, (8, tb), 0)
    vals.append(sv1[0:1, :] + sv2[8:16, :])
    pos.append((r8 + 8).astype(F32))
    vals.append(sv1[0:1, :] + sv2[0:8, :])
    pos.append(r8.astype(F32))
    for k1 in range(1, 8):
        lim = PEER_TOPK // (k1 + 1)
        v = sv1[k1:k1 + 1, :] + sv2[0:8, :]
        vals.append(jnp.where(r8 < lim, v, NEG_INF))
        pos.append((r8 + k1 * PEER_TOPK).astype(F32))
    vals.append(sv1[8:16, :] + sv2[0:1, :])
    pos.append(((r8 + 8) * PEER_TOPK).astype(F32))
    return vals, pos


def _sorting_pairs(n):
    pairs = []
    p = 1
    while p < n:
        k = p
        while k >= 1:
            for j in range(k % p, n - k, 2 * k):
                for i in range(min(k, n - j - k)):
                    if (i + j) // (2 * p) == (i + j + k) // (2 * p):
                        pairs.append((i + j, i + j + k))
            k //= 2
        p *= 2
    return pairs


def _sublane_allreduce(x, op):
    for d in (1, 2, 4):
        x = op(x, pltpu.roll(x, d, axis=0))
    return x


def _sorted_top16(groups):
    g = list(groups)
    for i, j in _sorting_pairs(len(g)):
        g[i], g[j] = jnp.maximum(g[i], g[j]), jnp.minimum(g[i], g[j])
    n = len(g)
    for d in (1, 2, 4):
        p = [pltpu.roll(x, d, axis=0) for x in g]
        g = [jnp.maximum(g[i], p[n - 1 - i]) for i in range(n)]
        stride = n // 2
        while stride >= 1:
            for i in range(n):
                if i & stride == 0:
                    g[i], g[i + stride] = (jnp.maximum(g[i], g[i + stride]),
                                           jnp.minimum(g[i], g[i + stride]))
            stride //= 2
    return g


def _rank_bits(x, sv):
    c8 = x < sv[7]
    t = jnp.where(c8, sv[11], sv[3])
    c4 = x < t
    t = jnp.where(c8, jnp.where(c4, sv[13], sv[9]), jnp.where(c4, sv[5], sv[1]))
    c2 = x < t
    t = jnp.where(c8,
                  jnp.where(c4, jnp.where(c2, sv[14], sv[12]), jnp.where(c2, sv[10], sv[8])),
                  jnp.where(c4, jnp.where(c2, sv[6], sv[4]), jnp.where(c2, sv[2], sv[0])))
    c1 = x < t
    return (c8, c4, c2, c1), x < sv[15]


def _select16(bits, rows):
    c8, c4, c2, c1 = bits
    lvl = [jnp.where(c1, rows[2 * i + 1], rows[2 * i]) for i in range(8)]
    lvl = [jnp.where(c2, lvl[2 * i + 1], lvl[2 * i]) for i in range(4)]
    lvl = [jnp.where(c4, lvl[2 * i + 1], lvl[2 * i]) for i in range(2)]
    return jnp.where(c8, lvl[1], lvl[0])


def _route_fast(s1, s2):
    tb = s1.shape[1]
    r8 = lax.broadcasted_iota(jnp.int32, (8, tb), 0)
    n_grp = PEER_NKEYS // 8
    g1 = [s1[v * 8:(v + 1) * 8, :] for v in range(n_grp)]
    g2 = [s2[v * 8:(v + 1) * 8, :] for v in range(n_grp)]
    sv1 = _sorted_top16(g1)
    sv2 = _sorted_top16(g2)

    bad = jnp.zeros((8, tb), F32)
    for sv, grp in ((sv1, g1), (sv2, g2)):
        for k in range(PEER_TOPK - 1):
            bad = jnp.where(sv[k] == sv[k + 1], 1.0, bad)
        n_in = jnp.zeros((8, tb), F32)
        for x in grp:
            n_in = n_in + jnp.where(x >= sv[PEER_TOPK - 1], 1.0, 0.0)
        n_in = _sublane_allreduce(n_in, jnp.add)
        bad = jnp.where(n_in != float(PEER_TOPK), 1.0, bad)

    def by_sublane(rows):
        out = jnp.zeros((8, tb), F32)
        for k, row in enumerate(rows):
            out = jnp.where(r8 == k, row, out)
        return out
    a2_lo, a2_hi, a1_hi = by_sublane(sv2[0:8]), by_sublane(sv2[8:16]), by_sublane(sv1[8:16])
    orig = [sv1[0] + a2_hi, sv1[0] + a2_lo]
    for k1 in range(1, 8):
        orig.append(jnp.where(r8 < PEER_TOPK // (k1 + 1), sv1[k1] + a2_lo, NEG_INF))
    orig.append(a1_hi + sv2[0])
    vals = list(orig)
    prev = None
    for _ in range(PEER_TOPK):
        m = vals[0]
        for v in vals[1:]:
            m = jnp.maximum(m, v)
        m = _sublane_allreduce(m, jnp.maximum)
        vals = [jnp.where(v == m, NEG_INF, v) for v in vals]
        if prev is not None:
            bad = jnp.where(m == prev, 1.0, bad)
        prev = m
    sel = [o >= prev for o in orig]
    cmax = sv1[0] + sv2[0]
    z = jnp.zeros((8, tb), F32)
    for o, sl in zip(orig, sel):
        z = z + jnp.where(sl, jnp.exp(o - cmax), 0.0)
    z = _sublane_allreduce(z, jnp.add)
    ones = [jnp.where(sl, 1.0, 0.0) for sl in sel]
    cnt = [_sublane_allreduce(ones[0] + ones[1], jnp.add)]
    cnt += [_sublane_allreduce(ones[k1 + 1], jnp.add) for k1 in range(1, 8)]
    cnt += [ones[9][j:j + 1, :] for j in range(8)]
    total = cnt[0]
    for c in cnt[1:]:
        total = total + c
    bad = jnp.where(total != float(PEER_TOPK), 1.0, bad)

    cnt1, r2 = [], []
    weights = (8.0, 4.0, 2.0, 1.0)
    for x in g1:
        bits, below = _rank_bits(x, sv1)
        cnt1.append(jnp.where(below, 0.0, _select16(bits, cnt)))
    for x in g2:
        bits, below = _rank_bits(x, sv2)
        rank = jnp.zeros((8, tb), F32)
        for b, wgt in zip(bits, weights):
            rank = rank + jnp.where(b, wgt, 0.0)
        r2.append(jnp.where(below, UNRANKED, rank))
    return cnt1, r2, sv1[0][0:1, :], sv2[0][0:1, :], z[0:1, :], bad


def _route_exact(s1, s2):
    sv1, r1 = _top16(s1)
    sv2, r2 = _top16(s2)

    vals, pos = _candidate_groups(sv1, sv2)
    orig = list(vals)
    sel = [jnp.zeros(v.shape, F32) for v in vals]
    for _ in range(PEER_TOPK):
        _, idx = _extract_max(vals, pos)
        for i in range(len(vals)):
            hit = pos[i] == idx
            sel[i] = jnp.where(hit, 1.0, sel[i])
            vals[i] = jnp.where(hit, NEG_INF, vals[i])

    cmax = orig[1][0:1, :]
    z = None
    for o, sl in zip(orig, sel):
        part = jnp.sum(jnp.where(sl > 0.0, jnp.exp(o - cmax), 0.0), axis=0, keepdims=True)
        z = part if z is None else z + part
    cnt = [jnp.sum(sel[0] + sel[1], axis=0, keepdims=True)]
    cnt += [jnp.sum(sel[k1 + 1], axis=0, keepdims=True) for k1 in range(1, 8)]
    cnt += [sel[9][j:j + 1, :] for j in range(8)]

    cnt1 = jnp.zeros(r1.shape, F32)
    for k1 in range(PEER_TOPK):
        cnt1 = jnp.where(r1 == float(k1), cnt[k1], cnt1)
    return cnt1, r2, sv1[0:1, :], sv2[0:1, :], z


ROUTE_HEADS_PER_STEP = 4


def _route_kernel(xnt_ref, wqt_ref, k1_ref, k2_ref, cnt1_ref, g1_ref, r2_ref, e2_ref, s_ref):
    qt_all = _dot(wqt_ref[...], xnt_ref[...]).astype(BF16)

    def emit_gates(hh, s1, s2, max1, max2, z):
        g1_ref[hh] = jnp.exp(s1 - max1) * (0.5 / z)
        e2_ref[hh] = jnp.exp(s2 - max2).astype(BF16)

    ties = []
    for hh in range(ROUTE_HEADS_PER_STEP):
        qt = qt_all[hh * 2 * PEER_HALF:(hh + 1) * 2 * PEER_HALF, :]
        s1 = _dot(k1_ref[hh], qt[0:PEER_HALF, :])
        s2 = _dot(k2_ref[hh], qt[PEER_HALF:, :])
        s_ref[hh, 0] = s1
        s_ref[hh, 1] = s2
        cnt1, r2, max1, max2, z, tie = _route_fast(s1, s2)
        for v in range(PEER_NKEYS // 8):
            cnt1_ref[hh, v * 8:(v + 1) * 8, :] = cnt1[v]
            r2_ref[hh, v * 8:(v + 1) * 8, :] = r2[v].astype(BF16)
        emit_gates(hh, s1, s2, max1, max2, z)
        ties.append(jnp.max(tie))

    for hh in range(ROUTE_HEADS_PER_STEP):
        @pl.when(ties[hh] > 0.0)
        def _(hh=hh):
            s1, s2 = s_ref[hh, 0], s_ref[hh, 1]
            cnt1, r2, max1, max2, z = _route_exact(s1, s2)
            cnt1_ref[hh] = cnt1
            r2_ref[hh] = r2.astype(BF16)
            emit_gates(hh, s1, s2, max1, max2, z)


def _route(xnt, wqt, keys):
    d, n = xnt.shape
    tb = _token_block(n, 256)
    hps = ROUTE_HEADS_PER_STEP
    per_head = pl.BlockSpec((hps, PEER_NKEYS, tb), lambda i, h: (h, 0, i))
    shape = lambda dt: jax.ShapeDtypeStruct((PEER_HEADS, PEER_NKEYS, n), dt)
    return pl.pallas_call(
        _route_kernel,
        grid=(n // tb, PEER_HEADS // hps),
        in_specs=[
            pl.BlockSpec((d, tb), lambda i, h: (0, i)),
            pl.BlockSpec((hps * 2 * PEER_HALF, d), lambda i, h: (h, 0)),
            pl.BlockSpec((hps, PEER_NKEYS, PEER_HALF), lambda i, h: (h, 0, 0)),
            pl.BlockSpec((hps, PEER_NKEYS, PEER_HALF), lambda i, h: (h + PEER_HEADS // hps, 0, 0)),
        ],
        out_specs=[per_head] * 4,
        out_shape=[shape(F32), shape(F32), shape(BF16), shape(BF16)],
        scratch_shapes=[pltpu.VMEM((hps, 2, PEER_NKEYS, tb), F32)],
        compiler_params=_params(("parallel", "parallel")),
        name="route",
    )(xnt, wqt, keys, keys)


I1_PER_TILE = 16
BF16_TILE = (8, 2 * LANE)
EXPERT_TILE = I1_PER_TILE * PEER_NKEYS


def _experts_kernel(xnt_ref, cnt1_ref, g1_ref, r2_ref, e2_ref, u_ref, v_ref, x1_ref,
                    x2_ref, acc_ref, act_ref, wa_ref):
    e = pl.program_id(1)
    tb = xnt_ref.shape[1]

    @pl.when(e == 0)
    def _():
        acc_ref[...] = jnp.zeros_like(acc_ref)

    a = _dot(u_ref[...], xnt_ref[...]).astype(BF16)
    act_ref[...] = a * (1.0 + jnp.tanh(a * (GELU_C0 + GELU_C1 * (a * a))))
    sub, width = BF16_TILE[0], min(BF16_TILE[1], tb)
    assert I1_PER_TILE % sub == 0
    i1_rows = pl.ds(pl.multiple_of(e * I1_PER_TILE, I1_PER_TILE), I1_PER_TILE)
    for lt in range(tb // width):
        lanes = pl.ds(lt * width, width)
        cnt_tile = [cnt1_ref[h, i1_rows, lanes] for h in range(PEER_HEADS)]
        g1_tile = [g1_ref[h, i1_rows, lanes] for h in range(PEER_HEADS)]
        for j in range(I1_PER_TILE):
            cnt_b = [jnp.broadcast_to(t[j:j + 1, :], (sub, width)).astype(BF16) for t in cnt_tile]
            g1_b = [jnp.broadcast_to(t[j:j + 1, :], (sub, width)).astype(BF16) for t in g1_tile]
            for rb in range(PEER_NKEYS // sub):
                krows = pl.ds(rb * sub, sub)
                w = None
                for h in range(PEER_HEADS):
                    mask = r2_ref[h, krows, lanes] < cnt_b[h]
                    term = jnp.where(mask, e2_ref[h, krows, lanes] * g1_b[h], jnp.zeros((), BF16))
                    w = term if w is None else w + term
                erows = pl.ds(j * PEER_NKEYS + rb * sub, sub)
                wa_ref[erows, lanes] = w * act_ref[erows, lanes]
    acc_ref[...] += lax.dot_general(v_ref[...], wa_ref[...], (((0,), (0,)), ((), ())),
                                    preferred_element_type=F32)

    @pl.when(e == pl.num_programs(1) - 1)
    def _():
        x2_ref[...] = x1_ref[...] + acc_ref[...].T


def _experts(xnt, cnt1, g1, r2, e2, u_tabs, v_tabs, layer, x1):
    d, n = xnt.shape
    n_exp = u_tabs.shape[1]
    tb = _token_block(n, 512)
    assert tb % LANE == 0 and n_exp % EXPERT_TILE == 0
    per_head = pl.BlockSpec((PEER_HEADS, PEER_NKEYS, tb), lambda i, e: (0, 0, i))
    tile_buf = pltpu.VMEM((EXPERT_TILE, tb), BF16)
    return pl.pallas_call(
        _experts_kernel,
        grid=(n // tb, n_exp // EXPERT_TILE),
        in_specs=[
            pl.BlockSpec((d, tb), lambda i, e: (0, i)),
            per_head, per_head, per_head, per_head,
            pl.BlockSpec((None, EXPERT_TILE, d), lambda i, e: (layer, e, 0)),
            pl.BlockSpec((None, EXPERT_TILE, d), lambda i, e: (layer, e, 0)),
            pl.BlockSpec((tb, d), lambda i, e: (i, 0)),
        ],
        out_specs=pl.BlockSpec((tb, d), lambda i, e: (i, 0)),
        out_shape=jax.ShapeDtypeStruct((n, d), F32),
        scratch_shapes=[pltpu.VMEM((d, tb), F32), tile_buf, tile_buf],
        compiler_params=_params(("parallel", "arbitrary")),
        name="experts",
    )(xnt, cnt1, g1, r2, e2, u_tabs, v_tabs, x1)


def _ple_kernel(x_ref, p_ref, wg_ref, wp_ref, nf_ref, y_ref, *, final_norm):
    x = x_ref[...]
    gate = _sigmoid(_dot(x.astype(BF16), wg_ref[...]))
    y = x + gate * _dot(p_ref[...].astype(BF16), wp_ref[...])
    if final_norm:
        y = _rmsnorm(y, nf_ref[...])
    y_ref[...] = y


def _ple(x, p_all, layer, w_gate, w_ple, norm_f, final_norm):
    n, d = x.shape
    tb = _token_block(n, 512)
    rows = lambda i: (i, 0)
    const = lambda i: (0, 0)
    return pl.pallas_call(
        functools.partial(_ple_kernel, final_norm=final_norm),
        grid=(n // tb,),
        in_specs=[
            pl.BlockSpec((tb, d), rows),
            pl.BlockSpec((None, tb, p_all.shape[2]), lambda i: (layer, i, 0)),
            pl.BlockSpec(w_gate.shape, const),
            pl.BlockSpec(w_ple.shape, const),
            pl.BlockSpec((1, d), const),
        ],
        out_specs=pl.BlockSpec((tb, d), rows),
        out_shape=jax.ShapeDtypeStruct((n, d), F32),
        compiler_params=_params(("parallel",)),
        name="ple",
    )(x, p_all, w_gate, w_ple, norm_f)


def _prep_layer_weights(l, w_in, b_gate, mlstm_norm, w_pool, pool_scale, w_out, norm1, norm2,
                        peer_wq, peer_keys, peer_u, peer_v, w_ple, w_gate):
    w = MLSTM_W
    wi = w_in[l]
    d = wi.shape[0]
    wa = jnp.concatenate([wi[:, 0:w], wi[:, 2 * w:4 * w + POOL_W]], axis=1).astype(BF16)
    wkt = wi[:, w:2 * w].T.astype(BF16)
    wgt = wi[:, 4 * w + POOL_W:].T.astype(BF16)
    return dict(
        wa=wa, wkt=wkt, wgt=wgt,
        bg=b_gate[l].reshape(2 * HEADS, 1).astype(F32),
        mlstm_norm=mlstm_norm[l].reshape(1, w),
        w_pool=w_pool[l].astype(BF16),
        pool_scale=pool_scale[l].reshape(1, POOL_W),
        w_out=w_out[l].astype(BF16),
        norm1=norm1[l].reshape(1, d), norm2=norm2[l].reshape(1, d),
        wqt=peer_wq[l].T.astype(BF16),
        keys=peer_keys[l].reshape(2 * PEER_HEADS, PEER_NKEYS, PEER_HALF).astype(BF16),
        w_ple=w_ple[l].astype(BF16), w_gate=w_gate[l].astype(BF16),
        layer=l,
    )


def _ffn_and_embed(x1, xnt, p, lw, norm_f, final_norm):
    cnt1, g1, r2, e2 = _route(xnt, lw["wqt"], lw["keys"])
    x2 = _experts(xnt, cnt1, g1, r2, e2, lw["u_tabs"], lw["v_tabs"], lw["layer"], x1)
    return _ple(x2, p, lw["layer"], lw["w_gate"], lw["w_ple"], norm_f, final_norm)


def _prompt_layer(x, p, lw, norm_f, final_norm, batch, t_len):
    q, kt, v, o, u, gt = _proj(x, lw["norm1"], lw["wa"], lw["wkt"], lw["wgt"], lw["bg"])
    gates3 = gt.reshape(2 * HEADS, (batch * t_len) // CHUNK, CHUNK)
    hm, cext, m8 = _mlstm(q, kt, v, o, gates3, lw["mlstm_norm"], batch, t_len)
    zp, buf = _pool(u, lw["w_pool"], lw["pool_scale"], batch, t_len)
    x1, xnt = _mix(x, hm, zp, lw["w_out"], lw["norm2"])
    y = _ffn_and_embed(x1, xnt, p, lw, norm_f, final_norm)
    return y, cext[..., :HEAD_DIM], cext[..., HEAD_DIM], m8[..., 0, 0], buf


def _sample_layer(x, p, c_all, n0, m0, buf0, lw, norm_f, final_norm):
    q, kt, v, o, u, gt = _proj(x, lw["norm1"], lw["wa"], lw["wkt"], lw["wgt"], lw["bg"])
    hm, c1, n1, m1 = _mlstm1(q, kt.T, v, o, gt.T, c_all, lw["layer"], n0.reshape(-1, MLSTM_W), m0,
                             lw["mlstm_norm"])
    n1 = n1.reshape(n0.shape)
    zp, buf_t = _pool1(u, jnp.swapaxes(buf0, 0, 1), lw["w_pool"], lw["pool_scale"])
    x1, xnt = _mix(x, hm, zp, lw["w_out"], lw["norm2"])
    y = _ffn_and_embed(x1, xnt, p, lw, norm_f, final_norm)
    return y, c1, n1, m1, jnp.swapaxes(buf_t, 0, 1)


def kernel(x_prompt, x_sample, p_prompt, p_sample, state_C, state_n, state_m, state_pool,
           w_in, b_gate, mlstm_norm, w_pool, pool_scale, w_out, norm1, norm2,
           peer_wq, peer_keys, peer_u, peer_v, w_ple, w_gate, norm_f):
    depth = w_in.shape[0]
    batch, t_len, d = x_prompt.shape
    dec_batch, dec_len, _ = x_sample.shape
    assert dec_len == 1 and t_len % CHUNK == 0
    nf = norm_f.reshape(1, d)

    xp = x_prompt.reshape(batch * t_len, d)
    xs = x_sample.reshape(dec_batch, d)
    outs_p, outs_s = [], []
    u_tabs = peer_u.astype(BF16)
    v_tabs = peer_v.astype(BF16)
    for l in range(depth):
        lw = _prep_layer_weights(l, w_in, b_gate, mlstm_norm, w_pool, pool_scale, w_out, norm1,
                                 norm2, peer_wq, peer_keys, peer_u, peer_v, w_ple, w_gate)
        lw.update(u_tabs=u_tabs, v_tabs=v_tabs)
        final = l == depth - 1
        pp = p_prompt.reshape(depth, batch * t_len, -1)
        ps = p_sample.reshape(depth, dec_batch, -1)
        xp, c_p, n_p, m_p, buf_p = _prompt_layer(xp, pp, lw, nf, final, batch, t_len)
        xs, c_s, n_s, m_s, buf_s = _sample_layer(xs, ps, state_C, state_n[l], state_m[l],
                                                 state_pool[l], lw, nf, final)
        outs_p.append((c_p, n_p, m_p, buf_p))
        outs_s.append((c_s, n_s, m_s, buf_s))

    stack = lambda outs, i: jnp.stack([o[i] for o in outs])
    return (xp.reshape(batch, t_len, d), xs.reshape(dec_batch, dec_len, d),
            stack(outs_p, 0), stack(outs_p, 1), stack(outs_p, 2), stack(outs_p, 3),
            stack(outs_s, 0), stack(outs_s, 1), stack(outs_s, 2), stack(outs_s, 3))
```

```python
import functools
import math

import jax
import jax.numpy as jnp
from jax import lax
from jax.experimental import pallas as pl
from jax.experimental.pallas import tpu as pltpu

F32 = jnp.float32
BF16 = jnp.bfloat16
EPS = 1e-6
NEG_INF = float("-inf")

LANE = 128
V7X_VMEM_LIMIT = 56 * 1024 * 1024

HEADS = 4
HEAD_DIM = 128
MLSTM_W = HEADS * HEAD_DIM
POOL_WINDOWS = (2, 4, 8, 16)
POOL_GROUP = 128
POOL_W = POOL_GROUP * len(POOL_WINDOWS)
POOL_BUF = max(POOL_WINDOWS) - 1
CHUNK = 128
PEER_HEADS = 8
PEER_NKEYS = 128
PEER_HALF = 128
PEER_TOPK = 16
UNRANKED = 127.0
GELU_C0 = math.sqrt(2.0 / math.pi)
GELU_C1 = 0.044715 * GELU_C0


def _params(semantics):
    return pltpu.CompilerParams(dimension_semantics=semantics, vmem_limit_bytes=V7X_VMEM_LIMIT)


def _token_block(n, want):
    tb = min(n, want)
    assert n % tb == 0
    return tb


def _rmsnorm(x, g):
    return x * lax.rsqrt(jnp.mean(x * x, axis=-1, keepdims=True) + EPS) * g


def _log_sigmoid(x):
    return jnp.minimum(x, 0.0) - jnp.log(1.0 + jnp.exp(-jnp.abs(x)))


def _sigmoid(x):
    return 1.0 / (1.0 + jnp.exp(-x))


def _dot(a, b):
    return jnp.dot(a, b, preferred_element_type=F32)


def _dot_nt(a, b):
    return lax.dot_general(a, b, (((1,), (1,)), ((), ())), preferred_element_type=F32)


def _proj_kernel(x_ref, g_ref, wa_ref, wkt_ref, wgt_ref, bg_ref,
                 q_ref, kt_ref, v_ref, o_ref, u_ref, gt_ref):
    hn = _rmsnorm(x_ref[...], g_ref[...]).astype(BF16)
    pa = _dot(hn, wa_ref[...])
    w = MLSTM_W
    q_ref[...] = pa[:, 0:w].astype(BF16)
    v_ref[...] = pa[:, w:2 * w].astype(BF16)
    o_ref[...] = _sigmoid(pa[:, 2 * w:3 * w])
    u_ref[...] = pa[:, 3 * w:4 * w]
    kt = _dot_nt(wkt_ref[...], hn) * (HEAD_DIM ** -0.5)
    kt_ref[...] = kt.astype(BF16)
    gt_ref[...] = _dot_nt(wgt_ref[...], hn) + bg_ref[...]


def _proj(x, norm_g, wa, wkt, wgt, bg):
    n, d = x.shape
    tb = _token_block(n, 512)
    grid = (n // tb,)
    const = lambda i: (0, 0)
    return pl.pallas_call(
        _proj_kernel,
        grid=grid,
        in_specs=[
            pl.BlockSpec((tb, d), lambda i: (i, 0)),
            pl.BlockSpec((1, d), const),
            pl.BlockSpec(wa.shape, const),
            pl.BlockSpec(wkt.shape, const),
            pl.BlockSpec(wgt.shape, const),
            pl.BlockSpec(bg.shape, const),
        ],
        out_specs=[
            pl.BlockSpec((tb, MLSTM_W), lambda i: (i, 0)),
            pl.BlockSpec((MLSTM_W, tb), lambda i: (0, i)),
            pl.BlockSpec((tb, MLSTM_W), lambda i: (i, 0)),
            pl.BlockSpec((tb, MLSTM_W), lambda i: (i, 0)),
            pl.BlockSpec((tb, POOL_W), lambda i: (i, 0)),
            pl.BlockSpec((2 * HEADS, tb), lambda i: (0, i)),
        ],
        out_shape=[
            jax.ShapeDtypeStruct((n, MLSTM_W), BF16),
            jax.ShapeDtypeStruct((MLSTM_W, n), BF16),
            jax.ShapeDtypeStruct((n, MLSTM_W), BF16),
            jax.ShapeDtypeStruct((n, MLSTM_W), F32),
            jax.ShapeDtypeStruct((n, POOL_W), F32),
            jax.ShapeDtypeStruct((2 * HEADS, n), F32),
        ],
        compiler_params=_params(("parallel",)),
        name="proj",
    )(x, norm_g, wa, wkt, wgt, bg)


def _lane_cumsum(x):
    lane = lax.broadcasted_iota(jnp.int32, x.shape, 1)
    shift = 1
    while shift < x.shape[1]:
        x = x + jnp.where(lane >= shift, pltpu.roll(x, shift, axis=1), 0.0)
        shift *= 2
    return x


def _mlstm_kernel(q_ref, kt_ref, v_ref, o_ref, ig_ref, lf_ref, nw_ref,
                  hm_ref, cext_out_ref, m_out_ref, cext_ref):
    t_len = q_ref.shape[0]
    n_chunks = t_len // CHUNK
    ig2 = ig_ref[0]
    lf2 = _log_sigmoid(lf_ref[0])
    b2 = _lane_cumsum(lf2)
    a2 = ig2 - b2

    t_idx = lax.broadcasted_iota(jnp.int32, (CHUNK, CHUNK), 0)
    s_idx = lax.broadcasted_iota(jnp.int32, (CHUNK, CHUNK), 1)
    causal = s_idx <= t_idx
    ones_col = (lax.broadcasted_iota(jnp.int32, (CHUNK, HEAD_DIM), 1) == 0).astype(BF16)
    nw = nw_ref[...]

    cext_ref[...] = jnp.zeros_like(cext_ref)
    m = jnp.zeros((1, 1), F32)
    for c in range(n_chunks):
        rows = pl.ds(c * CHUNK, CHUNK)
        a_row = a2[c:c + 1, :]
        lf_row = lf2[c:c + 1, :]
        b_last = b2[c:c + 1, CHUNK - 1:CHUNK]
        b_col = jnp.sum(jnp.where(causal, lf_row, 0.0), axis=1, keepdims=True)
        amax_col = jnp.max(jnp.where(causal, a_row, NEG_INF), axis=1, keepdims=True)
        m_col = jnp.maximum(amax_col, m)
        decay_mat = jnp.where(causal, jnp.exp(a_row - m_col), 0.0)

        qc = q_ref[rows, :]
        ktc = kt_ref[:, rows]
        v_ext = jnp.concatenate([v_ref[rows, :], ones_col], axis=1)
        s = _dot(qc, ktc) * decay_mat
        intra = _dot(s.astype(BF16), v_ext)
        inter = _dot(qc, cext_ref[...].astype(BF16))
        tot = intra + jnp.exp(m - m_col) * inter
        num = tot[:, :HEAD_DIM]
        den = tot[:, HEAD_DIM:HEAD_DIM + 1]
        h = num / jnp.maximum(jnp.abs(den), jnp.exp(-(b_col + m_col)))
        h = h * lax.rsqrt(jnp.mean(h * h, axis=1, keepdims=True) + EPS)
        hm_ref[rows, :] = (h * nw * o_ref[rows, :]).astype(BF16)

        m_new = jnp.maximum(b_last + m, jnp.max(a_row, axis=1, keepdims=True) + b_last)
        ws_row = jnp.exp(a_row + (b_last - m_new))
        kws = (ktc.astype(F32) * ws_row).astype(BF16)
        cext_ref[...] = jnp.exp(b_last + m - m_new) * cext_ref[...] + _dot(kws, v_ext)
        m = m_new

    cext_out_ref[0, 0] = cext_ref[...]
    m_out_ref[0, 0] = jnp.broadcast_to(m, m_out_ref.shape[2:])


def _mlstm(q, kt, v, o, gates3, norm_w, batch, t_len):
    n = q.shape[0]
    n_chunks = t_len // CHUNK
    seq = lambda b, h: (b, h)
    return pl.pallas_call(
        _mlstm_kernel,
        grid=(batch, HEADS),
        in_specs=[
            pl.BlockSpec((t_len, HEAD_DIM), seq),
            pl.BlockSpec((HEAD_DIM, t_len), lambda b, h: (h, b)),
            pl.BlockSpec((t_len, HEAD_DIM), seq),
            pl.BlockSpec((t_len, HEAD_DIM), seq),
            pl.BlockSpec((1, n_chunks, CHUNK), lambda b, h: (h, b, 0)),
            pl.BlockSpec((1, n_chunks, CHUNK), lambda b, h: (h + HEADS, b, 0)),
            pl.BlockSpec((1, HEAD_DIM), lambda b, h: (0, h)),
        ],
        out_specs=[
            pl.BlockSpec((t_len, HEAD_DIM), seq),
            pl.BlockSpec((1, 1, HEAD_DIM, 2 * HEAD_DIM), lambda b, h: (b, h, 0, 0)),
            pl.BlockSpec((1, 1, 8, LANE), lambda b, h: (b, h, 0, 0)),
        ],
        out_shape=[
            jax.ShapeDtypeStruct((n, MLSTM_W), BF16),
            jax.ShapeDtypeStruct((batch, HEADS, HEAD_DIM, 2 * HEAD_DIM), F32),
            jax.ShapeDtypeStruct((batch, HEADS, 8, LANE), F32),
        ],
        scratch_shapes=[pltpu.VMEM((HEAD_DIM, 2 * HEAD_DIM), F32)],
        compiler_params=_params(("parallel", "parallel")),
        name="mlstm",
    )(q, kt, v, o, gates3, gates3, norm_w)


def _mlstm1_kernel(q_ref, k_ref, v_ref, o_ref, g_ref, c_ref, n_ref, m_ref, nw_ref,
                   hm_ref, c_out_ref, n_out_ref, m_out_ref):
    bb = q_ref.shape[0]
    row = lax.broadcasted_iota(jnp.int32, (bb, HEAD_DIM), 0)
    eye = (lax.broadcasted_iota(jnp.int32, (HEAD_DIM, HEAD_DIM), 0)
           == lax.broadcasted_iota(jnp.int32, (HEAD_DIM, HEAD_DIM), 1))
    lane_h = lax.broadcasted_iota(jnp.int32, (bb, HEADS), 1)
    g = g_ref[...]
    m_all = m_ref[...]
    m_out = jnp.zeros((bb, HEADS), F32)
    for h in range(HEADS):
        cols = slice(h * HEAD_DIM, (h + 1) * HEAD_DIM)
        qb = q_ref[:, cols]
        qf = qb.astype(F32)
        kf = k_ref[:, cols].astype(F32)
        vf = v_ref[:, cols].astype(F32)
        ig = g[:, h:h + 1]
        lf = _log_sigmoid(g[:, HEADS + h:HEADS + h + 1])
        m_old = m_all[:, h:h + 1]
        n_old = n_ref[:, cols]
        m_new = jnp.maximum(ig, lf + m_old)
        w_in = jnp.exp(ig - m_new)
        w_st = jnp.exp(lf + m_old - m_new)
        qc = jnp.zeros((bb, HEAD_DIM), F32)
        for j in range(bb):
            c_old = c_ref[j, h]
            res = _dot(qb, c_old.astype(BF16))
            qc = jnp.where(row == j, res, qc)
            k_col = jnp.sum(jnp.where(eye, kf[j:j + 1, :], 0.0), axis=1, keepdims=True)
            c_out_ref[j, h] = w_st[j:j + 1, :] * c_old + k_col * (w_in[j:j + 1, :] * vf[j:j + 1, :])
        s = jnp.sum(qf * kf, axis=1, keepdims=True) * w_in
        num = s * vf + w_st * qc
        den = s + w_st * jnp.sum(qf * n_old, axis=1, keepdims=True)
        hh = num / jnp.maximum(jnp.abs(den), jnp.exp(-m_new))
        hh = hh * lax.rsqrt(jnp.mean(hh * hh, axis=1, keepdims=True) + EPS)
        hm_ref[:, cols] = (hh * nw_ref[:, cols] * o_ref[:, cols]).astype(BF16)
        n_out_ref[:, cols] = w_st * n_old + w_in * kf
        m_out = jnp.where(lane_h == h, m_new, m_out)
    m_out_ref[...] = m_out


def _mlstm1(q, k, v, o, g, c_all, layer, n0, m0, norm_w):
    batch = q.shape[0]
    bb = _token_block(batch, 16)
    rows = lambda i: (i, 0)
    return pl.pallas_call(
        _mlstm1_kernel,
        grid=(batch // bb,),
        in_specs=[
            pl.BlockSpec((bb, MLSTM_W), rows),
            pl.BlockSpec((bb, MLSTM_W), rows),
            pl.BlockSpec((bb, MLSTM_W), rows),
            pl.BlockSpec((bb, MLSTM_W), rows),
            pl.BlockSpec((bb, 2 * HEADS), rows),
            pl.BlockSpec((None, bb, HEADS, HEAD_DIM, HEAD_DIM), lambda i: (layer, i, 0, 0, 0)),
            pl.BlockSpec((bb, MLSTM_W), rows),
            pl.BlockSpec((bb, HEADS), rows),
            pl.BlockSpec((1, MLSTM_W), lambda i: (0, 0)),
        ],
        out_specs=[
            pl.BlockSpec((bb, MLSTM_W), rows),
            pl.BlockSpec((bb, HEADS, HEAD_DIM, HEAD_DIM), lambda i: (i, 0, 0, 0)),
            pl.BlockSpec((bb, MLSTM_W), rows),
            pl.BlockSpec((bb, HEADS), rows),
        ],
        out_shape=[
            jax.ShapeDtypeStruct((batch, MLSTM_W), BF16),
            jax.ShapeDtypeStruct(c_all.shape[1:], F32),
            jax.ShapeDtypeStruct(n0.shape, F32),
            jax.ShapeDtypeStruct(m0.shape, F32),
        ],
        compiler_params=_params(("parallel",)),
        name="mlstm1",
    )(q, k, v, o, g, c_all, n0, m0, norm_w)


def _pool_kernel(u_ref, wp_ref, sc_ref, z_ref, buf_ref):
    t_len = u_ref.shape[0]
    t_idx = lax.broadcasted_iota(jnp.int32, (t_len, POOL_GROUP), 0)
    for g, w in enumerate(POOL_WINDOWS):
        cols = slice(g * POOL_GROUP, (g + 1) * POOL_GROUP)
        x = u_ref[:, cols]
        s = x
        k = 1
        while k < w:
            s = s + jnp.where(t_idx >= k, pltpu.roll(s, k, axis=0), 0.0)
            k *= 2
        cnt = jnp.minimum(t_idx + 1, w).astype(F32)
        r = s / cnt - x
        z = _dot(r.astype(BF16), wp_ref[g]) * sc_ref[:, cols]
        z_ref[:, cols] = z.astype(BF16)
    buf_ref[0] = u_ref[t_len - POOL_BUF:t_len, :]


def _pool(u, w_pool, scale, batch, t_len):
    n = u.shape[0]
    return pl.pallas_call(
        _pool_kernel,
        grid=(batch,),
        in_specs=[
            pl.BlockSpec((t_len, POOL_W), lambda b: (b, 0)),
            pl.BlockSpec(w_pool.shape, lambda b: (0, 0, 0)),
            pl.BlockSpec((1, POOL_W), lambda b: (0, 0)),
        ],
        out_specs=[
            pl.BlockSpec((t_len, POOL_W), lambda b: (b, 0)),
            pl.BlockSpec((1, POOL_BUF, POOL_W), lambda b: (b, 0, 0)),
        ],
        out_shape=[
            jax.ShapeDtypeStruct((n, POOL_W), BF16),
            jax.ShapeDtypeStruct((batch, POOL_BUF, POOL_W), F32),
        ],
        compiler_params=_params(("parallel",)),
        name="pool",
    )(u, w_pool, scale)


def _pool1_kernel(u_ref, buft_ref, wp_ref, sc_ref, z_ref, buft_out_ref):
    for g, w in enumerate(POOL_WINDOWS):
        cols = slice(g * POOL_GROUP, (g + 1) * POOL_GROUP)
        x = u_ref[:, cols]
        s = x
        for j in range(1, w):
            s = s + buft_ref[POOL_BUF - j, :, cols]
        r = s / float(w) - x
        z = _dot(r.astype(BF16), wp_ref[g]) * sc_ref[:, cols]
        z_ref[:, cols] = z.astype(BF16)
    for j in range(POOL_BUF - 1):
        buft_out_ref[j] = buft_ref[j + 1]
    buft_out_ref[POOL_BUF - 1] = u_ref[...]


def _pool1(u, buf_t, w_pool, scale):
    batch = u.shape[0]
    full2 = lambda i: (0, 0)
    full3 = lambda i: (0, 0, 0)
    return pl.pallas_call(
        _pool1_kernel,
        grid=(1,),
        in_specs=[
            pl.BlockSpec(u.shape, full2),
            pl.BlockSpec(buf_t.shape, full3),
            pl.BlockSpec(w_pool.shape, full3),
            pl.BlockSpec((1, POOL_W), full2),
        ],
        out_specs=[
            pl.BlockSpec((batch, POOL_W), full2),
            pl.BlockSpec(buf_t.shape, full3),
        ],
        out_shape=[
            jax.ShapeDtypeStruct((batch, POOL_W), BF16),
            jax.ShapeDtypeStruct(buf_t.shape, F32),
        ],
        compiler_params=_params(("arbitrary",)),
        name="pool1",
    )(u, buf_t, w_pool, scale)


def _mix_kernel(x_ref, hm_ref, zp_ref, wo_ref, g_ref, x1_ref, xnt_ref):
    mix = _dot(hm_ref[...], wo_ref[0:MLSTM_W, :]) + _dot(zp_ref[...], wo_ref[MLSTM_W:, :])
    x1 = x_ref[...] + mix
    x1_ref[...] = x1
    xnt_ref[...] = _rmsnorm(x1, g_ref[...]).T.astype(BF16)


def _mix(x, hm, zp, w_out, norm_g):
    n, d = x.shape
    tb = _token_block(n, 512)
    rows = lambda i: (i, 0)
    const = lambda i: (0, 0)
    return pl.pallas_call(
        _mix_kernel,
        grid=(n // tb,),
        in_specs=[
            pl.BlockSpec((tb, d), rows),
            pl.BlockSpec((tb, MLSTM_W), rows),
            pl.BlockSpec((tb, POOL_W), rows),
            pl.BlockSpec(w_out.shape, const),
            pl.BlockSpec((1, d), const),
        ],
        out_specs=[
            pl.BlockSpec((tb, d), rows),
            pl.BlockSpec((d, tb), lambda i: (0, i)),
        ],
        out_shape=[
            jax.ShapeDtypeStruct((n, d), F32),
            jax.ShapeDtypeStruct((d, n), BF16),
        ],
        compiler_params=_params(("parallel",)),
        name="mix",
    )(x, hm, zp, w_out, norm_g)


def _extract_max(vals, pos):
    m = vals[0]
    for v in vals[1:]:
        m = jnp.maximum(m, v)
    m = jnp.max(m, axis=0, keepdims=True)
    big = jnp.float32(1e9)
    idx = None
    for v, p in zip(vals, pos):
        cand = jnp.where(v == m, p, big)
        idx = cand if idx is None else jnp.minimum(idx, cand)
    idx = jnp.min(idx, axis=0, keepdims=True)
    return m, idx


def _top16(s):
    tb = s.shape[1]
    row = lax.broadcasted_iota(jnp.int32, s.shape, 0).astype(F32)
    krow = lax.broadcasted_iota(jnp.int32, (PEER_TOPK, tb), 0)
    rank = jnp.full(s.shape, UNRANKED, F32)
    sv = jnp.zeros((PEER_TOPK, tb), F32)
    for k in range(PEER_TOPK):
        m, idx = _extract_max([s], [row])
        hit = row == idx
        rank = jnp.where(hit, float(k), rank)
        s = jnp.where(hit, NEG_INF, s)
        sv = jnp.where(krow == k, m, sv)
    return sv, rank


def _candidate_groups(sv1, sv2):
    tb = sv1.shape[1]
    vals, pos = [], []
    r8 = lax.broadcasted_iota(jnp.int32, (8, tb), 0)
    vals.append(sv1[0:1, :] + sv2[8:16, :])
    pos.append((r8 + 8).astype(F32))
    vals.append(sv1[0:1, :] + sv2[0:8, :])
    pos.append(r8.astype(F32))
    for k1 in range(1, 8):
        lim = PEER_TOPK // (k1 + 1)
        v = sv1[k1:k1 + 1, :] + sv2[0:8, :]
        vals.append(jnp.where(r8 < lim, v, NEG_INF))
        pos.append((r8 + k1 * PEER_TOPK).astype(F32))
    vals.append(sv1[8:16, :] + sv2[0:1, :])
    pos.append(((r8 + 8) * PEER_TOPK).astype(F32))
    return vals, pos


def _sorting_pairs(n):
    pairs = []
    p = 1
    while p < n:
        k = p
        while k >= 1:
            for j in range(k % p, n - k, 2 * k):
                for i in range(min(k, n - j - k)):
                    if (i + j) // (2 * p) == (i + j + k) // (2 * p):
                        pairs.append((i + j, i + j + k))
            k //= 2
        p *= 2
    return pairs


def _sublane_allreduce(x, op):
    for d in (1, 2, 4):
        x = op(x, pltpu.roll(x, d, axis=0))
    return x


def _sorted_top16(groups):
    g = list(groups)
    for i, j in _sorting_pairs(len(g)):
        g[i], g[j] = jnp.maximum(g[i], g[j]), jnp.minimum(g[i], g[j])
    n = len(g)
    for d in (1, 2, 4):
        p = [pltpu.roll(x, d, axis=0) for x in g]
        g = [jnp.maximum(g[i], p[n - 1 - i]) for i in range(n)]
        stride = n // 2
        while stride >= 1:
            for i in range(n):
                if i & stride == 0:
                    g[i], g[i + stride] = (jnp.maximum(g[i], g[i + stride]),
                                           jnp.minimum(g[i], g[i + stride]))
            stride //= 2
    return g


def _rank_bits(x, sv):
    c8 = x < sv[7]
    t = jnp.where(c8, sv[11], sv[3])
    c4 = x < t
    t = jnp.where(c8, jnp.where(c4, sv[13], sv[9]), jnp.where(c4, sv[5], sv[1]))
    c2 = x < t
    t = jnp.where(c8,
                  jnp.where(c4, jnp.where(c2, sv[14], sv[12]), jnp.where(c2, sv[10], sv[8])),
                  jnp.where(c4, jnp.where(c2, sv[6], sv[4]), jnp.where(c2, sv[2], sv[0])))
    c1 = x < t
    return (c8, c4, c2, c1), x < sv[15]


def _select16(bits, rows):
    c8, c4, c2, c1 = bits
    lvl = [jnp.where(c1, rows[2 * i + 1], rows[2 * i]) for i in range(8)]
    lvl = [jnp.where(c2, lvl[2 * i + 1], lvl[2 * i]) for i in range(4)]
    lvl = [jnp.where(c4, lvl[2 * i + 1], lvl[2 * i]) for i in range(2)]
    return jnp.where(c8, lvl[1], lvl[0])


def _route_fast(s1, s2):
    tb = s1.shape[1]
    r8 = lax.broadcasted_iota(jnp.int32, (8, tb), 0)
    n_grp = PEER_NKEYS // 8
    g1 = [s1[v * 8:(v + 1) * 8, :] for v in range(n_grp)]
    g2 = [s2[v * 8:(v + 1) * 8, :] for v in range(n_grp)]
    sv1 = _sorted_top16(g1)
    sv2 = _sorted_top16(g2)

    bad = jnp.zeros((8, tb), F32)
    for sv, grp in ((sv1, g1), (sv2, g2)):
        for k in range(PEER_TOPK - 1):
            bad = jnp.where(sv[k] == sv[k + 1], 1.0, bad)
        n_in = jnp.zeros((8, tb), F32)
        for x in grp:
            n_in = n_in + jnp.where(x >= sv[PEER_TOPK - 1], 1.0, 0.0)
        n_in = _sublane_allreduce(n_in, jnp.add)
        bad = jnp.where(n_in != float(PEER_TOPK), 1.0, bad)

    def by_sublane(rows):
        out = jnp.zeros((8, tb), F32)
        for k, row in enumerate(rows):
            out = jnp.where(r8 == k, row, out)
        return out
    a2_lo, a2_hi, a1_hi = by_sublane(sv2[0:8]), by_sublane(sv2[8:16]), by_sublane(sv1[8:16])
    orig = [sv1[0] + a2_hi, sv1[0] + a2_lo]
    for k1 in range(1, 8):
        orig.append(jnp.where(r8 < PEER_TOPK // (k1 + 1), sv1[k1] + a2_lo, NEG_INF))
    orig.append(a1_hi + sv2[0])
    vals = list(orig)
    prev = None
    for _ in range(PEER_TOPK):
        m = vals[0]
        for v in vals[1:]:
            m = jnp.maximum(m, v)
        m = _sublane_allreduce(m, jnp.maximum)
        vals = [jnp.where(v == m, NEG_INF, v) for v in vals]
        if prev is not None:
            bad = jnp.where(m == prev, 1.0, bad)
        prev = m
    sel = [o >= prev for o in orig]
    cmax = sv1[0] + sv2[0]
    z = jnp.zeros((8, tb), F32)
    for o, sl in zip(orig, sel):
        z = z + jnp.where(sl, jnp.exp(o - cmax), 0.0)
    z = _sublane_allreduce(z, jnp.add)
    ones = [jnp.where(sl, 1.0, 0.0) for sl in sel]
    cnt = [_sublane_allreduce(ones[0] + ones[1], jnp.add)]
    cnt += [_sublane_allreduce(ones[k1 + 1], jnp.add) for k1 in range(1, 8)]
    cnt += [ones[9][j:j + 1, :] for j in range(8)]
    total = cnt[0]
    for c in cnt[1:]:
        total = total + c
    bad = jnp.where(total != float(PEER_TOPK), 1.0, bad)

    cnt1, r2 = [], []
    weights = (8.0, 4.0, 2.0, 1.0)
    for x in g1:
        bits, below = _rank_bits(x, sv1)
        cnt1.append(jnp.where(below, 0.0, _select16(bits, cnt)))
    for x in g2:
        bits, below = _rank_bits(x, sv2)
        rank = jnp.zeros((8, tb), F32)
        for b, wgt in zip(bits, weights):
            rank = rank + jnp.where(b, wgt, 0.0)
        r2.append(jnp.where(below, UNRANKED, rank))
    return cnt1, r2, sv1[0][0:1, :], sv2[0][0:1, :], z[0:1, :], bad


def _route_exact(s1, s2):
    sv1, r1 = _top16(s1)
    sv2, r2 = _top16(s2)

    vals, pos = _candidate_groups(sv1, sv2)
    orig = list(vals)
    sel = [jnp.zeros(v.shape, F32) for v in vals]
    for _ in range(PEER_TOPK):
        _, idx = _extract_max(vals, pos)
        for i in range(len(vals)):
            hit = pos[i] == idx
            sel[i] = jnp.where(hit, 1.0, sel[i])
            vals[i] = jnp.where(hit, NEG_INF, vals[i])

    cmax = orig[1][0:1, :]
    z = None
    for o, sl in zip(orig, sel):
        part = jnp.sum(jnp.where(sl > 0.0, jnp.exp(o - cmax), 0.0), axis=0, keepdims=True)
        z = part if z is None else z + part
    cnt = [jnp.sum(sel[0] + sel[1], axis=0, keepdims=True)]
    cnt += [jnp.sum(sel[k1 + 1], axis=0, keepdims=True) for k1 in range(1, 8)]
    cnt += [sel[9][j:j + 1, :] for j in range(8)]

    cnt1 = jnp.zeros(r1.shape, F32)
    for k1 in range(PEER_TOPK):
        cnt1 = jnp.where(r1 == float(k1), cnt[k1], cnt1)
    return cnt1, r2, sv1[0:1, :], sv2[0:1, :], z


ROUTE_HEADS_PER_STEP = 4


def _route_kernel(xnt_ref, wqt_ref, k1_ref, k2_ref, cnt1_ref, g1_ref, r2_ref, e2_ref, s_ref):
    qt_all = _dot(wqt_ref[...], xnt_ref[...]).astype(BF16)

    def emit_gates(hh, s1, s2, max1, max2, z):
        g1_ref[hh] = jnp.exp(s1 - max1) * (0.5 / z)
        e2_ref[hh] = jnp.exp(s2 - max2).astype(BF16)

    ties = []
    for hh in range(ROUTE_HEADS_PER_STEP):
        qt = qt_all[hh * 2 * PEER_HALF:(hh + 1) * 2 * PEER_HALF, :]
        s1 = _dot(k1_ref[hh], qt[0:PEER_HALF, :])
        s2 = _dot(k2_ref[hh], qt[PEER_HALF:, :])
        s_ref[hh, 0] = s1
        s_ref[hh, 1] = s2
        cnt1, r2, max1, max2, z, tie = _route_fast(s1, s2)
        for v in range(PEER_NKEYS // 8):
            cnt1_ref[hh, v * 8:(v + 1) * 8, :] = cnt1[v]
            r2_ref[hh, v * 8:(v + 1) * 8, :] = r2[v].astype(BF16)
        emit_gates(hh, s1, s2, max1, max2, z)
        ties.append(tie)

    any_tie = ties[0]
    for tie in ties[1:]:
        any_tie = jnp.maximum(any_tie, tie)

    @pl.when(jnp.max(any_tie) > 0.0)
    def _():
        for hh in range(ROUTE_HEADS_PER_STEP):
            s1, s2 = s_ref[hh, 0], s_ref[hh, 1]
            cnt1, r2, max1, max2, z = _route_exact(s1, s2)
            cnt1_ref[hh] = cnt1
            r2_ref[hh] = r2.astype(BF16)
            emit_gates(hh, s1, s2, max1, max2, z)


def _route(xnt, wqt, keys):
    d, n = xnt.shape
    tb = _token_block(n, 256)
    hps = ROUTE_HEADS_PER_STEP
    per_head = pl.BlockSpec((hps, PEER_NKEYS, tb), lambda i, h: (h, 0, i))
    shape = lambda dt: jax.ShapeDtypeStruct((PEER_HEADS, PEER_NKEYS, n), dt)
    return pl.pallas_call(
        _route_kernel,
        grid=(n // tb, PEER_HEADS // hps),
        in_specs=[
            pl.BlockSpec((d, tb), lambda i, h: (0, i)),
            pl.BlockSpec((hps * 2 * PEER_HALF, d), lambda i, h: (h, 0)),
            pl.BlockSpec((hps, PEER_NKEYS, PEER_HALF), lambda i, h: (h, 0, 0)),
            pl.BlockSpec((hps, PEER_NKEYS, PEER_HALF), lambda i, h: (h + PEER_HEADS // hps, 0, 0)),
        ],
        out_specs=[per_head] * 4,
        out_shape=[shape(F32), shape(F32), shape(BF16), shape(BF16)],
        scratch_shapes=[pltpu.VMEM((hps, 2, PEER_NKEYS, tb), F32)],
        compiler_params=_params(("parallel", "parallel")),
        name="route",
    )(xnt, wqt, keys, keys)


I1_PER_TILE = 16
BF16_TILE = (8, 2 * LANE)
EXPERT_TILE = I1_PER_TILE * PEER_NKEYS


def _experts_kernel(xnt_ref, cnt1_ref, g1_ref, r2_ref, e2_ref, u_ref, vt_ref, x1_ref,
                    x2_ref, acc_ref, act_ref, wa_ref):
    e = pl.program_id(1)
    tb = xnt_ref.shape[1]

    @pl.when(e == 0)
    def _():
        acc_ref[...] = jnp.zeros_like(acc_ref)

    a = _dot(u_ref[...], xnt_ref[...]).astype(BF16)
    act_ref[...] = a * (1.0 + jnp.tanh(a * (GELU_C0 + GELU_C1 * (a * a))))
    sub, width = BF16_TILE[0], min(BF16_TILE[1], tb)
    assert I1_PER_TILE % sub == 0
    i1_rows = pl.ds(pl.multiple_of(e * I1_PER_TILE, I1_PER_TILE), I1_PER_TILE)
    for lt in range(tb // width):
        lanes = pl.ds(lt * width, width)
        cnt_tile = [cnt1_ref[h, i1_rows, lanes] for h in range(PEER_HEADS)]
        g1_tile = [g1_ref[h, i1_rows, lanes] for h in range(PEER_HEADS)]
        for j in range(I1_PER_TILE):
            cnt_b = [jnp.broadcast_to(t[j:j + 1, :], (sub, width)).astype(BF16) for t in cnt_tile]
            g1_b = [jnp.broadcast_to(t[j:j + 1, :], (sub, width)).astype(BF16) for t in g1_tile]
            for rb in range(PEER_NKEYS // sub):
                krows = pl.ds(rb * sub, sub)
                w = None
                for h in range(PEER_HEADS):
                    mask = r2_ref[h, krows, lanes] < cnt_b[h]
                    term = jnp.where(mask, e2_ref[h, krows, lanes] * g1_b[h], jnp.zeros((), BF16))
                    w = term if w is None else w + term
                erows = pl.ds(j * PEER_NKEYS + rb * sub, sub)
                wa_ref[erows, lanes] = w * act_ref[erows, lanes]
    acc_ref[...] += _dot(vt_ref[...], wa_ref[...])

    @pl.when(e == pl.num_programs(1) - 1)
    def _():
        x2_ref[...] = x1_ref[...] + acc_ref[...].T


def _experts(xnt, cnt1, g1, r2, e2, u_tabs, vt_tabs, layer, x1):
    d, n = xnt.shape
    n_exp = u_tabs.shape[1]
    tb = _token_block(n, 512)
    assert tb % LANE == 0 and n_exp % EXPERT_TILE == 0
    per_head = pl.BlockSpec((PEER_HEADS, PEER_NKEYS, tb), lambda i, e: (0, 0, i))
    tile_buf = pltpu.VMEM((EXPERT_TILE, tb), BF16)
    return pl.pallas_call(
        _experts_kernel,
        grid=(n // tb, n_exp // EXPERT_TILE),
        in_specs=[
            pl.BlockSpec((d, tb), lambda i, e: (0, i)),
            per_head, per_head, per_head, per_head,
            pl.BlockSpec((None, EXPERT_TILE, d), lambda i, e: (layer, e, 0)),
            pl.BlockSpec((None, d, EXPERT_TILE), lambda i, e: (layer, 0, e)),
            pl.BlockSpec((tb, d), lambda i, e: (i, 0)),
        ],
        out_specs=pl.BlockSpec((tb, d), lambda i, e: (i, 0)),
        out_shape=jax.ShapeDtypeStruct((n, d), F32),
        scratch_shapes=[pltpu.VMEM((d, tb), F32), tile_buf, tile_buf],
        compiler_params=_params(("parallel", "arbitrary")),
        name="experts",
    )(xnt, cnt1, g1, r2, e2, u_tabs, vt_tabs, x1)


def _ple_kernel(x_ref, p_ref, wg_ref, wp_ref, nf_ref, y_ref, *, final_norm):
    x = x_ref[...]
    gate = _sigmoid(_dot(x.astype(BF16), wg_ref[...]))
    y = x + gate * _dot(p_ref[...].astype(BF16), wp_ref[...])
    if final_norm:
        y = _rmsnorm(y, nf_ref[...])
    y_ref[...] = y


def _ple(x, p_all, layer, w_gate, w_ple, norm_f, final_norm):
    n, d = x.shape
    tb = _token_block(n, 512)
    rows = lambda i: (i, 0)
    const = lambda i: (0, 0)
    return pl.pallas_call(
        functools.partial(_ple_kernel, final_norm=final_norm),
        grid=(n // tb,),
        in_specs=[
            pl.BlockSpec((tb, d), rows),
            pl.BlockSpec((None, tb, p_all.shape[2]), lambda i: (layer, i, 0)),
            pl.BlockSpec(w_gate.shape, const),
            pl.BlockSpec(w_ple.shape, const),
            pl.BlockSpec((1, d), const),
        ],
        out_specs=pl.BlockSpec((tb, d), rows),
        out_shape=jax.ShapeDtypeStruct((n, d), F32),
        compiler_params=_params(("parallel",)),
        name="ple",
    )(x, p_all, w_gate, w_ple, norm_f)


def _prep_layer_weights(l, w_in, b_gate, mlstm_norm, w_pool, pool_scale, w_out, norm1, norm2,
                        peer_wq, peer_keys, peer_u, peer_v, w_ple, w_gate):
    w = MLSTM_W
    wi = w_in[l]
    d = wi.shape[0]
    wa = jnp.concatenate([wi[:, 0:w], wi[:, 2 * w:4 * w + POOL_W]], axis=1).astype(BF16)
    wkt = wi[:, w:2 * w].T.astype(BF16)
    wgt = wi[:, 4 * w + POOL_W:].T.astype(BF16)
    return dict(
        wa=wa, wkt=wkt, wgt=wgt,
        bg=b_gate[l].reshape(2 * HEADS, 1).astype(F32),
        mlstm_norm=mlstm_norm[l].reshape(1, w),
        w_pool=w_pool[l].astype(BF16),
        pool_scale=pool_scale[l].reshape(1, POOL_W),
        w_out=w_out[l].astype(BF16),
        norm1=norm1[l].reshape(1, d), norm2=norm2[l].reshape(1, d),
        wqt=peer_wq[l].T.astype(BF16),
        keys=peer_keys[l].reshape(2 * PEER_HEADS, PEER_NKEYS, PEER_HALF).astype(BF16),
        w_ple=w_ple[l].astype(BF16), w_gate=w_gate[l].astype(BF16),
        layer=l,
    )


def _ffn_and_embed(x1, xnt, p, lw, norm_f, final_norm):
    cnt1, g1, r2, e2 = _route(xnt, lw["wqt"], lw["keys"])
    x2 = _experts(xnt, cnt1, g1, r2, e2, lw["u_tabs"], lw["vt_tabs"], lw["layer"], x1)
    return _ple(x2, p, lw["layer"], lw["w_gate"], lw["w_ple"], norm_f, final_norm)


def _prompt_layer(x, p, lw, norm_f, final_norm, batch, t_len):
    q, kt, v, o, u, gt = _proj(x, lw["norm1"], lw["wa"], lw["wkt"], lw["wgt"], lw["bg"])
    gates3 = gt.reshape(2 * HEADS, (batch * t_len) // CHUNK, CHUNK)
    hm, cext, m8 = _mlstm(q, kt, v, o, gates3, lw["mlstm_norm"], batch, t_len)
    zp, buf = _pool(u, lw["w_pool"], lw["pool_scale"], batch, t_len)
    x1, xnt = _mix(x, hm, zp, lw["w_out"], lw["norm2"])
    y = _ffn_and_embed(x1, xnt, p, lw, norm_f, final_norm)
    return y, cext[..., :HEAD_DIM], cext[..., HEAD_DIM], m8[..., 0, 0], buf


def _sample_layer(x, p, c_all, n0, m0, buf0, lw, norm_f, final_norm):
    q, kt, v, o, u, gt = _proj(x, lw["norm1"], lw["wa"], lw["wkt"], lw["wgt"], lw["bg"])
    hm, c1, n1, m1 = _mlstm1(q, kt.T, v, o, gt.T, c_all, lw["layer"], n0.reshape(-1, MLSTM_W), m0,
                             lw["mlstm_norm"])
    n1 = n1.reshape(n0.shape)
    zp, buf_t = _pool1(u, jnp.swapaxes(buf0, 0, 1), lw["w_pool"], lw["pool_scale"])
    x1, xnt = _mix(x, hm, zp, lw["w_out"], lw["norm2"])
    y = _ffn_and_embed(x1, xnt, p, lw, norm_f, final_norm)
    return y, c1, n1, m1, jnp.swapaxes(buf_t, 0, 1)


def kernel(x_prompt, x_sample, p_prompt, p_sample, state_C, state_n, state_m, state_pool,
           w_in, b_gate, mlstm_norm, w_pool, pool_scale, w_out, norm1, norm2,
           peer_wq, peer_keys, peer_u, peer_v, w_ple, w_gate, norm_f):
    depth = w_in.shape[0]
    batch, t_len, d = x_prompt.shape
    dec_batch, dec_len, _ = x_sample.shape
    assert dec_len == 1 and t_len % CHUNK == 0
    nf = norm_f.reshape(1, d)

    xp = x_prompt.reshape(batch * t_len, d)
    xs = x_sample.reshape(dec_batch, d)
    outs_p, outs_s = [], []
    u_tabs = peer_u.astype(BF16)
    vt_tabs = jnp.swapaxes(peer_v, 1, 2).astype(BF16)
    for l in range(depth):
        lw = _prep_layer_weights(l, w_in, b_gate, mlstm_norm, w_pool, pool_scale, w_out, norm1,
                                 norm2, peer_wq, peer_keys, peer_u, peer_v, w_ple, w_gate)
        lw.update(u_tabs=u_tabs, vt_tabs=vt_tabs)
        final = l == depth - 1
        pp = p_prompt.reshape(depth, batch * t_len, -1)
        ps = p_sample.reshape(depth, dec_batch, -1)
        xp, c_p, n_p, m_p, buf_p = _prompt_layer(xp, pp, lw, nf, final, batch, t_len)
        xs, c_s, n_s, m_s, buf_s = _sample_layer(xs, ps, state_C, state_n[l], state_m[l],
                                                 state_pool[l], lw, nf, final)
        outs_p.append((c_p, n_p, m_p, buf_p))
        outs_s.append((c_s, n_s, m_s, buf_s))

    stack = lambda outs, i: jnp.stack([o[i] for o in outs])
    return (xp.reshape(batch, t_len, d), xs.reshape(dec_batch, dec_len, d),
            stack(outs_p, 0), stack(outs_p, 1), stack(outs_p, 2), stack(outs_p, 3),
            stack(outs_s, 0), stack(outs_s, 1), stack(outs_s, 2), stack(outs_s, 3))
```

```python
import functools
import math

import jax
import jax.numpy as jnp
from jax import lax
from jax.experimental import pallas as pl
from jax.experimental.pallas import tpu as pltpu

F32 = jnp.float32
BF16 = jnp.bfloat16
EPS = 1e-6
NEG_INF = float("-inf")

LANE = 128
V7X_VMEM_LIMIT = 56 * 1024 * 1024

HEADS = 4
HEAD_DIM = 128
MLSTM_W = HEADS * HEAD_DIM
POOL_WINDOWS = (2, 4, 8, 16)
POOL_GROUP = 128
POOL_W = POOL_GROUP * len(POOL_WINDOWS)
POOL_BUF = max(POOL_WINDOWS) - 1
CHUNK = 128
PEER_HEADS = 8
PEER_NKEYS = 128
PEER_HALF = 128
PEER_TOPK = 16
UNRANKED = 127.0
GELU_C0 = math.sqrt(2.0 / math.pi)
GELU_C1 = 0.044715 * GELU_C0


def _params(semantics):
    return pltpu.CompilerParams(dimension_semantics=semantics, vmem_limit_bytes=V7X_VMEM_LIMIT)


def _token_block(n, want):
    tb = min(n, want)
    assert n % tb == 0
    return tb


def _rmsnorm(x, g):
    return x * lax.rsqrt(jnp.mean(x * x, axis=-1, keepdims=True) + EPS) * g


def _log_sigmoid(x):
    return jnp.minimum(x, 0.0) - jnp.log(1.0 + jnp.exp(-jnp.abs(x)))


def _sigmoid(x):
    return 1.0 / (1.0 + jnp.exp(-x))


def _dot(a, b):
    return jnp.dot(a, b, preferred_element_type=F32)


def _dot_nt(a, b):
    return lax.dot_general(a, b, (((1,), (1,)), ((), ())), preferred_element_type=F32)


def _proj_kernel(x_ref, g_ref, wa_ref, wkt_ref, wgt_ref, bg_ref,
                 q_ref, kt_ref, v_ref, o_ref, u_ref, gt_ref):
    hn = _rmsnorm(x_ref[...], g_ref[...]).astype(BF16)
    pa = _dot(hn, wa_ref[...])
    w = MLSTM_W
    q_ref[...] = pa[:, 0:w].astype(BF16)
    v_ref[...] = pa[:, w:2 * w].astype(BF16)
    o_ref[...] = _sigmoid(pa[:, 2 * w:3 * w])
    u_ref[...] = pa[:, 3 * w:4 * w]
    kt = _dot_nt(wkt_ref[...], hn) * (HEAD_DIM ** -0.5)
    kt_ref[...] = kt.astype(BF16)
    gt_ref[...] = _dot_nt(wgt_ref[...], hn) + bg_ref[...]


def _proj(x, norm_g, wa, wkt, wgt, bg):
    n, d = x.shape
    tb = _token_block(n, 512)
    grid = (n // tb,)
    const = lambda i: (0, 0)
    return pl.pallas_call(
        _proj_kernel,
        grid=grid,
        in_specs=[
            pl.BlockSpec((tb, d), lambda i: (i, 0)),
            pl.BlockSpec((1, d), const),
            pl.BlockSpec(wa.shape, const),
            pl.BlockSpec(wkt.shape, const),
            pl.BlockSpec(wgt.shape, const),
            pl.BlockSpec(bg.shape, const),
        ],
        out_specs=[
            pl.BlockSpec((tb, MLSTM_W), lambda i: (i, 0)),
            pl.BlockSpec((MLSTM_W, tb), lambda i: (0, i)),
            pl.BlockSpec((tb, MLSTM_W), lambda i: (i, 0)),
            pl.BlockSpec((tb, MLSTM_W), lambda i: (i, 0)),
            pl.BlockSpec((tb, POOL_W), lambda i: (i, 0)),
            pl.BlockSpec((2 * HEADS, tb), lambda i: (0, i)),
        ],
        out_shape=[
            jax.ShapeDtypeStruct((n, MLSTM_W), BF16),
            jax.ShapeDtypeStruct((MLSTM_W, n), BF16),
            jax.ShapeDtypeStruct((n, MLSTM_W), BF16),
            jax.ShapeDtypeStruct((n, MLSTM_W), F32),
            jax.ShapeDtypeStruct((n, POOL_W), F32),
            jax.ShapeDtypeStruct((2 * HEADS, n), F32),
        ],
        compiler_params=_params(("parallel",)),
        name="proj",
    )(x, norm_g, wa, wkt, wgt, bg)


def _lane_cumsum(x):
    lane = lax.broadcasted_iota(jnp.int32, x.shape, 1)
    shift = 1
    while shift < x.shape[1]:
        x = x + jnp.where(lane >= shift, pltpu.roll(x, shift, axis=1), 0.0)
        shift *= 2
    return x


def _mlstm_kernel(q_ref, kt_ref, v_ref, o_ref, ig_ref, lf_ref, nw_ref,
                  hm_ref, cext_out_ref, m_out_ref, cext_ref):
    t_len = q_ref.shape[0]
    n_chunks = t_len // CHUNK
    ig2 = ig_ref[0]
    lf2 = _log_sigmoid(lf_ref[0])
    b2 = _lane_cumsum(lf2)
    a2 = ig2 - b2

    t_idx = lax.broadcasted_iota(jnp.int32, (CHUNK, CHUNK), 0)
    s_idx = lax.broadcasted_iota(jnp.int32, (CHUNK, CHUNK), 1)
    causal = s_idx <= t_idx
    ones_col = (lax.broadcasted_iota(jnp.int32, (CHUNK, HEAD_DIM), 1) == 0).astype(BF16)
    nw = nw_ref[...]

    cext_ref[...] = jnp.zeros_like(cext_ref)
    m = jnp.zeros((1, 1), F32)
    for c in range(n_chunks):
        rows = pl.ds(c * CHUNK, CHUNK)
        a_row = a2[c:c + 1, :]
        lf_row = lf2[c:c + 1, :]
        b_last = b2[c:c + 1, CHUNK - 1:CHUNK]
        b_col = jnp.sum(jnp.where(causal, lf_row, 0.0), axis=1, keepdims=True)
        amax_col = jnp.max(jnp.where(causal, a_row, NEG_INF), axis=1, keepdims=True)
        m_col = jnp.maximum(amax_col, m)
        decay_mat = jnp.where(causal, jnp.exp(a_row - m_col), 0.0)

        qc = q_ref[rows, :]
        ktc = kt_ref[:, rows]
        v_ext = jnp.concatenate([v_ref[rows, :], ones_col], axis=1)
        s = _dot(qc, ktc) * decay_mat
        intra = _dot(s.astype(BF16), v_ext)
        inter = _dot(qc, cext_ref[...].astype(BF16))
        tot = intra + jnp.exp(m - m_col) * inter
        num = tot[:, :HEAD_DIM]
        den = tot[:, HEAD_DIM:HEAD_DIM + 1]
        h = num / jnp.maximum(jnp.abs(den), jnp.exp(-(b_col + m_col)))
        h = h * lax.rsqrt(jnp.mean(h * h, axis=1, keepdims=True) + EPS)
        hm_ref[rows, :] = (h * nw * o_ref[rows, :]).astype(BF16)

        m_new = jnp.maximum(b_last + m, jnp.max(a_row, axis=1, keepdims=True) + b_last)
        ws_row = jnp.exp(a_row + (b_last - m_new))
        kws = (ktc.astype(F32) * ws_row).astype(BF16)
        cext_ref[...] = jnp.exp(b_last + m - m_new) * cext_ref[...] + _dot(kws, v_ext)
        m = m_new

    cext_out_ref[0, 0] = cext_ref[...]
    m_out_ref[0, 0] = jnp.broadcast_to(m, m_out_ref.shape[2:])


def _mlstm(q, kt, v, o, gates3, norm_w, batch, t_len):
    n = q.shape[0]
    n_chunks = t_len // CHUNK
    seq = lambda b, h: (b, h)
    return pl.pallas_call(
        _mlstm_kernel,
        grid=(batch, HEADS),
        in_specs=[
            pl.BlockSpec((t_len, HEAD_DIM), seq),
            pl.BlockSpec((HEAD_DIM, t_len), lambda b, h: (h, b)),
            pl.BlockSpec((t_len, HEAD_DIM), seq),
            pl.BlockSpec((t_len, HEAD_DIM), seq),
            pl.BlockSpec((1, n_chunks, CHUNK), lambda b, h: (h, b, 0)),
            pl.BlockSpec((1, n_chunks, CHUNK), lambda b, h: (h + HEADS, b, 0)),
            pl.BlockSpec((1, HEAD_DIM), lambda b, h: (0, h)),
        ],
        out_specs=[
            pl.BlockSpec((t_len, HEAD_DIM), seq),
            pl.BlockSpec((1, 1, HEAD_DIM, 2 * HEAD_DIM), lambda b, h: (b, h, 0, 0)),
            pl.BlockSpec((1, 1, 8, LANE), lambda b, h: (b, h, 0, 0)),
        ],
        out_shape=[
            jax.ShapeDtypeStruct((n, MLSTM_W), BF16),
            jax.ShapeDtypeStruct((batch, HEADS, HEAD_DIM, 2 * HEAD_DIM), F32),
            jax.ShapeDtypeStruct((batch, HEADS, 8, LANE), F32),
        ],
        scratch_shapes=[pltpu.VMEM((HEAD_DIM, 2 * HEAD_DIM), F32)],
        compiler_params=_params(("parallel", "parallel")),
        name="mlstm",
    )(q, kt, v, o, gates3, gates3, norm_w)


def _mlstm1_kernel(q_ref, k_ref, v_ref, o_ref, g_ref, c_ref, n_ref, m_ref, nw_ref,
                   hm_ref, c_out_ref, n_out_ref, m_out_ref):
    bb = q_ref.shape[0]
    row = lax.broadcasted_iota(jnp.int32, (bb, HEAD_DIM), 0)
    eye = (lax.broadcasted_iota(jnp.int32, (HEAD_DIM, HEAD_DIM), 0)
           == lax.broadcasted_iota(jnp.int32, (HEAD_DIM, HEAD_DIM), 1))
    lane_h = lax.broadcasted_iota(jnp.int32, (bb, HEADS), 1)
    g = g_ref[...]
    m_all = m_ref[...]
    m_out = jnp.zeros((bb, HEADS), F32)
    for h in range(HEADS):
        cols = slice(h * HEAD_DIM, (h + 1) * HEAD_DIM)
        qb = q_ref[:, cols]
        qf = qb.astype(F32)
        kf = k_ref[:, cols].astype(F32)
        vf = v_ref[:, cols].astype(F32)
        ig = g[:, h:h + 1]
        lf = _log_sigmoid(g[:, HEADS + h:HEADS + h + 1])
        m_old = m_all[:, h:h + 1]
        n_old = n_ref[:, cols]
        m_new = jnp.maximum(ig, lf + m_old)
        w_in = jnp.exp(ig - m_new)
        w_st = jnp.exp(lf + m_old - m_new)
        qc = jnp.zeros((bb, HEAD_DIM), F32)
        for j in range(bb):
            c_old = c_ref[j, h]
            res = _dot(qb, c_old.astype(BF16))
            qc = jnp.where(row == j, res, qc)
            k_col = jnp.sum(jnp.where(eye, kf[j:j + 1, :], 0.0), axis=1, keepdims=True)
            c_out_ref[j, h] = w_st[j:j + 1, :] * c_old + k_col * (w_in[j:j + 1, :] * vf[j:j + 1, :])
        s = jnp.sum(qf * kf, axis=1, keepdims=True) * w_in
        num = s * vf + w_st * qc
        den = s + w_st * jnp.sum(qf * n_old, axis=1, keepdims=True)
        hh = num / jnp.maximum(jnp.abs(den), jnp.exp(-m_new))
        hh = hh * lax.rsqrt(jnp.mean(hh * hh, axis=1, keepdims=True) + EPS)
        hm_ref[:, cols] = (hh * nw_ref[:, cols] * o_ref[:, cols]).astype(BF16)
        n_out_ref[:, cols] = w_st * n_old + w_in * kf
        m_out = jnp.where(lane_h == h, m_new, m_out)
    m_out_ref[...] = m_out


def _mlstm1(q, k, v, o, g, c_all, layer, n0, m0, norm_w):
    batch = q.shape[0]
    bb = _token_block(batch, 16)
    rows = lambda i: (i, 0)
    return pl.pallas_call(
        _mlstm1_kernel,
        grid=(batch // bb,),
        in_specs=[
            pl.BlockSpec((bb, MLSTM_W), rows),
            pl.BlockSpec((bb, MLSTM_W), rows),
            pl.BlockSpec((bb, MLSTM_W), rows),
            pl.BlockSpec((bb, MLSTM_W), rows),
            pl.BlockSpec((bb, 2 * HEADS), rows),
            pl.BlockSpec((None, bb, HEADS, HEAD_DIM, HEAD_DIM), lambda i: (layer, i, 0, 0, 0)),
            pl.BlockSpec((bb, MLSTM_W), rows),
            pl.BlockSpec((bb, HEADS), rows),
            pl.BlockSpec((1, MLSTM_W), lambda i: (0, 0)),
        ],
        out_specs=[
            pl.BlockSpec((bb, MLSTM_W), rows),
            pl.BlockSpec((bb, HEADS, HEAD_DIM, HEAD_DIM), lambda i: (i, 0, 0, 0)),
            pl.BlockSpec((bb, MLSTM_W), rows),
            pl.BlockSpec((bb, HEADS), rows),
        ],
        out_shape=[
            jax.ShapeDtypeStruct((batch, MLSTM_W), BF16),
            jax.ShapeDtypeStruct(c_all.shape[1:], F32),
            jax.ShapeDtypeStruct(n0.shape, F32),
            jax.ShapeDtypeStruct(m0.shape, F32),
        ],
        compiler_params=_params(("parallel",)),
        name="mlstm1",
    )(q, k, v, o, g, c_all, n0, m0, norm_w)


def _pool_kernel(u_ref, wp_ref, sc_ref, z_ref, buf_ref):
    t_len = u_ref.shape[0]
    t_idx = lax.broadcasted_iota(jnp.int32, (t_len, POOL_GROUP), 0)
    for g, w in enumerate(POOL_WINDOWS):
        cols = slice(g * POOL_GROUP, (g + 1) * POOL_GROUP)
        x = u_ref[:, cols]
        s = x
        k = 1
        while k < w:
            s = s + jnp.where(t_idx >= k, pltpu.roll(s, k, axis=0), 0.0)
            k *= 2
        cnt = jnp.minimum(t_idx + 1, w).astype(F32)
        r = s / cnt - x
        z = _dot(r.astype(BF16), wp_ref[g]) * sc_ref[:, cols]
        z_ref[:, cols] = z.astype(BF16)
    buf_ref[0] = u_ref[t_len - POOL_BUF:t_len, :]


def _pool(u, w_pool, scale, batch, t_len):
    n = u.shape[0]
    return pl.pallas_call(
        _pool_kernel,
        grid=(batch,),
        in_specs=[
            pl.BlockSpec((t_len, POOL_W), lambda b: (b, 0)),
            pl.BlockSpec(w_pool.shape, lambda b: (0, 0, 0)),
            pl.BlockSpec((1, POOL_W), lambda b: (0, 0)),
        ],
        out_specs=[
            pl.BlockSpec((t_len, POOL_W), lambda b: (b, 0)),
            pl.BlockSpec((1, POOL_BUF, POOL_W), lambda b: (b, 0, 0)),
        ],
        out_shape=[
            jax.ShapeDtypeStruct((n, POOL_W), BF16),
            jax.ShapeDtypeStruct((batch, POOL_BUF, POOL_W), F32),
        ],
        compiler_params=_params(("parallel",)),
        name="pool",
    )(u, w_pool, scale)


def _pool1_kernel(u_ref, buft_ref, wp_ref, sc_ref, z_ref, buft_out_ref):
    for g, w in enumerate(POOL_WINDOWS):
        cols = slice(g * POOL_GROUP, (g + 1) * POOL_GROUP)
        x = u_ref[:, cols]
        s = x
        for j in range(1, w):
            s = s + buft_ref[POOL_BUF - j, :, cols]
        r = s / float(w) - x
        z = _dot(r.astype(BF16), wp_ref[g]) * sc_ref[:, cols]
        z_ref[:, cols] = z.astype(BF16)
    for j in range(POOL_BUF - 1):
        buft_out_ref[j] = buft_ref[j + 1]
    buft_out_ref[POOL_BUF - 1] = u_ref[...]


def _pool1(u, buf_t, w_pool, scale):
    batch = u.shape[0]
    full2 = lambda i: (0, 0)
    full3 = lambda i: (0, 0, 0)
    return pl.pallas_call(
        _pool1_kernel,
        grid=(1,),
        in_specs=[
            pl.BlockSpec(u.shape, full2),
            pl.BlockSpec(buf_t.shape, full3),
            pl.BlockSpec(w_pool.shape, full3),
            pl.BlockSpec((1, POOL_W), full2),
        ],
        out_specs=[
            pl.BlockSpec((batch, POOL_W), full2),
            pl.BlockSpec(buf_t.shape, full3),
        ],
        out_shape=[
            jax.ShapeDtypeStruct((batch, POOL_W), BF16),
            jax.ShapeDtypeStruct(buf_t.shape, F32),
        ],
        compiler_params=_params(("arbitrary",)),
        name="pool1",
    )(u, buf_t, w_pool, scale)


def _mix_kernel(x_ref, hm_ref, zp_ref, wo_ref, g_ref, x1_ref, xnt_ref):
    mix = _dot(hm_ref[...], wo_ref[0:MLSTM_W, :]) + _dot(zp_ref[...], wo_ref[MLSTM_W:, :])
    x1 = x_ref[...] + mix
    x1_ref[...] = x1
    xnt_ref[...] = _rmsnorm(x1, g_ref[...]).T.astype(BF16)


def _mix(x, hm, zp, w_out, norm_g):
    n, d = x.shape
    tb = _token_block(n, 512)
    rows = lambda i: (i, 0)
    const = lambda i: (0, 0)
    return pl.pallas_call(
        _mix_kernel,
        grid=(n // tb,),
        in_specs=[
            pl.BlockSpec((tb, d), rows),
            pl.BlockSpec((tb, MLSTM_W), rows),
            pl.BlockSpec((tb, POOL_W), rows),
            pl.BlockSpec(w_out.shape, const),
            pl.BlockSpec((1, d), const),
        ],
        out_specs=[
            pl.BlockSpec((tb, d), rows),
            pl.BlockSpec((d, tb), lambda i: (0, i)),
        ],
        out_shape=[
            jax.ShapeDtypeStruct((n, d), F32),
            jax.ShapeDtypeStruct((d, n), BF16),
        ],
        compiler_params=_params(("parallel",)),
        name="mix",
    )(x, hm, zp, w_out, norm_g)


def _extract_max(vals, pos):
    m = vals[0]
    for v in vals[1:]:
        m = jnp.maximum(m, v)
    m = jnp.max(m, axis=0, keepdims=True)
    big = jnp.float32(1e9)
    idx = None
    for v, p in zip(vals, pos):
        cand = jnp.where(v == m, p, big)
        idx = cand if idx is None else jnp.minimum(idx, cand)
    idx = jnp.min(idx, axis=0, keepdims=True)
    return m, idx


def _top16(s):
    tb = s.shape[1]
    row = lax.broadcasted_iota(jnp.int32, s.shape, 0).astype(F32)
    krow = lax.broadcasted_iota(jnp.int32, (PEER_TOPK, tb), 0)
    rank = jnp.full(s.shape, UNRANKED, F32)
    sv = jnp.zeros((PEER_TOPK, tb), F32)
    for k in range(PEER_TOPK):
        m, idx = _extract_max([s], [row])
        hit = row == idx
        rank = jnp.where(hit, float(k), rank)
        s = jnp.where(hit, NEG_INF, s)
        sv = jnp.where(krow == k, m, sv)
    return sv, rank


def _candidate_groups(sv1, sv2):
    tb = sv1.shape[1]
    vals, pos = [], []
    r8 = lax.broadcasted_iota(jnp.int32, (8, tb), 0)
    vals.append(sv1[0:1, :] + sv2[8:16, :])
    pos.append((r8 + 8).astype(F32))
    vals.append(sv1[0:1, :] + sv2[0:8, :])
    pos.append(r8.astype(F32))
    for k1 in range(1, 8):
        lim = PEER_TOPK // (k1 + 1)
        v = sv1[k1:k1 + 1, :] + sv2[0:8, :]
        vals.append(jnp.where(r8 < lim, v, NEG_INF))
        pos.append((r8 + k1 * PEER_TOPK).astype(F32))
    vals.append(sv1[8:16, :] + sv2[0:1, :])
    pos.append(((r8 + 8) * PEER_TOPK).astype(F32))
    return vals, pos


def _sorting_pairs(n):
    pairs = []
    p = 1
    while p < n:
        k = p
        while k >= 1:
            for j in range(k % p, n - k, 2 * k):
                for i in range(min(k, n - j - k)):
                    if (i + j) // (2 * p) == (i + j + k) // (2 * p):
                        pairs.append((i + j, i + j + k))
            k //= 2
        p *= 2
    return pairs


def _sublane_allreduce(x, op):
    for d in (1, 2, 4):
        x = op(x, pltpu.roll(x, d, axis=0))
    return x


def _sorted_top16(groups):
    g = list(groups)
    for i, j in _sorting_pairs(len(g)):
        g[i], g[j] = jnp.maximum(g[i], g[j]), jnp.minimum(g[i], g[j])
    n = len(g)
    for d in (1, 2, 4):
        p = [pltpu.roll(x, d, axis=0) for x in g]
        g = [jnp.maximum(g[i], p[n - 1 - i]) for i in range(n)]
        stride = n // 2
        while stride >= 1:
            for i in range(n):
                if i & stride == 0:
                    g[i], g[i + stride] = (jnp.maximum(g[i], g[i + stride]),
                                           jnp.minimum(g[i], g[i + stride]))
            stride //= 2
    return g


def _rank_bits(x, sv):
    c8 = x < sv[7]
    t = jnp.where(c8, sv[11], sv[3])
    c4 = x < t
    t = jnp.where(c8, jnp.where(c4, sv[13], sv[9]), jnp.where(c4, sv[5], sv[1]))
    c2 = x < t
    t = jnp.where(c8,
                  jnp.where(c4, jnp.where(c2, sv[14], sv[12]), jnp.where(c2, sv[10], sv[8])),
                  jnp.where(c4, jnp.where(c2, sv[6], sv[4]), jnp.where(c2, sv[2], sv[0])))
    c1 = x < t
    return (c8, c4, c2, c1), x < sv[15]


def _select16(bits, rows):
    c8, c4, c2, c1 = bits
    lvl = [jnp.where(c1, rows[2 * i + 1], rows[2 * i]) for i in range(8)]
    lvl = [jnp.where(c2, lvl[2 * i + 1], lvl[2 * i]) for i in range(4)]
    lvl = [jnp.where(c4, lvl[2 * i + 1], lvl[2 * i]) for i in range(2)]
    return jnp.where(c8, lvl[1], lvl[0])


def _route_fast(s1, s2):
    tb = s1.shape[1]
    r8 = lax.broadcasted_iota(jnp.int32, (8, tb), 0)
    n_grp = PEER_NKEYS // 8
    g1 = [s1[v * 8:(v + 1) * 8, :] for v in range(n_grp)]
    g2 = [s2[v * 8:(v + 1) * 8, :] for v in range(n_grp)]
    sv1 = _sorted_top16(g1)
    sv2 = _sorted_top16(g2)

    bad = jnp.zeros((8, tb), F32)
    for sv, grp in ((sv1, g1), (sv2, g2)):
        for k in range(PEER_TOPK - 1):
            bad = jnp.where(sv[k] == sv[k + 1], 1.0, bad)
        n_in = jnp.zeros((8, tb), F32)
        for x in grp:
            n_in = n_in + jnp.where(x >= sv[PEER_TOPK - 1], 1.0, 0.0)
        n_in = _sublane_allreduce(n_in, jnp.add)
        bad = jnp.where(n_in != float(PEER_TOPK), 1.0, bad)

    def by_sublane(rows):
        out = jnp.zeros((8, tb), F32)
        for k, row in enumerate(rows):
            out = jnp.where(r8 == k, row, out)
        return out
    a2_lo, a2_hi, a1_hi = by_sublane(sv2[0:8]), by_sublane(sv2[8:16]), by_sublane(sv1[8:16])
    orig = [sv1[0] + a2_hi, sv1[0] + a2_lo]
    for k1 in range(1, 8):
        orig.append(jnp.where(r8 < PEER_TOPK // (k1 + 1), sv1[k1] + a2_lo, NEG_INF))
    orig.append(a1_hi + sv2[0])
    vals = list(orig)
    maxima = []
    for _ in range(PEER_TOPK):
        m = vals[0]
        for v in vals[1:]:
            m = jnp.maximum(m, v)
        m = _sublane_allreduce(m, jnp.maximum)
        vals = [jnp.where(v == m, NEG_INF, v) for v in vals]
        maxima.append(m)

    def count_ge(thr):
        n = jnp.zeros((8, tb), F32)
        for o in orig:
            n = n + jnp.where(o >= thr, 1.0, 0.0)
        return _sublane_allreduce(n, jnp.add)
    n_last = count_ge(maxima[-1])
    n_prev = count_ge(maxima[-2])
    use_last = n_last == float(PEER_TOPK)
    thr = jnp.where(use_last, maxima[-1], maxima[-2])
    bad = jnp.where(use_last | (n_prev == float(PEER_TOPK)), bad, 1.0)
    sel = [o >= thr for o in orig]
    cmax = sv1[0] + sv2[0]
    z = jnp.zeros((8, tb), F32)
    for o, sl in zip(orig, sel):
        z = z + jnp.where(sl, jnp.exp(o - cmax), 0.0)
    z = _sublane_allreduce(z, jnp.add)
    ones = [jnp.where(sl, 1.0, 0.0) for sl in sel]
    cnt = [_sublane_allreduce(ones[0] + ones[1], jnp.add)]
    cnt += [_sublane_allreduce(ones[k1 + 1], jnp.add) for k1 in range(1, 8)]
    cnt += [ones[9][j:j + 1, :] for j in range(8)]

    cnt1, r2 = [], []
    weights = (8.0, 4.0, 2.0, 1.0)
    for x in g1:
        bits, below = _rank_bits(x, sv1)
        cnt1.append(jnp.where(below, 0.0, _select16(bits, cnt)))
    for x in g2:
        bits, below = _rank_bits(x, sv2)
        rank = jnp.zeros((8, tb), F32)
        for b, wgt in zip(bits, weights):
            rank = rank + jnp.where(b, wgt, 0.0)
        r2.append(jnp.where(below, UNRANKED, rank))
    return cnt1, r2, sv1[0][0:1, :], sv2[0][0:1, :], z[0:1, :], bad


def _route_exact(s1, s2):
    sv1, r1 = _top16(s1)
    sv2, r2 = _top16(s2)

    vals, pos = _candidate_groups(sv1, sv2)
    orig = list(vals)
    sel = [jnp.zeros(v.shape, F32) for v in vals]
    for _ in range(PEER_TOPK):
        _, idx = _extract_max(vals, pos)
        for i in range(len(vals)):
            hit = pos[i] == idx
            sel[i] = jnp.where(hit, 1.0, sel[i])
            vals[i] = jnp.where(hit, NEG_INF, vals[i])

    cmax = orig[1][0:1, :]
    z = None
    for o, sl in zip(orig, sel):
        part = jnp.sum(jnp.where(sl > 0.0, jnp.exp(o - cmax), 0.0), axis=0, keepdims=True)
        z = part if z is None else z + part
    cnt = [jnp.sum(sel[0] + sel[1], axis=0, keepdims=True)]
    cnt += [jnp.sum(sel[k1 + 1], axis=0, keepdims=True) for k1 in range(1, 8)]
    cnt += [sel[9][j:j + 1, :] for j in range(8)]

    cnt1 = jnp.zeros(r1.shape, F32)
    for k1 in range(PEER_TOPK):
        cnt1 = jnp.where(r1 == float(k1), cnt[k1], cnt1)
    return cnt1, r2, sv1[0:1, :], sv2[0:1, :], z


ROUTE_HEADS_PER_STEP = 4


def _route_kernel(xnt_ref, wqt_ref, k1_ref, k2_ref, cnt1_ref, g1_ref, r2_ref, e2_ref, s_ref):
    qt_all = _dot(wqt_ref[...], xnt_ref[...]).astype(BF16)

    def emit_gates(hh, s1, s2, max1, max2, z):
        g1_ref[hh] = jnp.exp(s1 - max1) * (0.5 / z)
        e2_ref[hh] = jnp.exp(s2 - max2).astype(BF16)

    ties = []
    for hh in range(ROUTE_HEADS_PER_STEP):
        qt = qt_all[hh * 2 * PEER_HALF:(hh + 1) * 2 * PEER_HALF, :]
        s1 = _dot(k1_ref[hh], qt[0:PEER_HALF, :])
        s2 = _dot(k2_ref[hh], qt[PEER_HALF:, :])
        s_ref[hh, 0] = s1
        s_ref[hh, 1] = s2
        cnt1, r2, max1, max2, z, tie = _route_fast(s1, s2)
        for v in range(PEER_NKEYS // 8):
            cnt1_ref[hh, v * 8:(v + 1) * 8, :] = cnt1[v]
            r2_ref[hh, v * 8:(v + 1) * 8, :] = r2[v].astype(BF16)
        emit_gates(hh, s1, s2, max1, max2, z)
        ties.append(jnp.max(tie))

    for hh in range(ROUTE_HEADS_PER_STEP):
        @pl.when(ties[hh] > 0.0)
        def _(hh=hh):
            s1, s2 = s_ref[hh, 0], s_ref[hh, 1]
            cnt1, r2, max1, max2, z = _route_exact(s1, s2)
            cnt1_ref[hh] = cnt1
            r2_ref[hh] = r2.astype(BF16)
            emit_gates(hh, s1, s2, max1, max2, z)


def _route(xnt, wqt, keys):
    d, n = xnt.shape
    tb = _token_block(n, 256)
    hps = ROUTE_HEADS_PER_STEP
    per_head = pl.BlockSpec((hps, PEER_NKEYS, tb), lambda i, h: (h, 0, i))
    shape = lambda dt: jax.ShapeDtypeStruct((PEER_HEADS, PEER_NKEYS, n), dt)
    return pl.pallas_call(
        _route_kernel,
        grid=(n // tb, PEER_HEADS // hps),
        in_specs=[
            pl.BlockSpec((d, tb), lambda i, h: (0, i)),
            pl.BlockSpec((hps * 2 * PEER_HALF, d), lambda i, h: (h, 0)),
            pl.BlockSpec((hps, PEER_NKEYS, PEER_HALF), lambda i, h: (h, 0, 0)),
            pl.BlockSpec((hps, PEER_NKEYS, PEER_HALF), lambda i, h: (h + PEER_HEADS // hps, 0, 0)),
        ],
        out_specs=[per_head] * 4,
        out_shape=[shape(F32), shape(F32), shape(BF16), shape(BF16)],
        scratch_shapes=[pltpu.VMEM((hps, 2, PEER_NKEYS, tb), F32)],
        compiler_params=_params(("parallel", "parallel")),
        name="route",
    )(xnt, wqt, keys, keys)


I1_PER_TILE = 16
BF16_TILE = (8, 2 * LANE)
EXPERT_TILE = I1_PER_TILE * PEER_NKEYS


def _experts_kernel(xnt_ref, cnt1_ref, g1_ref, r2_ref, e2_ref, u_ref, vt_ref, x1_ref,
                    x2_ref, acc_ref, act_ref, wa_ref):
    e = pl.program_id(1)
    tb = xnt_ref.shape[1]

    @pl.when(e == 0)
    def _():
        acc_ref[...] = jnp.zeros_like(acc_ref)

    a = _dot(u_ref[...], xnt_ref[...]).astype(BF16)
    act_ref[...] = a * (1.0 + jnp.tanh(a * (GELU_C0 + GELU_C1 * (a * a))))
    sub, width = BF16_TILE[0], min(BF16_TILE[1], tb)
    assert I1_PER_TILE % sub == 0
    i1_rows = pl.ds(pl.multiple_of(e * I1_PER_TILE, I1_PER_TILE), I1_PER_TILE)
    for lt in range(tb // width):
        lanes = pl.ds(lt * width, width)
        cnt_tile = [cnt1_ref[h, i1_rows, lanes] for h in range(PEER_HEADS)]
        g1_tile = [g1_ref[h, i1_rows, lanes] for h in range(PEER_HEADS)]
        for j in range(I1_PER_TILE):
            cnt_b = [jnp.broadcast_to(t[j:j + 1, :], (sub, width)).astype(BF16) for t in cnt_tile]
            g1_b = [jnp.broadcast_to(t[j:j + 1, :], (sub, width)).astype(BF16) for t in g1_tile]
            for rb in range(PEER_NKEYS // sub):
                krows = pl.ds(rb * sub, sub)
                w = None
                for h in range(PEER_HEADS):
                    mask = r2_ref[h, krows, lanes] < cnt_b[h]
                    term = jnp.where(mask, e2_ref[h, krows, lanes] * g1_b[h], jnp.zeros((), BF16))
                    w = term if w is None else w + term
                erows = pl.ds(j * PEER_NKEYS + rb * sub, sub)
                wa_ref[erows, lanes] = w * act_ref[erows, lanes]
    acc_ref[...] += _dot(vt_ref[...], wa_ref[...])

    @pl.when(e == pl.num_programs(1) - 1)
    def _():
        x2_ref[...] = x1_ref[...] + acc_ref[...].T


def _experts(xnt, cnt1, g1, r2, e2, u_tabs, vt_tabs, layer, x1):
    d, n = xnt.shape
    n_exp = u_tabs.shape[1]
    tb = _token_block(n, 512)
    assert tb % LANE == 0 and n_exp % EXPERT_TILE == 0
    per_head = pl.BlockSpec((PEER_HEADS, PEER_NKEYS, tb), lambda i, e: (0, 0, i))
    tile_buf = pltpu.VMEM((EXPERT_TILE, tb), BF16)
    return pl.pallas_call(
        _experts_kernel,
        grid=(n // tb, n_exp // EXPERT_TILE),
        in_specs=[
            pl.BlockSpec((d, tb), lambda i, e: (0, i)),
            per_head, per_head, per_head, per_head,
            pl.BlockSpec((None, EXPERT_TILE, d), lambda i, e: (layer, e, 0)),
            pl.BlockSpec((None, d, EXPERT_TILE), lambda i, e: (layer, 0, e)),
            pl.BlockSpec((tb, d), lambda i, e: (i, 0)),
        ],
        out_specs=pl.BlockSpec((tb, d), lambda i, e: (i, 0)),
        out_shape=jax.ShapeDtypeStruct((n, d), F32),
        scratch_shapes=[pltpu.VMEM((d, tb), F32), tile_buf, tile_buf],
        compiler_params=_params(("parallel", "arbitrary")),
        name="experts",
    )(xnt, cnt1, g1, r2, e2, u_tabs, vt_tabs, x1)


def _ple_kernel(x_ref, p_ref, wg_ref, wp_ref, nf_ref, y_ref, *, final_norm):
    x = x_ref[...]
    gate = _sigmoid(_dot(x.astype(BF16), wg_ref[...]))
    y = x + gate * _dot(p_ref[...].astype(BF16), wp_ref[...])
    if final_norm:
        y = _rmsnorm(y, nf_ref[...])
    y_ref[...] = y


def _ple(x, p_all, layer, w_gate, w_ple, norm_f, final_norm):
    n, d = x.shape
    tb = _token_block(n, 512)
    rows = lambda i: (i, 0)
    const = lambda i: (0, 0)
    return pl.pallas_call(
        functools.partial(_ple_kernel, final_norm=final_norm),
        grid=(n // tb,),
        in_specs=[
            pl.BlockSpec((tb, d), rows),
            pl.BlockSpec((None, tb, p_all.shape[2]), lambda i: (layer, i, 0)),
            pl.BlockSpec(w_gate.shape, const),
            pl.BlockSpec(w_ple.shape, const),
            pl.BlockSpec((1, d), const),
        ],
        out_specs=pl.BlockSpec((tb, d), rows),
        out_shape=jax.ShapeDtypeStruct((n, d), F32),
        compiler_params=_params(("parallel",)),
        name="ple",
    )(x, p_all, w_gate, w_ple, norm_f)


def _prep_layer_weights(l, w_in, b_gate, mlstm_norm, w_pool, pool_scale, w_out, norm1, norm2,
                        peer_wq, peer_keys, peer_u, peer_v, w_ple, w_gate):
    w = MLSTM_W
    wi = w_in[l]
    d = wi.shape[0]
    wa = jnp.concatenate([wi[:, 0:w], wi[:, 2 * w:4 * w + POOL_W]], axis=1).astype(BF16)
    wkt = wi[:, w:2 * w].T.astype(BF16)
    wgt = wi[:, 4 * w + POOL_W:].T.astype(BF16)
    return dict(
        wa=wa, wkt=wkt, wgt=wgt,
        bg=b_gate[l].reshape(2 * HEADS, 1).astype(F32),
        mlstm_norm=mlstm_norm[l].reshape(1, w),
        w_pool=w_pool[l].astype(BF16),
        pool_scale=pool_scale[l].reshape(1, POOL_W),
        w_out=w_out[l].astype(BF16),
        norm1=norm1[l].reshape(1, d), norm2=norm2[l].reshape(1, d),
        wqt=peer_wq[l].T.astype(BF16),
        keys=peer_keys[l].reshape(2 * PEER_HEADS, PEER_NKEYS, PEER_HALF).astype(BF16),
        w_ple=w_ple[l].astype(BF16), w_gate=w_gate[l].astype(BF16),
        layer=l,
    )


def _ffn_and_embed(x1, xnt, p, lw, norm_f, final_norm):
    cnt1, g1, r2, e2 = _route(xnt, lw["wqt"], lw["keys"])
    x2 = _experts(xnt, cnt1, g1, r2, e2, lw["u_tabs"], lw["vt_tabs"], lw["layer"], x1)
    return _ple(x2, p, lw["layer"], lw["w_gate"], lw["w_ple"], norm_f, final_norm)


def _prompt_layer(x, p, lw, norm_f, final_norm, batch, t_len):
    q, kt, v, o, u, gt = _proj(x, lw["norm1"], lw["wa"], lw["wkt"], lw["wgt"], lw["bg"])
    gates3 = gt.reshape(2 * HEADS, (batch * t_len) // CHUNK, CHUNK)
    hm, cext, m8 = _mlstm(q, kt, v, o, gates3, lw["mlstm_norm"], batch, t_len)
    zp, buf = _pool(u, lw["w_pool"], lw["pool_scale"], batch, t_len)
    x1, xnt = _mix(x, hm, zp, lw["w_out"], lw["norm2"])
    y = _ffn_and_embed(x1, xnt, p, lw, norm_f, final_norm)
    return y, cext[..., :HEAD_DIM], cext[..., HEAD_DIM], m8[..., 0, 0], buf


def _sample_layer(x, p, c_all, n0, m0, buf0, lw, norm_f, final_norm):
    q, kt, v, o, u, gt = _proj(x, lw["norm1"], lw["wa"], lw["wkt"], lw["wgt"], lw["bg"])
    hm, c1, n1, m1 = _mlstm1(q, kt.T, v, o, gt.T, c_all, lw["layer"], n0.reshape(-1, MLSTM_W), m0,
                             lw["mlstm_norm"])
    n1 = n1.reshape(n0.shape)
    zp, buf_t = _pool1(u, jnp.swapaxes(buf0, 0, 1), lw["w_pool"], lw["pool_scale"])
    x1, xnt = _mix(x, hm, zp, lw["w_out"], lw["norm2"])
    y = _ffn_and_embed(x1, xnt, p, lw, norm_f, final_norm)
    return y, c1, n1, m1, jnp.swapaxes(buf_t, 0, 1)


def kernel(x_prompt, x_sample, p_prompt, p_sample, state_C, state_n, state_m, state_pool,
           w_in, b_gate, mlstm_norm, w_pool, pool_scale, w_out, norm1, norm2,
           peer_wq, peer_keys, peer_u, peer_v, w_ple, w_gate, norm_f):
    depth = w_in.shape[0]
    batch, t_len, d = x_prompt.shape
    dec_batch, dec_len, _ = x_sample.shape
    assert dec_len == 1 and t_len % CHUNK == 0
    nf = norm_f.reshape(1, d)

    xp = x_prompt.reshape(batch * t_len, d)
    xs = x_sample.reshape(dec_batch, d)
    outs_p, outs_s = [], []
    u_tabs = peer_u.astype(BF16)
    vt_tabs = jnp.swapaxes(peer_v, 1, 2).astype(BF16)
    for l in range(depth):
        lw = _prep_layer_weights(l, w_in, b_gate, mlstm_norm, w_pool, pool_scale, w_out, norm1,
                                 norm2, peer_wq, peer_keys, peer_u, peer_v, w_ple, w_gate)
        lw.update(u_tabs=u_tabs, vt_tabs=vt_tabs)
        final = l == depth - 1
        pp = p_prompt.reshape(depth, batch * t_len, -1)
        ps = p_sample.reshape(depth, dec_batch, -1)
        xp, c_p, n_p, m_p, buf_p = _prompt_layer(xp, pp, lw, nf, final, batch, t_len)
        xs, c_s, n_s, m_s, buf_s = _sample_layer(xs, ps, state_C, state_n[l], state_m[l],
                                                 state_pool[l], lw, nf, final)
        outs_p.append((c_p, n_p, m_p, buf_p))
        outs_s.append((c_s, n_s, m_s, buf_s))

    stack = lambda outs, i: jnp.stack([o[i] for o in outs])
    return (xp.reshape(batch, t_len, d), xs.reshape(dec_batch, dec_len, d),
            stack(outs_p, 0), stack(outs_p, 1), stack(outs_p, 2), stack(outs_p, 3),
            stack(outs_s, 0), stack(outs_s, 1), stack(outs_s, 2), stack(outs_s, 3))
```

```python
import functools
import math

import jax
import jax.numpy as jnp
from jax import lax
from jax.experimental import pallas as pl
from jax.experimental.pallas import tpu as pltpu

F32 = jnp.float32
BF16 = jnp.bfloat16
EPS = 1e-6
NEG_INF = float("-inf")

LANE = 128
V7X_VMEM_LIMIT = 56 * 1024 * 1024

HEADS = 4
HEAD_DIM = 128
MLSTM_W = HEADS * HEAD_DIM
POOL_WINDOWS = (2, 4, 8, 16)
POOL_GROUP = 128
POOL_W = POOL_GROUP * len(POOL_WINDOWS)
POOL_BUF = max(POOL_WINDOWS) - 1
CHUNK = 128
PEER_HEADS = 8
PEER_NKEYS = 128
PEER_HALF = 128
PEER_TOPK = 16
UNRANKED = 127.0
GELU_C0 = math.sqrt(2.0 / math.pi)
GELU_C1 = 0.044715 * GELU_C0


def _params(semantics):
    return pltpu.CompilerParams(dimension_semantics=semantics, vmem_limit_bytes=V7X_VMEM_LIMIT)


def _token_block(n, want):
    tb = min(n, want)
    assert n % tb == 0
    return tb


def _rmsnorm(x, g):
    return x * lax.rsqrt(jnp.mean(x * x, axis=-1, keepdims=True) + EPS) * g


def _log_sigmoid(x):
    return jnp.minimum(x, 0.0) - jnp.log(1.0 + jnp.exp(-jnp.abs(x)))


def _sigmoid(x):
    return 1.0 / (1.0 + jnp.exp(-x))


def _dot(a, b):
    return jnp.dot(a, b, preferred_element_type=F32)


def _dot_nt(a, b):
    return lax.dot_general(a, b, (((1,), (1,)), ((), ())), preferred_element_type=F32)


def _proj_kernel(x_ref, g_ref, wa_ref, wkt_ref, wgt_ref, bg_ref,
                 q_ref, kt_ref, v_ref, o_ref, u_ref, gt_ref):
    hn = _rmsnorm(x_ref[...], g_ref[...]).astype(BF16)
    pa = _dot(hn, wa_ref[...])
    w = MLSTM_W
    q_ref[...] = pa[:, 0:w].astype(BF16)
    v_ref[...] = pa[:, w:2 * w].astype(BF16)
    o_ref[...] = _sigmoid(pa[:, 2 * w:3 * w]).astype(BF16)
    u_ref[...] = pa[:, 3 * w:4 * w]
    kt = _dot_nt(wkt_ref[...], hn) * (HEAD_DIM ** -0.5)
    kt_ref[...] = kt.astype(BF16)
    gt_ref[...] = _dot_nt(wgt_ref[...], hn) + bg_ref[...]


def _proj(x, norm_g, wa, wkt, wgt, bg):
    n, d = x.shape
    tb = _token_block(n, 1024)
    grid = (n // tb,)
    const = lambda i: (0, 0)
    return pl.pallas_call(
        _proj_kernel,
        grid=grid,
        in_specs=[
            pl.BlockSpec((tb, d), lambda i: (i, 0)),
            pl.BlockSpec((1, d), const),
            pl.BlockSpec(wa.shape, const),
            pl.BlockSpec(wkt.shape, const),
            pl.BlockSpec(wgt.shape, const),
            pl.BlockSpec(bg.shape, const),
        ],
        out_specs=[
            pl.BlockSpec((tb, MLSTM_W), lambda i: (i, 0)),
            pl.BlockSpec((MLSTM_W, tb), lambda i: (0, i)),
            pl.BlockSpec((tb, MLSTM_W), lambda i: (i, 0)),
            pl.BlockSpec((tb, MLSTM_W), lambda i: (i, 0)),
            pl.BlockSpec((tb, POOL_W), lambda i: (i, 0)),
            pl.BlockSpec((2 * HEADS, tb), lambda i: (0, i)),
        ],
        out_shape=[
            jax.ShapeDtypeStruct((n, MLSTM_W), BF16),
            jax.ShapeDtypeStruct((MLSTM_W, n), BF16),
            jax.ShapeDtypeStruct((n, MLSTM_W), BF16),
            jax.ShapeDtypeStruct((n, MLSTM_W), BF16),
            jax.ShapeDtypeStruct((n, POOL_W), F32),
            jax.ShapeDtypeStruct((2 * HEADS, n), F32),
        ],
        compiler_params=_params(("parallel",)),
        name="proj",
    )(x, norm_g, wa, wkt, wgt, bg)


def _lane_cumsum(x):
    lane = lax.broadcasted_iota(jnp.int32, x.shape, 1)
    shift = 1
    while shift < x.shape[1]:
        x = x + jnp.where(lane >= shift, pltpu.roll(x, shift, axis=1), 0.0)
        shift *= 2
    return x


def _mlstm_kernel(q_ref, kt_ref, v_ref, o_ref, ig_ref, lf_ref, nw_ref,
                  hm_ref, cext_out_ref, m_out_ref, cext_ref):
    t_len = q_ref.shape[0]
    n_chunks = t_len // CHUNK
    ig2 = ig_ref[0]
    lf2 = _log_sigmoid(lf_ref[0])
    b2 = _lane_cumsum(lf2)
    a2 = ig2 - b2

    t_idx = lax.broadcasted_iota(jnp.int32, (CHUNK, CHUNK), 0)
    s_idx = lax.broadcasted_iota(jnp.int32, (CHUNK, CHUNK), 1)
    causal = s_idx <= t_idx
    ones_col = (lax.broadcasted_iota(jnp.int32, (CHUNK, HEAD_DIM), 1) == 0).astype(BF16)
    nw = nw_ref[...]

    cext_ref[...] = jnp.zeros_like(cext_ref)
    m = jnp.zeros((1, 1), F32)
    for c in range(n_chunks):
        rows = pl.ds(c * CHUNK, CHUNK)
        a_row = a2[c:c + 1, :]
        lf_row = lf2[c:c + 1, :]
        b_last = b2[c:c + 1, CHUNK - 1:CHUNK]
        b_col = jnp.sum(jnp.where(causal, lf_row, 0.0), axis=1, keepdims=True)
        amax_col = jnp.max(jnp.where(causal, a_row, NEG_INF), axis=1, keepdims=True)
        m_col = jnp.maximum(amax_col, m)
        decay_mat = jnp.where(causal, jnp.exp(a_row - m_col), 0.0)

        qc = q_ref[rows, :]
        ktc = kt_ref[:, rows]
        v_ext = jnp.concatenate([v_ref[rows, :], ones_col], axis=1)
        s = _dot(qc, ktc) * decay_mat
        intra = _dot(s.astype(BF16), v_ext)
        inter = _dot(qc, cext_ref[...].astype(BF16))
        tot = intra + jnp.exp(m - m_col) * inter
        num = tot[:, :HEAD_DIM]
        den = tot[:, HEAD_DIM:HEAD_DIM + 1]
        h = num / jnp.maximum(jnp.abs(den), jnp.exp(-(b_col + m_col)))
        h = h * lax.rsqrt(jnp.mean(h * h, axis=1, keepdims=True) + EPS)
        hm_ref[rows, :] = (h * nw * o_ref[rows, :]).astype(BF16)

        m_new = jnp.maximum(b_last + m, jnp.max(a_row, axis=1, keepdims=True) + b_last)
        ws_row = jnp.exp(a_row + (b_last - m_new))
        kws = (ktc.astype(F32) * ws_row).astype(BF16)
        cext_ref[...] = jnp.exp(b_last + m - m_new) * cext_ref[...] + _dot(kws, v_ext)
        m = m_new

    cext_out_ref[0, 0] = cext_ref[...]
    m_out_ref[0, 0] = jnp.broadcast_to(m, m_out_ref.shape[2:])


def _mlstm(q, kt, v, o, gates3, norm_w, batch, t_len):
    n = q.shape[0]
    n_chunks = t_len // CHUNK
    seq = lambda b, h: (b, h)
    return pl.pallas_call(
        _mlstm_kernel,
        grid=(batch, HEADS),
        in_specs=[
            pl.BlockSpec((t_len, HEAD_DIM), seq),
            pl.BlockSpec((HEAD_DIM, t_len), lambda b, h: (h, b)),
            pl.BlockSpec((t_len, HEAD_DIM), seq),
            pl.BlockSpec((t_len, HEAD_DIM), seq),
            pl.BlockSpec((1, n_chunks, CHUNK), lambda b, h: (h, b, 0)),
            pl.BlockSpec((1, n_chunks, CHUNK), lambda b, h: (h + HEADS, b, 0)),
            pl.BlockSpec((1, HEAD_DIM), lambda b, h: (0, h)),
        ],
        out_specs=[
            pl.BlockSpec((t_len, HEAD_DIM), seq),
            pl.BlockSpec((1, 1, HEAD_DIM, 2 * HEAD_DIM), lambda b, h: (b, h, 0, 0)),
            pl.BlockSpec((1, 1, 8, LANE), lambda b, h: (b, h, 0, 0)),
        ],
        out_shape=[
            jax.ShapeDtypeStruct((n, MLSTM_W), BF16),
            jax.ShapeDtypeStruct((batch, HEADS, HEAD_DIM, 2 * HEAD_DIM), F32),
            jax.ShapeDtypeStruct((batch, HEADS, 8, LANE), F32),
        ],
        scratch_shapes=[pltpu.VMEM((HEAD_DIM, 2 * HEAD_DIM), F32)],
        compiler_params=_params(("parallel", "parallel")),
        name="mlstm",
    )(q, kt, v, o, gates3, gates3, norm_w)


def _mlstm1_kernel(q_ref, k_ref, v_ref, o_ref, g_ref, c_ref, n_ref, m_ref, nw_ref,
                   hm_ref, c_out_ref, n_out_ref, m_out_ref):
    bb = q_ref.shape[0]
    row = lax.broadcasted_iota(jnp.int32, (bb, HEAD_DIM), 0)
    eye = (lax.broadcasted_iota(jnp.int32, (HEAD_DIM, HEAD_DIM), 0)
           == lax.broadcasted_iota(jnp.int32, (HEAD_DIM, HEAD_DIM), 1))
    lane_h = lax.broadcasted_iota(jnp.int32, (bb, HEADS), 1)
    g = g_ref[...]
    m_all = m_ref[...]
    m_out = jnp.zeros((bb, HEADS), F32)
    for h in range(HEADS):
        cols = slice(h * HEAD_DIM, (h + 1) * HEAD_DIM)
        qb = q_ref[:, cols]
        qf = qb.astype(F32)
        kf = k_ref[:, cols].astype(F32)
        vf = v_ref[:, cols].astype(F32)
        ig = g[:, h:h + 1]
        lf = _log_sigmoid(g[:, HEADS + h:HEADS + h + 1])
        m_old = m_all[:, h:h + 1]
        n_old = n_ref[:, cols]
        m_new = jnp.maximum(ig, lf + m_old)
        w_in = jnp.exp(ig - m_new)
        w_st = jnp.exp(lf + m_old - m_new)
        qc = jnp.zeros((bb, HEAD_DIM), F32)
        for j in range(bb):
            c_old = c_ref[j, h]
            res = _dot(qb, c_old.astype(BF16))
            qc = jnp.where(row == j, res, qc)
            k_col = jnp.sum(jnp.where(eye, kf[j:j + 1, :], 0.0), axis=1, keepdims=True)
            c_out_ref[j, h] = w_st[j:j + 1, :] * c_old + k_col * (w_in[j:j + 1, :] * vf[j:j + 1, :])
        s = jnp.sum(qf * kf, axis=1, keepdims=True) * w_in
        num = s * vf + w_st * qc
        den = s + w_st * jnp.sum(qf * n_old, axis=1, keepdims=True)
        hh = num / jnp.maximum(jnp.abs(den), jnp.exp(-m_new))
        hh = hh * lax.rsqrt(jnp.mean(hh * hh, axis=1, keepdims=True) + EPS)
        hm_ref[:, cols] = (hh * nw_ref[:, cols] * o_ref[:, cols]).astype(BF16)
        n_out_ref[:, cols] = w_st * n_old + w_in * kf
        m_out = jnp.where(lane_h == h, m_new, m_out)
    m_out_ref[...] = m_out


def _mlstm1(q, k, v, o, g, c_all, layer, n0, m0, norm_w):
    batch = q.shape[0]
    bb = _token_block(batch, 16)
    rows = lambda i: (i, 0)
    return pl.pallas_call(
        _mlstm1_kernel,
        grid=(batch // bb,),
        in_specs=[
            pl.BlockSpec((bb, MLSTM_W), rows),
            pl.BlockSpec((bb, MLSTM_W), rows),
            pl.BlockSpec((bb, MLSTM_W), rows),
            pl.BlockSpec((bb, MLSTM_W), rows),
            pl.BlockSpec((bb, 2 * HEADS), rows),
            pl.BlockSpec((None, bb, HEADS, HEAD_DIM, HEAD_DIM), lambda i: (layer, i, 0, 0, 0)),
            pl.BlockSpec((bb, MLSTM_W), rows),
            pl.BlockSpec((bb, HEADS), rows),
            pl.BlockSpec((1, MLSTM_W), lambda i: (0, 0)),
        ],
        out_specs=[
            pl.BlockSpec((bb, MLSTM_W), rows),
            pl.BlockSpec((bb, HEADS, HEAD_DIM, HEAD_DIM), lambda i: (i, 0, 0, 0)),
            pl.BlockSpec((bb, MLSTM_W), rows),
            pl.BlockSpec((bb, HEADS), rows),
        ],
        out_shape=[
            jax.ShapeDtypeStruct((batch, MLSTM_W), BF16),
            jax.ShapeDtypeStruct(c_all.shape[1:], F32),
            jax.ShapeDtypeStruct(n0.shape, F32),
            jax.ShapeDtypeStruct(m0.shape, F32),
        ],
        compiler_params=_params(("parallel",)),
        name="mlstm1",
    )(q, k, v, o, g, c_all, n0, m0, norm_w)


def _pool_kernel(u_ref, wp_ref, sc_ref, z_ref, buf_ref):
    t_len = u_ref.shape[0]
    t_idx = lax.broadcasted_iota(jnp.int32, (t_len, POOL_GROUP), 0)
    for g, w in enumerate(POOL_WINDOWS):
        cols = slice(g * POOL_GROUP, (g + 1) * POOL_GROUP)
        x = u_ref[:, cols]
        s = x
        k = 1
        while k < w:
            s = s + jnp.where(t_idx >= k, pltpu.roll(s, k, axis=0), 0.0)
            k *= 2
        cnt = jnp.minimum(t_idx + 1, w).astype(F32)
        r = s / cnt - x
        z = _dot(r.astype(BF16), wp_ref[g]) * sc_ref[:, cols]
        z_ref[:, cols] = z.astype(BF16)
    buf_ref[0] = u_ref[t_len - POOL_BUF:t_len, :]


def _pool(u, w_pool, scale, batch, t_len):
    n = u.shape[0]
    return pl.pallas_call(
        _pool_kernel,
        grid=(batch,),
        in_specs=[
            pl.BlockSpec((t_len, POOL_W), lambda b: (b, 0)),
            pl.BlockSpec(w_pool.shape, lambda b: (0, 0, 0)),
            pl.BlockSpec((1, POOL_W), lambda b: (0, 0)),
        ],
        out_specs=[
            pl.BlockSpec((t_len, POOL_W), lambda b: (b, 0)),
            pl.BlockSpec((1, POOL_BUF, POOL_W), lambda b: (b, 0, 0)),
        ],
        out_shape=[
            jax.ShapeDtypeStruct((n, POOL_W), BF16),
            jax.ShapeDtypeStruct((batch, POOL_BUF, POOL_W), F32),
        ],
        compiler_params=_params(("parallel",)),
        name="pool",
    )(u, w_pool, scale)


def _pool1_kernel(u_ref, buft_ref, wp_ref, sc_ref, z_ref, buft_out_ref):
    for g, w in enumerate(POOL_WINDOWS):
        cols = slice(g * POOL_GROUP, (g + 1) * POOL_GROUP)
        x = u_ref[:, cols]
        s = x
        for j in range(1, w):
            s = s + buft_ref[POOL_BUF - j, :, cols]
        r = s / float(w) - x
        z = _dot(r.astype(BF16), wp_ref[g]) * sc_ref[:, cols]
        z_ref[:, cols] = z.astype(BF16)
    for j in range(POOL_BUF - 1):
        buft_out_ref[j] = buft_ref[j + 1]
    buft_out_ref[POOL_BUF - 1] = u_ref[...]


def _pool1(u, buf_t, w_pool, scale):
    batch = u.shape[0]
    full2 = lambda i: (0, 0)
    full3 = lambda i: (0, 0, 0)
    return pl.pallas_call(
        _pool1_kernel,
        grid=(1,),
        in_specs=[
            pl.BlockSpec(u.shape, full2),
            pl.BlockSpec(buf_t.shape, full3),
            pl.BlockSpec(w_pool.shape, full3),
            pl.BlockSpec((1, POOL_W), full2),
        ],
        out_specs=[
            pl.BlockSpec((batch, POOL_W), full2),
            pl.BlockSpec(buf_t.shape, full3),
        ],
        out_shape=[
            jax.ShapeDtypeStruct((batch, POOL_W), BF16),
            jax.ShapeDtypeStruct(buf_t.shape, F32),
        ],
        compiler_params=_params(("arbitrary",)),
        name="pool1",
    )(u, buf_t, w_pool, scale)


def _mix_kernel(x_ref, hm_ref, zp_ref, wo_ref, g_ref, x1_ref, xnt_ref):
    mix = _dot(hm_ref[...], wo_ref[0:MLSTM_W, :]) + _dot(zp_ref[...], wo_ref[MLSTM_W:, :])
    x1 = x_ref[...] + mix
    x1_ref[...] = x1
    xnt_ref[...] = _rmsnorm(x1, g_ref[...]).T.astype(BF16)


def _mix(x, hm, zp, w_out, norm_g):
    n, d = x.shape
    tb = _token_block(n, 1024)
    rows = lambda i: (i, 0)
    const = lambda i: (0, 0)
    return pl.pallas_call(
        _mix_kernel,
        grid=(n // tb,),
        in_specs=[
            pl.BlockSpec((tb, d), rows),
            pl.BlockSpec((tb, MLSTM_W), rows),
            pl.BlockSpec((tb, POOL_W), rows),
            pl.BlockSpec(w_out.shape, const),
            pl.BlockSpec((1, d), const),
        ],
        out_specs=[
            pl.BlockSpec((tb, d), rows),
            pl.BlockSpec((d, tb), lambda i: (0, i)),
        ],
        out_shape=[
            jax.ShapeDtypeStruct((n, d), F32),
            jax.ShapeDtypeStruct((d, n), BF16),
        ],
        compiler_params=_params(("parallel",)),
        name="mix",
    )(x, hm, zp, w_out, norm_g)


def _extract_max(vals, pos):
    m = vals[0]
    for v in vals[1:]:
        m = jnp.maximum(m, v)
    m = jnp.max(m, axis=0, keepdims=True)
    big = jnp.float32(1e9)
    idx = None
    for v, p in zip(vals, pos):
        cand = jnp.where(v == m, p, big)
        idx = cand if idx is None else jnp.minimum(idx, cand)
    idx = jnp.min(idx, axis=0, keepdims=True)
    return m, idx


def _top16(s):
    tb = s.shape[1]
    row = lax.broadcasted_iota(jnp.int32, s.shape, 0).astype(F32)
    krow = lax.broadcasted_iota(jnp.int32, (PEER_TOPK, tb), 0)
    rank = jnp.full(s.shape, UNRANKED, F32)
    sv = jnp.zeros((PEER_TOPK, tb), F32)
    for k in range(PEER_TOPK):
        m, idx = _extract_max([s], [row])
        hit = row == idx
        rank = jnp.where(hit, float(k), rank)
        s = jnp.where(hit, NEG_INF, s)
        sv = jnp.where(krow == k, m, sv)
    return sv, rank


def _candidate_groups(sv1, sv2):
    tb = sv1.shape[1]
    vals, pos = [], []
    r8 = lax.broadcasted_iota(jnp.int32, (8, tb), 0)
    vals.append(sv1[0:1, :] + sv2[8:16, :])
    pos.append((r8 + 8).astype(F32))
    vals.append(sv1[0:1, :] + sv2[0:8, :])
    pos.append(r8.astype(F32))
    for k1 in range(1, 8):
        lim = PEER_TOPK // (k1 + 1)
        v = sv1[k1:k1 + 1, :] + sv2[0:8, :]
        vals.append(jnp.where(r8 < lim, v, NEG_INF))
        pos.append((r8 + k1 * PEER_TOPK).astype(F32))
    vals.append(sv1[8:16, :] + sv2[0:1, :])
    pos.append(((r8 + 8) * PEER_TOPK).astype(F32))
    return vals, pos


def _sorting_pairs(n):
    pairs = []
    p = 1
    while p < n:
        k = p
        while k >= 1:
            for j in range(k % p, n - k, 2 * k):
                for i in range(min(k, n - j - k)):
                    if (i + j) // (2 * p) == (i + j + k) // (2 * p):
                        pairs.append((i + j, i + j + k))
            k //= 2
        p *= 2
    return pairs


def _sublane_allreduce(x, op):
    for d in (1, 2, 4):
        x = op(x, pltpu.roll(x, d, axis=0))
    return x


def _sorted_top16(groups):
    g = list(groups)
    for i, j in _sorting_pairs(len(g)):
        g[i], g[j] = jnp.maximum(g[i], g[j]), jnp.minimum(g[i], g[j])
    n = len(g)
    for d in (1, 2, 4):
        p = [pltpu.roll(x, d, axis=0) for x in g]
        g = [jnp.maximum(g[i], p[n - 1 - i]) for i in range(n)]
        stride = n // 2
        while stride >= 1:
            for i in range(n):
                if i & stride == 0:
                    g[i], g[i + stride] = (jnp.maximum(g[i], g[i + stride]),
                                           jnp.minimum(g[i], g[i + stride]))
            stride //= 2
    return g


def _rank_bits(x, sv):
    c8 = x < sv[7]
    t = jnp.where(c8, sv[11], sv[3])
    c4 = x < t
    t = jnp.where(c8, jnp.where(c4, sv[13], sv[9]), jnp.where(c4, sv[5], sv[1]))
    c2 = x < t
    t = jnp.where(c8,
                  jnp.where(c4, jnp.where(c2, sv[14], sv[12]), jnp.where(c2, sv[10], sv[8])),
                  jnp.where(c4, jnp.where(c2, sv[6], sv[4]), jnp.where(c2, sv[2], sv[0])))
    c1 = x < t
    return (c8, c4, c2, c1), x < sv[15]


def _select16(bits, rows):
    c8, c4, c2, c1 = bits
    lvl = [jnp.where(c1, rows[2 * i + 1], rows[2 * i]) for i in range(8)]
    lvl = [jnp.where(c2, lvl[2 * i + 1], lvl[2 * i]) for i in range(4)]
    lvl = [jnp.where(c4, lvl[2 * i + 1], lvl[2 * i]) for i in range(2)]
    return jnp.where(c8, lvl[1], lvl[0])


def _route_fast(s1, s2):
    tb = s1.shape[1]
    r8 = lax.broadcasted_iota(jnp.int32, (8, tb), 0)
    n_grp = PEER_NKEYS // 8
    g1 = [s1[v * 8:(v + 1) * 8, :] for v in range(n_grp)]
    g2 = [s2[v * 8:(v + 1) * 8, :] for v in range(n_grp)]
    sv1 = _sorted_top16(g1)
    sv2 = _sorted_top16(g2)

    bad = jnp.zeros((8, tb), F32)
    for sv, grp in ((sv1, g1), (sv2, g2)):
        for k in range(PEER_TOPK - 1):
            bad = jnp.where(sv[k] == sv[k + 1], 1.0, bad)
        n_in = jnp.zeros((8, tb), F32)
        for x in grp:
            n_in = n_in + jnp.where(x >= sv[PEER_TOPK - 1], 1.0, 0.0)
        n_in = _sublane_allreduce(n_in, jnp.add)
        bad = jnp.where(n_in != float(PEER_TOPK), 1.0, bad)

    def by_sublane(rows):
        out = jnp.zeros((8, tb), F32)
        for k, row in enumerate(rows):
            out = jnp.where(r8 == k, row, out)
        return out
    a2_lo, a2_hi, a1_hi = by_sublane(sv2[0:8]), by_sublane(sv2[8:16]), by_sublane(sv1[8:16])
    orig = [sv1[0] + a2_hi, sv1[0] + a2_lo]
    for k1 in range(1, 8):
        orig.append(jnp.where(r8 < PEER_TOPK // (k1 + 1), sv1[k1] + a2_lo, NEG_INF))
    orig.append(a1_hi + sv2[0])
    vals = list(orig)
    maxima = []
    for _ in range(PEER_TOPK):
        m = vals[0]
        for v in vals[1:]:
            m = jnp.maximum(m, v)
        m = _sublane_allreduce(m, jnp.maximum)
        vals = [jnp.where(v == m, NEG_INF, v) for v in vals]
        maxima.append(m)

    def count_ge(thr):
        n = jnp.zeros((8, tb), F32)
        for o in orig:
            n = n + jnp.where(o >= thr, 1.0, 0.0)
        return _sublane_allreduce(n, jnp.add)
    n_last = count_ge(maxima[-1])
    n_prev = count_ge(maxima[-2])
    use_last = n_last == float(PEER_TOPK)
    thr = jnp.where(use_last, maxima[-1], maxima[-2])
    bad = jnp.where(use_last | (n_prev == float(PEER_TOPK)), bad, 1.0)
    sel = [o >= thr for o in orig]
    cmax = sv1[0] + sv2[0]
    z = jnp.zeros((8, tb), F32)
    for o, sl in zip(orig, sel):
        z = z + jnp.where(sl, jnp.exp(o - cmax), 0.0)
    z = _sublane_allreduce(z, jnp.add)
    ones = [jnp.where(sl, 1.0, 0.0) for sl in sel]
    cnt = [_sublane_allreduce(ones[0] + ones[1], jnp.add)]
    cnt += [_sublane_allreduce(ones[k1 + 1], jnp.add) for k1 in range(1, 8)]
    cnt += [ones[9][j:j + 1, :] for j in range(8)]

    cnt1, r2 = [], []
    weights = (8.0, 4.0, 2.0, 1.0)
    for x in g1:
        bits, below = _rank_bits(x, sv1)
        cnt1.append(jnp.where(below, 0.0, _select16(bits, cnt)))
    for x in g2:
        bits, below = _rank_bits(x, sv2)
        rank = jnp.zeros((8, tb), F32)
        for b, wgt in zip(bits, weights):
            rank = rank + jnp.where(b, wgt, 0.0)
        r2.append(jnp.where(below, UNRANKED, rank))
    return cnt1, r2, sv1[0][0:1, :], sv2[0][0:1, :], z[0:1, :], bad


def _route_exact(s1, s2):
    sv1, r1 = _top16(s1)
    sv2, r2 = _top16(s2)

    vals, pos = _candidate_groups(sv1, sv2)
    orig = list(vals)
    sel = [jnp.zeros(v.shape, F32) for v in vals]
    for _ in range(PEER_TOPK):
        _, idx = _extract_max(vals, pos)
        for i in range(len(vals)):
            hit = pos[i] == idx
            sel[i] = jnp.where(hit, 1.0, sel[i])
            vals[i] = jnp.where(hit, NEG_INF, vals[i])

    cmax = orig[1][0:1, :]
    z = None
    for o, sl in zip(orig, sel):
        part = jnp.sum(jnp.where(sl > 0.0, jnp.exp(o - cmax), 0.0), axis=0, keepdims=True)
        z = part if z is None else z + part
    cnt = [jnp.sum(sel[0] + sel[1], axis=0, keepdims=True)]
    cnt += [jnp.sum(sel[k1 + 1], axis=0, keepdims=True) for k1 in range(1, 8)]
    cnt += [sel[9][j:j + 1, :] for j in range(8)]

    cnt1 = jnp.zeros(r1.shape, F32)
    for k1 in range(PEER_TOPK):
        cnt1 = jnp.where(r1 == float(k1), cnt[k1], cnt1)
    return cnt1, r2, sv1[0:1, :], sv2[0:1, :], z


ROUTE_HEADS_PER_STEP = 4


def _route_kernel(xnt_ref, wqt_ref, k1_ref, k2_ref, cnt1_ref, g1_ref, r2_ref, e2_ref, s_ref):
    qt_all = _dot(wqt_ref[...], xnt_ref[...]).astype(BF16)

    def emit_gates(hh, s1, s2, max1, max2, z):
        g1_ref[hh] = jnp.exp(s1 - max1) * (0.5 / z)
        e2_ref[hh] = jnp.exp(s2 - max2).astype(BF16)

    ties = []
    for hh in range(ROUTE_HEADS_PER_STEP):
        qt = qt_all[hh * 2 * PEER_HALF:(hh + 1) * 2 * PEER_HALF, :]
        s1 = _dot(k1_ref[hh], qt[0:PEER_HALF, :])
        s2 = _dot(k2_ref[hh], qt[PEER_HALF:, :])
        s_ref[hh, 0] = s1
        s_ref[hh, 1] = s2
        cnt1, r2, max1, max2, z, tie = _route_fast(s1, s2)
        for v in range(PEER_NKEYS // 8):
            cnt1_ref[hh, v * 8:(v + 1) * 8, :] = cnt1[v]
            r2_ref[hh, v * 8:(v + 1) * 8, :] = r2[v].astype(BF16)
        emit_gates(hh, s1, s2, max1, max2, z)
        ties.append(jnp.max(tie))

    for hh in range(ROUTE_HEADS_PER_STEP):
        @pl.when(ties[hh] > 0.0)
        def _(hh=hh):
            s1, s2 = s_ref[hh, 0], s_ref[hh, 1]
            cnt1, r2, max1, max2, z = _route_exact(s1, s2)
            cnt1_ref[hh] = cnt1
            r2_ref[hh] = r2.astype(BF16)
            emit_gates(hh, s1, s2, max1, max2, z)


def _route(xnt, wqt, keys):
    d, n = xnt.shape
    tb = _token_block(n, 256)
    hps = ROUTE_HEADS_PER_STEP
    per_head = pl.BlockSpec((hps, PEER_NKEYS, tb), lambda i, h: (h, 0, i))
    shape = lambda dt: jax.ShapeDtypeStruct((PEER_HEADS, PEER_NKEYS, n), dt)
    return pl.pallas_call(
        _route_kernel,
        grid=(n // tb, PEER_HEADS // hps),
        in_specs=[
            pl.BlockSpec((d, tb), lambda i, h: (0, i)),
            pl.BlockSpec((hps * 2 * PEER_HALF, d), lambda i, h: (h, 0)),
            pl.BlockSpec((hps, PEER_NKEYS, PEER_HALF), lambda i, h: (h, 0, 0)),
            pl.BlockSpec((hps, PEER_NKEYS, PEER_HALF), lambda i, h: (h + PEER_HEADS // hps, 0, 0)),
        ],
        out_specs=[per_head] * 4,
        out_shape=[shape(F32), shape(F32), shape(BF16), shape(BF16)],
        scratch_shapes=[pltpu.VMEM((hps, 2, PEER_NKEYS, tb), F32)],
        compiler_params=_params(("parallel", "parallel")),
        name="route",
    )(xnt, wqt, keys, keys)


I1_PER_TILE = 16
BF16_TILE = (8, 2 * LANE)
EXPERT_TILE = I1_PER_TILE * PEER_NKEYS


def _experts_kernel(xnt_ref, cnt1_ref, g1_ref, r2_ref, e2_ref, u_ref, vt_ref, x1_ref,
                    x2_ref, acc_ref, act_ref, wa_ref):
    e = pl.program_id(1)
    tb = xnt_ref.shape[1]

    @pl.when(e == 0)
    def _():
        acc_ref[...] = jnp.zeros_like(acc_ref)

    a = _dot(u_ref[...], xnt_ref[...]).astype(BF16)
    act_ref[...] = a * (1.0 + jnp.tanh(a * (GELU_C0 + GELU_C1 * (a * a))))
    sub, width = BF16_TILE[0], min(BF16_TILE[1], tb)
    assert I1_PER_TILE % sub == 0
    i1_rows = pl.ds(pl.multiple_of(e * I1_PER_TILE, I1_PER_TILE), I1_PER_TILE)
    for lt in range(tb // width):
        lanes = pl.ds(lt * width, width)
        cnt_tile = [cnt1_ref[h, i1_rows, lanes] for h in range(PEER_HEADS)]
        g1_tile = [g1_ref[h, i1_rows, lanes] for h in range(PEER_HEADS)]
        for j in range(I1_PER_TILE):
            cnt_b = [jnp.broadcast_to(t[j:j + 1, :], (sub, width)).astype(BF16) for t in cnt_tile]
            g1_b = [jnp.broadcast_to(t[j:j + 1, :], (sub, width)).astype(BF16) for t in g1_tile]
            for rb in range(PEER_NKEYS // sub):
                krows = pl.ds(rb * sub, sub)
                w = None
                for h in range(PEER_HEADS):
                    mask = r2_ref[h, krows, lanes] < cnt_b[h]
                    term = jnp.where(mask, e2_ref[h, krows, lanes] * g1_b[h], jnp.zeros((), BF16))
                    w = term if w is None else w + term
                erows = pl.ds(j * PEER_NKEYS + rb * sub, sub)
                wa_ref[erows, lanes] = w * act_ref[erows, lanes]
    acc_ref[...] += _dot(vt_ref[...], wa_ref[...])

    @pl.when(e == pl.num_programs(1) - 1)
    def _():
        x2_ref[...] = x1_ref[...] + acc_ref[...].T


def _experts(xnt, cnt1, g1, r2, e2, u_tabs, vt_tabs, layer, x1):
    d, n = xnt.shape
    n_exp = u_tabs.shape[1]
    tb = _token_block(n, 512)
    assert tb % LANE == 0 and n_exp % EXPERT_TILE == 0
    per_head = pl.BlockSpec((PEER_HEADS, PEER_NKEYS, tb), lambda i, e: (0, 0, i))
    tile_buf = pltpu.VMEM((EXPERT_TILE, tb), BF16)
    return pl.pallas_call(
        _experts_kernel,
        grid=(n // tb, n_exp // EXPERT_TILE),
        in_specs=[
            pl.BlockSpec((d, tb), lambda i, e: (0, i)),
            per_head, per_head, per_head, per_head,
            pl.BlockSpec((None, EXPERT_TILE, d), lambda i, e: (layer, e, 0)),
            pl.BlockSpec((None, d, EXPERT_TILE), lambda i, e: (layer, 0, e)),
            pl.BlockSpec((tb, d), lambda i, e: (i, 0)),
        ],
        out_specs=pl.BlockSpec((tb, d), lambda i, e: (i, 0)),
        out_shape=jax.ShapeDtypeStruct((n, d), F32),
        scratch_shapes=[pltpu.VMEM((d, tb), F32), tile_buf, tile_buf],
        compiler_params=_params(("parallel", "arbitrary")),
        name="experts",
    )(xnt, cnt1, g1, r2, e2, u_tabs, vt_tabs, x1)


def _ple_kernel(x_ref, p_ref, wg_ref, wp_ref, nf_ref, y_ref, *, final_norm):
    x = x_ref[...]
    gate = _sigmoid(_dot(x.astype(BF16), wg_ref[...]))
    y = x + gate * _dot(p_ref[...].astype(BF16), wp_ref[...])
    if final_norm:
        y = _rmsnorm(y, nf_ref[...])
    y_ref[...] = y


def _ple(x, p_all, layer, w_gate, w_ple, norm_f, final_norm):
    n, d = x.shape
    tb = _token_block(n, 1024)
    rows = lambda i: (i, 0)
    const = lambda i: (0, 0)
    return pl.pallas_call(
        functools.partial(_ple_kernel, final_norm=final_norm),
        grid=(n // tb,),
        in_specs=[
            pl.BlockSpec((tb, d), rows),
            pl.BlockSpec((None, tb, p_all.shape[2]), lambda i: (layer, i, 0)),
            pl.BlockSpec(w_gate.shape, const),
            pl.BlockSpec(w_ple.shape, const),
            pl.BlockSpec((1, d), const),
        ],
        out_specs=pl.BlockSpec((tb, d), rows),
        out_shape=jax.ShapeDtypeStruct((n, d), F32),
        compiler_params=_params(("parallel",)),
        name="ple",
    )(x, p_all, w_gate, w_ple, norm_f)


def _prep_layer_weights(l, w_in, b_gate, mlstm_norm, w_pool, pool_scale, w_out, norm1, norm2,
                        peer_wq, peer_keys, peer_u, peer_v, w_ple, w_gate):
    w = MLSTM_W
    wi = w_in[l]
    d = wi.shape[0]
    wa = jnp.concatenate([wi[:, 0:w], wi[:, 2 * w:4 * w + POOL_W]], axis=1).astype(BF16)
    wkt = wi[:, w:2 * w].T.astype(BF16)
    wgt = wi[:, 4 * w + POOL_W:].T.astype(BF16)
    return dict(
        wa=wa, wkt=wkt, wgt=wgt,
        bg=b_gate[l].reshape(2 * HEADS, 1).astype(F32),
        mlstm_norm=mlstm_norm[l].reshape(1, w),
        w_pool=w_pool[l].astype(BF16),
        pool_scale=pool_scale[l].reshape(1, POOL_W),
        w_out=w_out[l].astype(BF16),
        norm1=norm1[l].reshape(1, d), norm2=norm2[l].reshape(1, d),
        wqt=peer_wq[l].T.astype(BF16),
        keys=peer_keys[l].reshape(2 * PEER_HEADS, PEER_NKEYS, PEER_HALF).astype(BF16),
        w_ple=w_ple[l].astype(BF16), w_gate=w_gate[l].astype(BF16),
        layer=l,
    )


def _ffn_and_embed(x1, xnt, p, lw, norm_f, final_norm):
    cnt1, g1, r2, e2 = _route(xnt, lw["wqt"], lw["keys"])
    x2 = _experts(xnt, cnt1, g1, r2, e2, lw["u_tabs"], lw["vt_tabs"], lw["layer"], x1)
    return _ple(x2, p, lw["layer"], lw["w_gate"], lw["w_ple"], norm_f, final_norm)


def _prompt_layer(x, p, lw, norm_f, final_norm, batch, t_len):
    q, kt, v, o, u, gt = _proj(x, lw["norm1"], lw["wa"], lw["wkt"], lw["wgt"], lw["bg"])
    gates3 = gt.reshape(2 * HEADS, (batch * t_len) // CHUNK, CHUNK)
    hm, cext, m8 = _mlstm(q, kt, v, o, gates3, lw["mlstm_norm"], batch, t_len)
    zp, buf = _pool(u, lw["w_pool"], lw["pool_scale"], batch, t_len)
    x1, xnt = _mix(x, hm, zp, lw["w_out"], lw["norm2"])
    y = _ffn_and_embed(x1, xnt, p, lw, norm_f, final_norm)
    return y, cext[..., :HEAD_DIM], cext[..., HEAD_DIM], m8[..., 0, 0], buf


def _sample_layer(x, p, c_all, n0, m0, buf0, lw, norm_f, final_norm):
    q, kt, v, o, u, gt = _proj(x, lw["norm1"], lw["wa"], lw["wkt"], lw["wgt"], lw["bg"])
    hm, c1, n1, m1 = _mlstm1(q, kt.T, v, o, gt.T, c_all, lw["layer"], n0.reshape(-1, MLSTM_W), m0,
                             lw["mlstm_norm"])
    n1 = n1.reshape(n0.shape)
    zp, buf_t = _pool1(u, jnp.swapaxes(buf0, 0, 1), lw["w_pool"], lw["pool_scale"])
    x1, xnt = _mix(x, hm, zp, lw["w_out"], lw["norm2"])
    y = _ffn_and_embed(x1, xnt, p, lw, norm_f, final_norm)
    return y, c1, n1, m1, jnp.swapaxes(buf_t, 0, 1)


def kernel(x_prompt, x_sample, p_prompt, p_sample, state_C, state_n, state_m, state_pool,
           w_in, b_gate, mlstm_norm, w_pool, pool_scale, w_out, norm1, norm2,
           peer_wq, peer_keys, peer_u, peer_v, w_ple, w_gate, norm_f):
    depth = w_in.shape[0]
    batch, t_len, d = x_prompt.shape
    dec_batch, dec_len, _ = x_sample.shape
    assert dec_len == 1 and t_len % CHUNK == 0
    nf = norm_f.reshape(1, d)

    xp = x_prompt.reshape(batch * t_len, d)
    xs = x_sample.reshape(dec_batch, d)
    outs_p, outs_s = [], []
    u_tabs = peer_u.astype(BF16)
    vt_tabs = jnp.swapaxes(peer_v, 1, 2).astype(BF16)
    for l in range(depth):
        lw = _prep_layer_weights(l, w_in, b_gate, mlstm_norm, w_pool, pool_scale, w_out, norm1,
                                 norm2, peer_wq, peer_keys, peer_u, peer_v, w_ple, w_gate)
        lw.update(u_tabs=u_tabs, vt_tabs=vt_tabs)
        final = l == depth - 1
        pp = p_prompt.reshape(depth, batch * t_len, -1)
        ps = p_sample.reshape(depth, dec_batch, -1)
        xp, c_p, n_p, m_p, buf_p = _prompt_layer(xp, pp, lw, nf, final, batch, t_len)
        xs, c_s, n_s, m_s, buf_s = _sample_layer(xs, ps, state_C, state_n[l], state_m[l],
                                                 state_pool[l], lw, nf, final)
        outs_p.append((c_p, n_p, m_p, buf_p))
        outs_s.append((c_s, n_s, m_s, buf_s))

    stack = lambda outs, i: jnp.stack([o[i] for o in outs])
    return (xp.reshape(batch, t_len, d), xs.reshape(dec_batch, dec_len, d),
            stack(outs_p, 0), stack(outs_p, 1), stack(outs_p, 2), stack(outs_p, 3),
            stack(outs_s, 0), stack(outs_s, 1), stack(outs_s, 2), stack(outs_s, 3))
```

```python
import functools
import math

import jax
import jax.numpy as jnp
from jax import lax
from jax.experimental import pallas as pl
from jax.experimental.pallas import tpu as pltpu

F32 = jnp.float32
BF16 = jnp.bfloat16
EPS = 1e-6
NEG_INF = float("-inf")

LANE = 128
V7X_VMEM_LIMIT = 56 * 1024 * 1024

HEADS = 4
HEAD_DIM = 128
MLSTM_W = HEADS * HEAD_DIM
POOL_WINDOWS = (2, 4, 8, 16)
POOL_GROUP = 128
POOL_W = POOL_GROUP * len(POOL_WINDOWS)
POOL_BUF = max(POOL_WINDOWS) - 1
CHUNK = 128
PEER_HEADS = 8
PEER_NKEYS = 128
PEER_HALF = 128
PEER_TOPK = 16
UNRANKED = 127.0
GELU_C0 = math.sqrt(2.0 / math.pi)
GELU_C1 = 0.044715 * GELU_C0


def _params(semantics):
    return pltpu.CompilerParams(dimension_semantics=semantics, vmem_limit_bytes=V7X_VMEM_LIMIT)


def _token_block(n, want):
    tb = min(n, want)
    assert n % tb == 0
    return tb


def _rmsnorm(x, g):
    return x * lax.rsqrt(jnp.mean(x * x, axis=-1, keepdims=True) + EPS) * g


def _log_sigmoid(x):
    return jnp.minimum(x, 0.0) - jnp.log(1.0 + jnp.exp(-jnp.abs(x)))


def _sigmoid(x):
    return 1.0 / (1.0 + jnp.exp(-x))


def _dot(a, b):
    return jnp.dot(a, b, preferred_element_type=F32)


def _dot_nt(a, b):
    return lax.dot_general(a, b, (((1,), (1,)), ((), ())), preferred_element_type=F32)


def _proj_kernel(x_ref, g_ref, wa_ref, wkt_ref, wgt_ref, bg_ref,
                 q_ref, kt_ref, v_ref, o_ref, u_ref, gt_ref):
    hn = _rmsnorm(x_ref[...], g_ref[...]).astype(BF16)
    pa = _dot(hn, wa_ref[...])
    w = MLSTM_W
    q_ref[...] = pa[:, 0:w].astype(BF16)
    v_ref[...] = pa[:, w:2 * w].astype(BF16)
    o_ref[...] = _sigmoid(pa[:, 2 * w:3 * w]).astype(BF16)
    u_ref[...] = pa[:, 3 * w:4 * w]
    kt = _dot_nt(wkt_ref[...], hn) * (HEAD_DIM ** -0.5)
    kt_ref[...] = kt.astype(BF16)
    gt_ref[...] = _dot_nt(wgt_ref[...], hn) + bg_ref[...]


def _proj(x, norm_g, wa, wkt, wgt, bg):
    n, d = x.shape
    tb = _token_block(n, 1024)
    grid = (n // tb,)
    const = lambda i: (0, 0)
    return pl.pallas_call(
        _proj_kernel,
        grid=grid,
        in_specs=[
            pl.BlockSpec((tb, d), lambda i: (i, 0)),
            pl.BlockSpec((1, d), const),
            pl.BlockSpec(wa.shape, const),
            pl.BlockSpec(wkt.shape, const),
            pl.BlockSpec(wgt.shape, const),
            pl.BlockSpec(bg.shape, const),
        ],
        out_specs=[
            pl.BlockSpec((tb, MLSTM_W), lambda i: (i, 0)),
            pl.BlockSpec((MLSTM_W, tb), lambda i: (0, i)),
            pl.BlockSpec((tb, MLSTM_W), lambda i: (i, 0)),
            pl.BlockSpec((tb, MLSTM_W), lambda i: (i, 0)),
            pl.BlockSpec((tb, POOL_W), lambda i: (i, 0)),
            pl.BlockSpec((2 * HEADS, tb), lambda i: (0, i)),
        ],
        out_shape=[
            jax.ShapeDtypeStruct((n, MLSTM_W), BF16),
            jax.ShapeDtypeStruct((MLSTM_W, n), BF16),
            jax.ShapeDtypeStruct((n, MLSTM_W), BF16),
            jax.ShapeDtypeStruct((n, MLSTM_W), BF16),
            jax.ShapeDtypeStruct((n, POOL_W), F32),
            jax.ShapeDtypeStruct((2 * HEADS, n), F32),
        ],
        compiler_params=_params(("parallel",)),
        name="proj",
    )(x, norm_g, wa, wkt, wgt, bg)


def _lane_cumsum(x):
    lane = lax.broadcasted_iota(jnp.int32, x.shape, 1)
    shift = 1
    while shift < x.shape[1]:
        x = x + jnp.where(lane >= shift, pltpu.roll(x, shift, axis=1), 0.0)
        shift *= 2
    return x


def _mlstm_kernel(q_ref, kt_ref, v_ref, o_ref, ig_ref, lf_ref, nw_ref,
                  hm_ref, cext_out_ref, m_out_ref, cext_ref):
    t_len = q_ref.shape[0]
    n_chunks = t_len // CHUNK
    ig2 = ig_ref[0]
    lf2 = _log_sigmoid(lf_ref[0])
    b2 = _lane_cumsum(lf2)
    a2 = ig2 - b2

    t_idx = lax.broadcasted_iota(jnp.int32, (CHUNK, CHUNK), 0)
    s_idx = lax.broadcasted_iota(jnp.int32, (CHUNK, CHUNK), 1)
    causal = s_idx <= t_idx
    ones_col = (lax.broadcasted_iota(jnp.int32, (CHUNK, HEAD_DIM), 1) == 0).astype(BF16)
    nw = nw_ref[...]

    cext_ref[...] = jnp.zeros_like(cext_ref)
    m = jnp.zeros((1, 1), F32)
    for c in range(n_chunks):
        rows = pl.ds(c * CHUNK, CHUNK)
        a_row = a2[c:c + 1, :]
        lf_row = lf2[c:c + 1, :]
        b_last = b2[c:c + 1, CHUNK - 1:CHUNK]
        b_col = jnp.sum(jnp.where(causal, lf_row, 0.0), axis=1, keepdims=True)
        amax_col = jnp.max(jnp.where(causal, a_row, NEG_INF), axis=1, keepdims=True)
        m_col = jnp.maximum(amax_col, m)
        decay_mat = jnp.where(causal, jnp.exp(a_row - m_col), 0.0)

        qc = q_ref[rows, :]
        ktc = kt_ref[:, rows]
        v_ext = jnp.concatenate([v_ref[rows, :], ones_col], axis=1)
        s = _dot(qc, ktc) * decay_mat
        intra = _dot(s.astype(BF16), v_ext)
        inter = _dot(qc, cext_ref[...].astype(BF16))
        tot = intra + jnp.exp(m - m_col) * inter
        num = tot[:, :HEAD_DIM]
        den = tot[:, HEAD_DIM:HEAD_DIM + 1]
        h = num / jnp.maximum(jnp.abs(den), jnp.exp(-(b_col + m_col)))
        h = h * lax.rsqrt(jnp.mean(h * h, axis=1, keepdims=True) + EPS)
        hm_ref[rows, :] = (h * nw * o_ref[rows, :]).astype(BF16)

        m_new = jnp.maximum(b_last + m, jnp.max(a_row, axis=1, keepdims=True) + b_last)
        ws_row = jnp.exp(a_row + (b_last - m_new))
        kws = (ktc.astype(F32) * ws_row).astype(BF16)
        cext_ref[...] = jnp.exp(b_last + m - m_new) * cext_ref[...] + _dot(kws, v_ext)
        m = m_new

    cext_out_ref[0, 0] = cext_ref[...]
    m_out_ref[0, 0] = jnp.broadcast_to(m, m_out_ref.shape[2:])


def _mlstm(q, kt, v, o, gates3, norm_w, batch, t_len):
    n = q.shape[0]
    n_chunks = t_len // CHUNK
    seq = lambda b, h: (b, h)
    return pl.pallas_call(
        _mlstm_kernel,
        grid=(batch, HEADS),
        in_specs=[
            pl.BlockSpec((t_len, HEAD_DIM), seq),
            pl.BlockSpec((HEAD_DIM, t_len), lambda b, h: (h, b)),
            pl.BlockSpec((t_len, HEAD_DIM), seq),
            pl.BlockSpec((t_len, HEAD_DIM), seq),
            pl.BlockSpec((1, n_chunks, CHUNK), lambda b, h: (h, b, 0)),
            pl.BlockSpec((1, n_chunks, CHUNK), lambda b, h: (h + HEADS, b, 0)),
            pl.BlockSpec((1, HEAD_DIM), lambda b, h: (0, h)),
        ],
        out_specs=[
            pl.BlockSpec((t_len, HEAD_DIM), seq),
            pl.BlockSpec((1, 1, HEAD_DIM, 2 * HEAD_DIM), lambda b, h: (b, h, 0, 0)),
            pl.BlockSpec((1, 1, 8, LANE), lambda b, h: (b, h, 0, 0)),
        ],
        out_shape=[
            jax.ShapeDtypeStruct((n, MLSTM_W), BF16),
            jax.ShapeDtypeStruct((batch, HEADS, HEAD_DIM, 2 * HEAD_DIM), F32),
            jax.ShapeDtypeStruct((batch, HEADS, 8, LANE), F32),
        ],
        scratch_shapes=[pltpu.VMEM((HEAD_DIM, 2 * HEAD_DIM), F32)],
        compiler_params=_params(("parallel", "parallel")),
        name="mlstm",
    )(q, kt, v, o, gates3, gates3, norm_w)


def _mlstm1_kernel(q_ref, k_ref, v_ref, o_ref, g_ref, c_ref, n_ref, m_ref, nw_ref,
                   hm_ref, c_out_ref, n_out_ref, m_out_ref):
    bb = q_ref.shape[0]
    row = lax.broadcasted_iota(jnp.int32, (bb, HEAD_DIM), 0)
    eye = (lax.broadcasted_iota(jnp.int32, (HEAD_DIM, HEAD_DIM), 0)
           == lax.broadcasted_iota(jnp.int32, (HEAD_DIM, HEAD_DIM), 1))
    lane_h = lax.broadcasted_iota(jnp.int32, (bb, HEADS), 1)
    g = g_ref[...]
    m_all = m_ref[...]
    m_out = jnp.zeros((bb, HEADS), F32)
    for h in range(HEADS):
        cols = slice(h * HEAD_DIM, (h + 1) * HEAD_DIM)
        qb = q_ref[:, cols]
        qf = qb.astype(F32)
        kf = k_ref[:, cols].astype(F32)
        vf = v_ref[:, cols].astype(F32)
        ig = g[:, h:h + 1]
        lf = _log_sigmoid(g[:, HEADS + h:HEADS + h + 1])
        m_old = m_all[:, h:h + 1]
        n_old = n_ref[:, cols]
        m_new = jnp.maximum(ig, lf + m_old)
        w_in = jnp.exp(ig - m_new)
        w_st = jnp.exp(lf + m_old - m_new)
        qc = jnp.zeros((bb, HEAD_DIM), F32)
        for j in range(bb):
            c_old = c_ref[j, h]
            res = _dot(qb, c_old.astype(BF16))
            qc = jnp.where(row == j, res, qc)
            k_col = jnp.sum(jnp.where(eye, kf[j:j + 1, :], 0.0), axis=1, keepdims=True)
            c_out_ref[j, h] = w_st[j:j + 1, :] * c_old + k_col * (w_in[j:j + 1, :] * vf[j:j + 1, :])
        s = jnp.sum(qf * kf, axis=1, keepdims=True) * w_in
        num = s * vf + w_st * qc
        den = s + w_st * jnp.sum(qf * n_old, axis=1, keepdims=True)
        hh = num / jnp.maximum(jnp.abs(den), jnp.exp(-m_new))
        hh = hh * lax.rsqrt(jnp.mean(hh * hh, axis=1, keepdims=True) + EPS)
        hm_ref[:, cols] = (hh * nw_ref[:, cols] * o_ref[:, cols]).astype(BF16)
        n_out_ref[:, cols] = w_st * n_old + w_in * kf
        m_out = jnp.where(lane_h == h, m_new, m_out)
    m_out_ref[...] = m_out


def _mlstm1(q, k, v, o, g, c_all, layer, n0, m0, norm_w):
    batch = q.shape[0]
    bb = _token_block(batch, 16)
    rows = lambda i: (i, 0)
    return pl.pallas_call(
        _mlstm1_kernel,
        grid=(batch // bb,),
        in_specs=[
            pl.BlockSpec((bb, MLSTM_W), rows),
            pl.BlockSpec((bb, MLSTM_W), rows),
            pl.BlockSpec((bb, MLSTM_W), rows),
            pl.BlockSpec((bb, MLSTM_W), rows),
            pl.BlockSpec((bb, 2 * HEADS), rows),
            pl.BlockSpec((None, bb, HEADS, HEAD_DIM, HEAD_DIM), lambda i: (layer, i, 0, 0, 0)),
            pl.BlockSpec((bb, MLSTM_W), rows),
            pl.BlockSpec((bb, HEADS), rows),
            pl.BlockSpec((1, MLSTM_W), lambda i: (0, 0)),
        ],
        out_specs=[
            pl.BlockSpec((bb, MLSTM_W), rows),
            pl.BlockSpec((bb, HEADS, HEAD_DIM, HEAD_DIM), lambda i: (i, 0, 0, 0)),
            pl.BlockSpec((bb, MLSTM_W), rows),
            pl.BlockSpec((bb, HEADS), rows),
        ],
        out_shape=[
            jax.ShapeDtypeStruct((batch, MLSTM_W), BF16),
            jax.ShapeDtypeStruct(c_all.shape[1:], F32),
            jax.ShapeDtypeStruct(n0.shape, F32),
            jax.ShapeDtypeStruct(m0.shape, F32),
        ],
        compiler_params=_params(("parallel",)),
        name="mlstm1",
    )(q, k, v, o, g, c_all, n0, m0, norm_w)


def _pool_kernel(u_ref, wp_ref, sc_ref, z_ref, buf_ref):
    t_len = u_ref.shape[0]
    t_idx = lax.broadcasted_iota(jnp.int32, (t_len, POOL_GROUP), 0)
    for g, w in enumerate(POOL_WINDOWS):
        cols = slice(g * POOL_GROUP, (g + 1) * POOL_GROUP)
        x = u_ref[:, cols]
        s = x
        k = 1
        while k < w:
            s = s + jnp.where(t_idx >= k, pltpu.roll(s, k, axis=0), 0.0)
            k *= 2
        cnt = jnp.minimum(t_idx + 1, w).astype(F32)
        r = s / cnt - x
        z = _dot(r.astype(BF16), wp_ref[g]) * sc_ref[:, cols]
        z_ref[:, cols] = z.astype(BF16)
    buf_ref[0] = u_ref[t_len - POOL_BUF:t_len, :]


def _pool(u, w_pool, scale, batch, t_len):
    n = u.shape[0]
    return pl.pallas_call(
        _pool_kernel,
        grid=(batch,),
        in_specs=[
            pl.BlockSpec((t_len, POOL_W), lambda b: (b, 0)),
            pl.BlockSpec(w_pool.shape, lambda b: (0, 0, 0)),
            pl.BlockSpec((1, POOL_W), lambda b: (0, 0)),
        ],
        out_specs=[
            pl.BlockSpec((t_len, POOL_W), lambda b: (b, 0)),
            pl.BlockSpec((1, POOL_BUF, POOL_W), lambda b: (b, 0, 0)),
        ],
        out_shape=[
            jax.ShapeDtypeStruct((n, POOL_W), BF16),
            jax.ShapeDtypeStruct((batch, POOL_BUF, POOL_W), F32),
        ],
        compiler_params=_params(("parallel",)),
        name="pool",
    )(u, w_pool, scale)


def _pool1_kernel(u_ref, buft_ref, wp_ref, sc_ref, z_ref, buft_out_ref):
    for g, w in enumerate(POOL_WINDOWS):
        cols = slice(g * POOL_GROUP, (g + 1) * POOL_GROUP)
        x = u_ref[:, cols]
        s = x
        for j in range(1, w):
            s = s + buft_ref[POOL_BUF - j, :, cols]
        r = s / float(w) - x
        z = _dot(r.astype(BF16), wp_ref[g]) * sc_ref[:, cols]
        z_ref[:, cols] = z.astype(BF16)
    for j in range(POOL_BUF - 1):
        buft_out_ref[j] = buft_ref[j + 1]
    buft_out_ref[POOL_BUF - 1] = u_ref[...]


def _pool1(u, buf_t, w_pool, scale):
    batch = u.shape[0]
    full2 = lambda i: (0, 0)
    full3 = lambda i: (0, 0, 0)
    return pl.pallas_call(
        _pool1_kernel,
        grid=(1,),
        in_specs=[
            pl.BlockSpec(u.shape, full2),
            pl.BlockSpec(buf_t.shape, full3),
            pl.BlockSpec(w_pool.shape, full3),
            pl.BlockSpec((1, POOL_W), full2),
        ],
        out_specs=[
            pl.BlockSpec((batch, POOL_W), full2),
            pl.BlockSpec(buf_t.shape, full3),
        ],
        out_shape=[
            jax.ShapeDtypeStruct((batch, POOL_W), BF16),
            jax.ShapeDtypeStruct(buf_t.shape, F32),
        ],
        compiler_params=_params(("arbitrary",)),
        name="pool1",
    )(u, buf_t, w_pool, scale)


def _mix_kernel(x_ref, hm_ref, zp_ref, wo_ref, g_ref, x1_ref, xnt_ref):
    mix = _dot(hm_ref[...], wo_ref[0:MLSTM_W, :]) + _dot(zp_ref[...], wo_ref[MLSTM_W:, :])
    x1 = x_ref[...] + mix
    x1_ref[...] = x1
    xnt_ref[...] = _rmsnorm(x1, g_ref[...]).T.astype(BF16)


def _mix(x, hm, zp, w_out, norm_g):
    n, d = x.shape
    tb = _token_block(n, 1024)
    rows = lambda i: (i, 0)
    const = lambda i: (0, 0)
    return pl.pallas_call(
        _mix_kernel,
        grid=(n // tb,),
        in_specs=[
            pl.BlockSpec((tb, d), rows),
            pl.BlockSpec((tb, MLSTM_W), rows),
            pl.BlockSpec((tb, POOL_W), rows),
            pl.BlockSpec(w_out.shape, const),
            pl.BlockSpec((1, d), const),
        ],
        out_specs=[
            pl.BlockSpec((tb, d), rows),
            pl.BlockSpec((d, tb), lambda i: (0, i)),
        ],
        out_shape=[
            jax.ShapeDtypeStruct((n, d), F32),
            jax.ShapeDtypeStruct((d, n), BF16),
        ],
        compiler_params=_params(("parallel",)),
        name="mix",
    )(x, hm, zp, w_out, norm_g)


def _extract_max(vals, pos):
    m = vals[0]
    for v in vals[1:]:
        m = jnp.maximum(m, v)
    m = jnp.max(m, axis=0, keepdims=True)
    big = jnp.float32(1e9)
    idx = None
    for v, p in zip(vals, pos):
        cand = jnp.where(v == m, p, big)
        idx = cand if idx is None else jnp.minimum(idx, cand)
    idx = jnp.min(idx, axis=0, keepdims=True)
    return m, idx


def _top16(s):
    tb = s.shape[1]
    row = lax.broadcasted_iota(jnp.int32, s.shape, 0).astype(F32)
    krow = lax.broadcasted_iota(jnp.int32, (PEER_TOPK, tb), 0)
    rank = jnp.full(s.shape, UNRANKED, F32)
    sv = jnp.zeros((PEER_TOPK, tb), F32)
    for k in range(PEER_TOPK):
        m, idx = _extract_max([s], [row])
        hit = row == idx
        rank = jnp.where(hit, float(k), rank)
        s = jnp.where(hit, NEG_INF, s)
        sv = jnp.where(krow == k, m, sv)
    return sv, rank


def _candidate_groups(sv1, sv2):
    tb = sv1.shape[1]
    vals, pos = [], []
    r8 = lax.broadcasted_iota(jnp.int32, (8, tb), 0)
    vals.append(sv1[0:1, :] + sv2[8:16, :])
    pos.append((r8 + 8).astype(F32))
    vals.append(sv1[0:1, :] + sv2[0:8, :])
    pos.append(r8.astype(F32))
    for k1 in range(1, 8):
        lim = PEER_TOPK // (k1 + 1)
        v = sv1[k1:k1 + 1, :] + sv2[0:8, :]
        vals.append(jnp.where(r8 < lim, v, NEG_INF))
        pos.append((r8 + k1 * PEER_TOPK).astype(F32))
    vals.append(sv1[8:16, :] + sv2[0:1, :])
    pos.append(((r8 + 8) * PEER_TOPK).astype(F32))
    return vals, pos


def _sorting_pairs(n):
    pairs = []
    p = 1
    while p < n:
        k = p
        while k >= 1:
            for j in range(k % p, n - k, 2 * k):
                for i in range(min(k, n - j - k)):
                    if (i + j) // (2 * p) == (i + j + k) // (2 * p):
                        pairs.append((i + j, i + j + k))
            k //= 2
        p *= 2
    return pairs


def _sublane_allreduce(x, op):
    for d in (1, 2, 4):
        x = op(x, pltpu.roll(x, d, axis=0))
    return x


def _sorted_top16(groups):
    g = list(groups)
    for i, j in _sorting_pairs(len(g)):
        g[i], g[j] = jnp.maximum(g[i], g[j]), jnp.minimum(g[i], g[j])
    n = len(g)
    for d in (1, 2, 4):
        p = [pltpu.roll(x, d, axis=0) for x in g]
        g = [jnp.maximum(g[i], p[n - 1 - i]) for i in range(n)]
        stride = n // 2
        while stride >= 1:
            for i in range(n):
                if i & stride == 0:
                    g[i], g[i + stride] = (jnp.maximum(g[i], g[i + stride]),
                                           jnp.minimum(g[i], g[i + stride]))
            stride //= 2
    return g


def _rank_bits(x, sv):
    c8 = x < sv[7]
    t = jnp.where(c8, sv[11], sv[3])
    c4 = x < t
    t = jnp.where(c8, jnp.where(c4, sv[13], sv[9]), jnp.where(c4, sv[5], sv[1]))
    c2 = x < t
    t = jnp.where(c8,
                  jnp.where(c4, jnp.where(c2, sv[14], sv[12]), jnp.where(c2, sv[10], sv[8])),
                  jnp.where(c4, jnp.where(c2, sv[6], sv[4]), jnp.where(c2, sv[2], sv[0])))
    c1 = x < t
    return (c8, c4, c2, c1), x < sv[15]


def _select16(bits, rows):
    c8, c4, c2, c1 = bits
    lvl = [jnp.where(c1, rows[2 * i + 1], rows[2 * i]) for i in range(8)]
    lvl = [jnp.where(c2, lvl[2 * i + 1], lvl[2 * i]) for i in range(4)]
    lvl = [jnp.where(c4, lvl[2 * i + 1], lvl[2 * i]) for i in range(2)]
    return jnp.where(c8, lvl[1], lvl[0])


def _route_fast(s1, s2):
    tb = s1.shape[1]
    r8 = lax.broadcasted_iota(jnp.int32, (8, tb), 0)
    n_grp = PEER_NKEYS // 8
    g1 = [s1[v * 8:(v + 1) * 8, :] for v in range(n_grp)]
    g2 = [s2[v * 8:(v + 1) * 8, :] for v in range(n_grp)]
    sv1 = _sorted_top16(g1)
    sv2 = _sorted_top16(g2)

    bad = jnp.zeros((8, tb), F32)
    for sv, grp in ((sv1, g1), (sv2, g2)):
        for k in range(PEER_TOPK - 1):
            bad = jnp.where(sv[k] == sv[k + 1], 1.0, bad)
        n_in = jnp.zeros((8, tb), F32)
        for x in grp:
            n_in = n_in + jnp.where(x >= sv[PEER_TOPK - 1], 1.0, 0.0)
        n_in = _sublane_allreduce(n_in, jnp.add)
        bad = jnp.where(n_in != float(PEER_TOPK), 1.0, bad)

    def by_sublane(rows):
        out = jnp.zeros((8, tb), F32)
        for k, row in enumerate(rows):
            out = jnp.where(r8 == k, row, out)
        return out
    a2_lo, a2_hi, a1_hi = by_sublane(sv2[0:8]), by_sublane(sv2[8:16]), by_sublane(sv1[8:16])
    orig = [sv1[0] + a2_hi, sv1[0] + a2_lo]
    for k1 in range(1, 8):
        orig.append(jnp.where(r8 < PEER_TOPK // (k1 + 1), sv1[k1] + a2_lo, NEG_INF))
    orig.append(a1_hi + sv2[0])
    vals = list(orig)
    maxima = []
    for _ in range(PEER_TOPK):
        m = vals[0]
        for v in vals[1:]:
            m = jnp.maximum(m, v)
        m = _sublane_allreduce(m, jnp.maximum)
        vals = [jnp.where(v == m, NEG_INF, v) for v in vals]
        maxima.append(m)

    def count_ge(thr):
        n = jnp.zeros((8, tb), F32)
        for o in orig:
            n = n + jnp.where(o >= thr, 1.0, 0.0)
        return _sublane_allreduce(n, jnp.add)
    n_last = count_ge(maxima[-1])
    n_prev = count_ge(maxima[-2])
    use_last = n_last == float(PEER_TOPK)
    thr = jnp.where(use_last, maxima[-1], maxima[-2])
    bad = jnp.where(use_last | (n_prev == float(PEER_TOPK)), bad, 1.0)
    sel = [o >= thr for o in orig]
    cmax = sv1[0] + sv2[0]
    z = jnp.zeros((8, tb), F32)
    for o, sl in zip(orig, sel):
        z = z + jnp.where(sl, jnp.exp(o - cmax), 0.0)
    z = _sublane_allreduce(z, jnp.add)
    ones = [jnp.where(sl, 1.0, 0.0) for sl in sel]
    cnt = [_sublane_allreduce(ones[0] + ones[1], jnp.add)]
    cnt += [_sublane_allreduce(ones[k1 + 1], jnp.add) for k1 in range(1, 8)]
    cnt += [ones[9][j:j + 1, :] for j in range(8)]

    cnt1, r2 = [], []
    weights = (8.0, 4.0, 2.0, 1.0)
    for x in g1:
        bits, below = _rank_bits(x, sv1)
        cnt1.append(jnp.where(below, 0.0, _select16(bits, cnt)))
    for x in g2:
        bits, below = _rank_bits(x, sv2)
        rank = jnp.zeros((8, tb), F32)
        for b, wgt in zip(bits, weights):
            rank = rank + jnp.where(b, wgt, 0.0)
        r2.append(jnp.where(below, UNRANKED, rank))
    return cnt1, r2, sv1[0][0:1, :], sv2[0][0:1, :], z[0:1, :], bad


def _route_exact(s1, s2):
    sv1, r1 = _top16(s1)
    sv2, r2 = _top16(s2)

    vals, pos = _candidate_groups(sv1, sv2)
    orig = list(vals)
    sel = [jnp.zeros(v.shape, F32) for v in vals]
    for _ in range(PEER_TOPK):
        _, idx = _extract_max(vals, pos)
        for i in range(len(vals)):
            hit = pos[i] == idx
            sel[i] = jnp.where(hit, 1.0, sel[i])
            vals[i] = jnp.where(hit, NEG_INF, vals[i])

    cmax = orig[1][0:1, :]
    z = None
    for o, sl in zip(orig, sel):
        part = jnp.sum(jnp.where(sl > 0.0, jnp.exp(o - cmax), 0.0), axis=0, keepdims=True)
        z = part if z is None else z + part
    cnt = [jnp.sum(sel[0] + sel[1], axis=0, keepdims=True)]
    cnt += [jnp.sum(sel[k1 + 1], axis=0, keepdims=True) for k1 in range(1, 8)]
    cnt += [sel[9][j:j + 1, :] for j in range(8)]

    cnt1 = jnp.zeros(r1.shape, F32)
    for k1 in range(PEER_TOPK):
        cnt1 = jnp.where(r1 == float(k1), cnt[k1], cnt1)
    return cnt1, r2, sv1[0:1, :], sv2[0:1, :], z


ROUTE_HEADS_PER_STEP = 4


def _route_kernel(xnt_ref, wqt_ref, k1_ref, k2_ref, cnt1_ref, g1_ref, r2_ref, e2_ref, s_ref):
    qt_all = _dot(wqt_ref[...], xnt_ref[...]).astype(BF16)

    def emit_gates(hh, s1, s2, max1, max2, z):
        g1_ref[hh] = jnp.exp(s1 - max1) * (0.5 / z)
        e2_ref[hh] = jnp.exp(s2 - max2).astype(BF16)

    ties = []
    for hh in range(ROUTE_HEADS_PER_STEP):
        qt = qt_all[hh * 2 * PEER_HALF:(hh + 1) * 2 * PEER_HALF, :]
        s1 = _dot(k1_ref[hh], qt[0:PEER_HALF, :])
        s2 = _dot(k2_ref[hh], qt[PEER_HALF:, :])
        s_ref[hh, 0] = s1
        s_ref[hh, 1] = s2
        cnt1, r2, max1, max2, z, tie = _route_fast(s1, s2)
        for v in range(PEER_NKEYS // 8):
            cnt1_ref[hh, v * 8:(v + 1) * 8, :] = cnt1[v]
            r2_ref[hh, v * 8:(v + 1) * 8, :] = r2[v].astype(BF16)
        emit_gates(hh, s1, s2, max1, max2, z)
        ties.append(jnp.max(tie))

    for hh in range(ROUTE_HEADS_PER_STEP):
        @pl.when(ties[hh] > 0.0)
        def _(hh=hh):
            s1, s2 = s_ref[hh, 0], s_ref[hh, 1]
            cnt1, r2, max1, max2, z = _route_exact(s1, s2)
            cnt1_ref[hh] = cnt1
            r2_ref[hh] = r2.astype(BF16)
            emit_gates(hh, s1, s2, max1, max2, z)


def _route(xnt, wqt, keys):
    d, n = xnt.shape
    tb = _token_block(n, 256)
    hps = ROUTE_HEADS_PER_STEP
    per_head = pl.BlockSpec((hps, PEER_NKEYS, tb), lambda h, i: (h, 0, i))
    shape = lambda dt: jax.ShapeDtypeStruct((PEER_HEADS, PEER_NKEYS, n), dt)
    return pl.pallas_call(
        _route_kernel,
        grid=(PEER_HEADS // hps, n // tb),
        in_specs=[
            pl.BlockSpec((d, tb), lambda h, i: (0, i)),
            pl.BlockSpec((hps * 2 * PEER_HALF, d), lambda h, i: (h, 0)),
            pl.BlockSpec((hps, PEER_NKEYS, PEER_HALF), lambda h, i: (h, 0, 0)),
            pl.BlockSpec((hps, PEER_NKEYS, PEER_HALF), lambda h, i: (h + PEER_HEADS // hps, 0, 0)),
        ],
        out_specs=[per_head] * 4,
        out_shape=[shape(F32), shape(F32), shape(BF16), shape(BF16)],
        scratch_shapes=[pltpu.VMEM((hps, 2, PEER_NKEYS, tb), F32)],
        compiler_params=_params(("parallel", "parallel")),
        name="route",
    )(xnt, wqt, keys, keys)


I1_PER_TILE = 16
BF16_TILE = (8, 2 * LANE)
EXPERT_TILE = I1_PER_TILE * PEER_NKEYS


def _experts_kernel(xnt_ref, cnt1_ref, g1_ref, r2_ref, e2_ref, u_ref, vt_ref, x1_ref,
                    x2_ref, acc_ref, act_ref, wa_ref):
    e = pl.program_id(1)
    tb = xnt_ref.shape[1]

    @pl.when(e == 0)
    def _():
        acc_ref[...] = jnp.zeros_like(acc_ref)

    a = _dot(u_ref[...], xnt_ref[...]).astype(BF16)
    act_ref[...] = a * (1.0 + jnp.tanh(a * (GELU_C0 + GELU_C1 * (a * a))))
    sub, width = BF16_TILE[0], min(BF16_TILE[1], tb)
    assert I1_PER_TILE % sub == 0
    i1_rows = pl.ds(pl.multiple_of(e * I1_PER_TILE, I1_PER_TILE), I1_PER_TILE)
    for lt in range(tb // width):
        lanes = pl.ds(lt * width, width)
        cnt_tile = [cnt1_ref[h, i1_rows, lanes] for h in range(PEER_HEADS)]
        g1_tile = [g1_ref[h, i1_rows, lanes] for h in range(PEER_HEADS)]
        for j in range(I1_PER_TILE):
            cnt_b = [jnp.broadcast_to(t[j:j + 1, :], (sub, width)).astype(BF16) for t in cnt_tile]
            g1_b = [jnp.broadcast_to(t[j:j + 1, :], (sub, width)).astype(BF16) for t in g1_tile]
            for rb in range(PEER_NKEYS // sub):
                krows = pl.ds(rb * sub, sub)
                w = None
                for h in range(PEER_HEADS):
                    mask = r2_ref[h, krows, lanes] < cnt_b[h]
                    term = jnp.where(mask, e2_ref[h, krows, lanes] * g1_b[h], jnp.zeros((), BF16))
                    w = term if w is None else w + term
                erows = pl.ds(j * PEER_NKEYS + rb * sub, sub)
                wa_ref[erows, lanes] = w * act_ref[erows, lanes]
    acc_ref[...] += _dot(vt_ref[...], wa_ref[...])

    @pl.when(e == pl.num_programs(1) - 1)
    def _():
        x2_ref[...] = x1_ref[...] + acc_ref[...].T


def _experts(xnt, cnt1, g1, r2, e2, u_tabs, vt_tabs, layer, x1):
    d, n = xnt.shape
    n_exp = u_tabs.shape[1]
    tb = _token_block(n, 512)
    assert tb % LANE == 0 and n_exp % EXPERT_TILE == 0
    per_head = pl.BlockSpec((PEER_HEADS, PEER_NKEYS, tb), lambda i, e: (0, 0, i))
    tile_buf = pltpu.VMEM((EXPERT_TILE, tb), BF16)
    return pl.pallas_call(
        _experts_kernel,
        grid=(n // tb, n_exp // EXPERT_TILE),
        in_specs=[
            pl.BlockSpec((d, tb), lambda i, e: (0, i)),
            per_head, per_head, per_head, per_head,
            pl.BlockSpec((None, EXPERT_TILE, d), lambda i, e: (layer, e, 0)),
            pl.BlockSpec((None, d, EXPERT_TILE), lambda i, e: (layer, 0, e)),
            pl.BlockSpec((tb, d), lambda i, e: (i, 0)),
        ],
        out_specs=pl.BlockSpec((tb, d), lambda i, e: (i, 0)),
        out_shape=jax.ShapeDtypeStruct((n, d), F32),
        scratch_shapes=[pltpu.VMEM((d, tb), F32), tile_buf, tile_buf],
        compiler_params=_params(("parallel", "arbitrary")),
        name="experts",
    )(xnt, cnt1, g1, r2, e2, u_tabs, vt_tabs, x1)


def _ple_kernel(x_ref, p_ref, wg_ref, wp_ref, nf_ref, y_ref, *, final_norm):
    x = x_ref[...]
    gate = _sigmoid(_dot(x.astype(BF16), wg_ref[...]))
    y = x + gate * _dot(p_ref[...].astype(BF16), wp_ref[...])
    if final_norm:
        y = _rmsnorm(y, nf_ref[...])
    y_ref[...] = y


def _ple(x, p_all, layer, w_gate, w_ple, norm_f, final_norm):
    n, d = x.shape
    tb = _token_block(n, 1024)
    rows = lambda i: (i, 0)
    const = lambda i: (0, 0)
    return pl.pallas_call(
        functools.partial(_ple_kernel, final_norm=final_norm),
        grid=(n // tb,),
        in_specs=[
            pl.BlockSpec((tb, d), rows),
            pl.BlockSpec((None, tb, p_all.shape[2]), lambda i: (layer, i, 0)),
            pl.BlockSpec(w_gate.shape, const),
            pl.BlockSpec(w_ple.shape, const),
            pl.BlockSpec((1, d), const),
        ],
        out_specs=pl.BlockSpec((tb, d), rows),
        out_shape=jax.ShapeDtypeStruct((n, d), F32),
        compiler_params=_params(("parallel",)),
        name="ple",
    )(x, p_all, w_gate, w_ple, norm_f)


def _prep_layer_weights(l, w_in, b_gate, mlstm_norm, w_pool, pool_scale, w_out, norm1, norm2,
                        peer_wq, peer_keys, peer_u, peer_v, w_ple, w_gate):
    w = MLSTM_W
    wi = w_in[l]
    d = wi.shape[0]
    wa = jnp.concatenate([wi[:, 0:w], wi[:, 2 * w:4 * w + POOL_W]], axis=1).astype(BF16)
    wkt = wi[:, w:2 * w].T.astype(BF16)
    wgt = wi[:, 4 * w + POOL_W:].T.astype(BF16)
    return dict(
        wa=wa, wkt=wkt, wgt=wgt,
        bg=b_gate[l].reshape(2 * HEADS, 1).astype(F32),
        mlstm_norm=mlstm_norm[l].reshape(1, w),
        w_pool=w_pool[l].astype(BF16),
        pool_scale=pool_scale[l].reshape(1, POOL_W),
        w_out=w_out[l].astype(BF16),
        norm1=norm1[l].reshape(1, d), norm2=norm2[l].reshape(1, d),
        wqt=peer_wq[l].T.astype(BF16),
        keys=peer_keys[l].reshape(2 * PEER_HEADS, PEER_NKEYS, PEER_HALF).astype(BF16),
        w_ple=w_ple[l].astype(BF16), w_gate=w_gate[l].astype(BF16),
        layer=l,
    )


def _ffn_and_embed(x1, xnt, p, lw, norm_f, final_norm):
    cnt1, g1, r2, e2 = _route(xnt, lw["wqt"], lw["keys"])
    x2 = _experts(xnt, cnt1, g1, r2, e2, lw["u_tabs"], lw["vt_tabs"], lw["layer"], x1)
    return _ple(x2, p, lw["layer"], lw["w_gate"], lw["w_ple"], norm_f, final_norm)


def _prompt_layer(x, p, lw, norm_f, final_norm, batch, t_len):
    q, kt, v, o, u, gt = _proj(x, lw["norm1"], lw["wa"], lw["wkt"], lw["wgt"], lw["bg"])
    gates3 = gt.reshape(2 * HEADS, (batch * t_len) // CHUNK, CHUNK)
    hm, cext, m8 = _mlstm(q, kt, v, o, gates3, lw["mlstm_norm"], batch, t_len)
    zp, buf = _pool(u, lw["w_pool"], lw["pool_scale"], batch, t_len)
    x1, xnt = _mix(x, hm, zp, lw["w_out"], lw["norm2"])
    y = _ffn_and_embed(x1, xnt, p, lw, norm_f, final_norm)
    return y, cext[..., :HEAD_DIM], cext[..., HEAD_DIM], m8[..., 0, 0], buf


def _sample_layer(x, p, c_all, n0, m0, buf0, lw, norm_f, final_norm):
    q, kt, v, o, u, gt = _proj(x, lw["norm1"], lw["wa"], lw["wkt"], lw["wgt"], lw["bg"])
    hm, c1, n1, m1 = _mlstm1(q, kt.T, v, o, gt.T, c_all, lw["layer"], n0.reshape(-1, MLSTM_W), m0,
                             lw["mlstm_norm"])
    n1 = n1.reshape(n0.shape)
    zp, buf_t = _pool1(u, jnp.swapaxes(buf0, 0, 1), lw["w_pool"], lw["pool_scale"])
    x1, xnt = _mix(x, hm, zp, lw["w_out"], lw["norm2"])
    y = _ffn_and_embed(x1, xnt, p, lw, norm_f, final_norm)
    return y, c1, n1, m1, jnp.swapaxes(buf_t, 0, 1)


def kernel(x_prompt, x_sample, p_prompt, p_sample, state_C, state_n, state_m, state_pool,
           w_in, b_gate, mlstm_norm, w_pool, pool_scale, w_out, norm1, norm2,
           peer_wq, peer_keys, peer_u, peer_v, w_ple, w_gate, norm_f):
    depth = w_in.shape[0]
    batch, t_len, d = x_prompt.shape
    dec_batch, dec_len, _ = x_sample.shape
    assert dec_len == 1 and t_len % CHUNK == 0
    nf = norm_f.reshape(1, d)

    xp = x_prompt.reshape(batch * t_len, d)
    xs = x_sample.reshape(dec_batch, d)
    outs_p, outs_s = [], []
    u_tabs = peer_u.astype(BF16)
    vt_tabs = jnp.swapaxes(peer_v, 1, 2).astype(BF16)
    for l in range(depth):
        lw = _prep_layer_weights(l, w_in, b_gate, mlstm_norm, w_pool, pool_scale, w_out, norm1,
                                 norm2, peer_wq, peer_keys, peer_u, peer_v, w_ple, w_gate)
        lw.update(u_tabs=u_tabs, vt_tabs=vt_tabs)
        final = l == depth - 1
        pp = p_prompt.reshape(depth, batch * t_len, -1)
        ps = p_sample.reshape(depth, dec_batch, -1)
        xp, c_p, n_p, m_p, buf_p = _prompt_layer(xp, pp, lw, nf, final, batch, t_len)
        xs, c_s, n_s, m_s, buf_s = _sample_layer(xs, ps, state_C, state_n[l], state_m[l],
                                                 state_pool[l], lw, nf, final)
        outs_p.append((c_p, n_p, m_p, buf_p))
        outs_s.append((c_s, n_s, m_s, buf_s))

    stack = lambda outs, i: jnp.stack([o[i] for o in outs])
    return (xp.reshape(batch, t_len, d), xs.reshape(dec_batch, dec_len, d),
            stack(outs_p, 0), stack(outs_p, 1), stack(outs_p, 2), stack(outs_p, 3),
            stack(outs_s, 0), stack(outs_s, 1), stack(outs_s, 2), stack(outs_s, 3))
```

```python
import functools
import math

import jax
import jax.numpy as jnp
from jax import lax
from jax.experimental import pallas as pl
from jax.experimental.pallas import tpu as pltpu

F32 = jnp.float32
BF16 = jnp.bfloat16
EPS = 1e-6
NEG_INF = float("-inf")

LANE = 128
V7X_VMEM_LIMIT = 56 * 1024 * 1024

HEADS = 4
HEAD_DIM = 128
MLSTM_W = HEADS * HEAD_DIM
POOL_WINDOWS = (2, 4, 8, 16)
POOL_GROUP = 128
POOL_W = POOL_GROUP * len(POOL_WINDOWS)
POOL_BUF = max(POOL_WINDOWS) - 1
CHUNK = 128
PEER_HEADS = 8
PEER_NKEYS = 128
PEER_HALF = 128
PEER_TOPK = 16
UNRANKED = 127.0
GELU_C0 = math.sqrt(2.0 / math.pi)
GELU_C1 = 0.044715 * GELU_C0


def _params(semantics):
    return pltpu.CompilerParams(dimension_semantics=semantics, vmem_limit_bytes=V7X_VMEM_LIMIT)


def _token_block(n, want):
    tb = min(n, want)
    assert n % tb == 0
    return tb


def _rmsnorm(x, g):
    return x * lax.rsqrt(jnp.mean(x * x, axis=-1, keepdims=True) + EPS) * g


def _log_sigmoid(x):
    return jnp.minimum(x, 0.0) - jnp.log(1.0 + jnp.exp(-jnp.abs(x)))


def _sigmoid(x):
    return 1.0 / (1.0 + jnp.exp(-x))


def _dot(a, b):
    return jnp.dot(a, b, preferred_element_type=F32)


def _dot_nt(a, b):
    return lax.dot_general(a, b, (((1,), (1,)), ((), ())), preferred_element_type=F32)


def _proj_kernel(x_ref, g_ref, wa_ref, wkt_ref, wgt_ref, bg_ref,
                 q_ref, kt_ref, v_ref, o_ref, u_ref, gt_ref):
    hn = _rmsnorm(x_ref[...], g_ref[...]).astype(BF16)
    pa = _dot(hn, wa_ref[...])
    w = MLSTM_W
    q_ref[...] = pa[:, 0:w].astype(BF16)
    v_ref[...] = pa[:, w:2 * w].astype(BF16)
    o_ref[...] = _sigmoid(pa[:, 2 * w:3 * w]).astype(BF16)
    u_ref[...] = pa[:, 3 * w:4 * w]
    kt = _dot_nt(wkt_ref[...], hn) * (HEAD_DIM ** -0.5)
    kt_ref[...] = kt.astype(BF16)
    gt_ref[...] = _dot_nt(wgt_ref[...], hn) + bg_ref[...]


def _proj(x, norm_g, wa, wkt, wgt, bg):
    n, d = x.shape
    tb = _token_block(n, 1024)
    grid = (n // tb,)
    const = lambda i: (0, 0)
    return pl.pallas_call(
        _proj_kernel,
        grid=grid,
        in_specs=[
            pl.BlockSpec((tb, d), lambda i: (i, 0)),
            pl.BlockSpec((1, d), const),
            pl.BlockSpec(wa.shape, const),
            pl.BlockSpec(wkt.shape, const),
            pl.BlockSpec(wgt.shape, const),
            pl.BlockSpec(bg.shape, const),
        ],
        out_specs=[
            pl.BlockSpec((tb, MLSTM_W), lambda i: (i, 0)),
            pl.BlockSpec((MLSTM_W, tb), lambda i: (0, i)),
            pl.BlockSpec((tb, MLSTM_W), lambda i: (i, 0)),
            pl.BlockSpec((tb, MLSTM_W), lambda i: (i, 0)),
            pl.BlockSpec((tb, POOL_W), lambda i: (i, 0)),
            pl.BlockSpec((2 * HEADS, tb), lambda i: (0, i)),
        ],
        out_shape=[
            jax.ShapeDtypeStruct((n, MLSTM_W), BF16),
            jax.ShapeDtypeStruct((MLSTM_W, n), BF16),
            jax.ShapeDtypeStruct((n, MLSTM_W), BF16),
            jax.ShapeDtypeStruct((n, MLSTM_W), BF16),
            jax.ShapeDtypeStruct((n, POOL_W), F32),
            jax.ShapeDtypeStruct((2 * HEADS, n), F32),
        ],
        compiler_params=_params(("parallel",)),
        name="proj",
    )(x, norm_g, wa, wkt, wgt, bg)


def _lane_cumsum(x):
    lane = lax.broadcasted_iota(jnp.int32, x.shape, 1)
    shift = 1
    while shift < x.shape[1]:
        x = x + jnp.where(lane >= shift, pltpu.roll(x, shift, axis=1), 0.0)
        shift *= 2
    return x


def _mlstm_kernel(q_ref, kt_ref, v_ref, o_ref, ig_ref, lf_ref, nw_ref,
                  hm_ref, cext_out_ref, m_out_ref, cext_ref):
    t_len = q_ref.shape[0]
    n_chunks = t_len // CHUNK
    ig2 = ig_ref[0]
    lf2 = _log_sigmoid(lf_ref[0])
    b2 = _lane_cumsum(lf2)
    a2 = ig2 - b2

    t_idx = lax.broadcasted_iota(jnp.int32, (CHUNK, CHUNK), 0)
    s_idx = lax.broadcasted_iota(jnp.int32, (CHUNK, CHUNK), 1)
    causal = s_idx <= t_idx
    ones_col = (lax.broadcasted_iota(jnp.int32, (CHUNK, HEAD_DIM), 1) == 0).astype(BF16)
    nw = nw_ref[...]

    cext_ref[...] = jnp.zeros_like(cext_ref)
    m = jnp.zeros((1, 1), F32)
    for c in range(n_chunks):
        rows = pl.ds(c * CHUNK, CHUNK)
        a_row = a2[c:c + 1, :]
        lf_row = lf2[c:c + 1, :]
        b_last = b2[c:c + 1, CHUNK - 1:CHUNK]
        b_col = jnp.sum(jnp.where(causal, lf_row, 0.0), axis=1, keepdims=True)
        amax_col = jnp.max(jnp.where(causal, a_row, NEG_INF), axis=1, keepdims=True)
        m_col = jnp.maximum(amax_col, m)
        decay_mat = jnp.where(causal, jnp.exp(a_row - m_col), 0.0)

        qc = q_ref[rows, :]
        ktc = kt_ref[:, rows]
        v_ext = jnp.concatenate([v_ref[rows, :], ones_col], axis=1)
        s = _dot(qc, ktc) * decay_mat
        intra = _dot(s.astype(BF16), v_ext)
        inter = _dot(qc, cext_ref[...].astype(BF16))
        tot = intra + jnp.exp(m - m_col) * inter
        num = tot[:, :HEAD_DIM]
        den = tot[:, HEAD_DIM:HEAD_DIM + 1]
        h = num / jnp.maximum(jnp.abs(den), jnp.exp(-(b_col + m_col)))
        h = h * lax.rsqrt(jnp.mean(h * h, axis=1, keepdims=True) + EPS)
        hm_ref[rows, :] = (h * nw * o_ref[rows, :]).astype(BF16)

        m_new = jnp.maximum(b_last + m, jnp.max(a_row, axis=1, keepdims=True) + b_last)
        ws_row = jnp.exp(a_row + (b_last - m_new))
        kws = (ktc.astype(F32) * ws_row).astype(BF16)
        cext_ref[...] = jnp.exp(b_last + m - m_new) * cext_ref[...] + _dot(kws, v_ext)
        m = m_new

    cext_out_ref[0, 0] = cext_ref[...]
    m_out_ref[0, 0] = jnp.broadcast_to(m, m_out_ref.shape[2:])


def _mlstm(q, kt, v, o, gates3, norm_w, batch, t_len):
    n = q.shape[0]
    n_chunks = t_len // CHUNK
    seq = lambda b, h: (b, h)
    return pl.pallas_call(
        _mlstm_kernel,
        grid=(batch, HEADS),
        in_specs=[
            pl.BlockSpec((t_len, HEAD_DIM), seq),
            pl.BlockSpec((HEAD_DIM, t_len), lambda b, h: (h, b)),
            pl.BlockSpec((t_len, HEAD_DIM), seq),
            pl.BlockSpec((t_len, HEAD_DIM), seq),
            pl.BlockSpec((1, n_chunks, CHUNK), lambda b, h: (h, b, 0)),
            pl.BlockSpec((1, n_chunks, CHUNK), lambda b, h: (h + HEADS, b, 0)),
            pl.BlockSpec((1, HEAD_DIM), lambda b, h: (0, h)),
        ],
        out_specs=[
            pl.BlockSpec((t_len, HEAD_DIM), seq),
            pl.BlockSpec((1, 1, HEAD_DIM, 2 * HEAD_DIM), lambda b, h: (b, h, 0, 0)),
            pl.BlockSpec((1, 1, 8, LANE), lambda b, h: (b, h, 0, 0)),
        ],
        out_shape=[
            jax.ShapeDtypeStruct((n, MLSTM_W), BF16),
            jax.ShapeDtypeStruct((batch, HEADS, HEAD_DIM, 2 * HEAD_DIM), F32),
            jax.ShapeDtypeStruct((batch, HEADS, 8, LANE), F32),
        ],
        scratch_shapes=[pltpu.VMEM((HEAD_DIM, 2 * HEAD_DIM), F32)],
        compiler_params=_params(("parallel", "parallel")),
        name="mlstm",
    )(q, kt, v, o, gates3, gates3, norm_w)


def _mlstm1_kernel(q_ref, k_ref, v_ref, o_ref, g_ref, c_ref, n_ref, m_ref, nw_ref,
                   hm_ref, c_out_ref, n_out_ref, m_out_ref):
    bb = q_ref.shape[0]
    row = lax.broadcasted_iota(jnp.int32, (bb, HEAD_DIM), 0)
    eye = (lax.broadcasted_iota(jnp.int32, (HEAD_DIM, HEAD_DIM), 0)
           == lax.broadcasted_iota(jnp.int32, (HEAD_DIM, HEAD_DIM), 1))
    lane_h = lax.broadcasted_iota(jnp.int32, (bb, HEADS), 1)
    g = g_ref[...]
    m_all = m_ref[...]
    m_out = jnp.zeros((bb, HEADS), F32)
    for h in range(HEADS):
        cols = slice(h * HEAD_DIM, (h + 1) * HEAD_DIM)
        qb = q_ref[:, cols]
        qf = qb.astype(F32)
        kf = k_ref[:, cols].astype(F32)
        vf = v_ref[:, cols].astype(F32)
        ig = g[:, h:h + 1]
        lf = _log_sigmoid(g[:, HEADS + h:HEADS + h + 1])
        m_old = m_all[:, h:h + 1]
        n_old = n_ref[:, cols]
        m_new = jnp.maximum(ig, lf + m_old)
        w_in = jnp.exp(ig - m_new)
        w_st = jnp.exp(lf + m_old - m_new)
        qc = jnp.zeros((bb, HEAD_DIM), F32)
        for j in range(bb):
            c_old = c_ref[j, h]
            res = _dot(qb, c_old.astype(BF16))
            qc = jnp.where(row == j, res, qc)
            k_col = jnp.sum(jnp.where(eye, kf[j:j + 1, :], 0.0), axis=1, keepdims=True)
            c_out_ref[j, h] = w_st[j:j + 1, :] * c_old + k_col * (w_in[j:j + 1, :] * vf[j:j + 1, :])
        s = jnp.sum(qf * kf, axis=1, keepdims=True) * w_in
        num = s * vf + w_st * qc
        den = s + w_st * jnp.sum(qf * n_old, axis=1, keepdims=True)
        hh = num / jnp.maximum(jnp.abs(den), jnp.exp(-m_new))
        hh = hh * lax.rsqrt(jnp.mean(hh * hh, axis=1, keepdims=True) + EPS)
        hm_ref[:, cols] = (hh * nw_ref[:, cols] * o_ref[:, cols]).astype(BF16)
        n_out_ref[:, cols] = w_st * n_old + w_in * kf
        m_out = jnp.where(lane_h == h, m_new, m_out)
    m_out_ref[...] = m_out


def _mlstm1_kernel_stacked(*refs):
    _mlstm1_kernel(*refs[:9], *refs[10:])


def _mlstm1(q, k, v, o, g, c_all, layer, c_stack, n0, m0, norm_w):
    batch = q.shape[0]
    bb = _token_block(batch, 16)
    rows = lambda i: (i, 0)
    stacked = c_stack is not None
    return pl.pallas_call(
        _mlstm1_kernel_stacked if stacked else _mlstm1_kernel,
        input_output_aliases={9: 1} if stacked else {},
        grid=(batch // bb,),
        in_specs=[
            pl.BlockSpec((bb, MLSTM_W), rows),
            pl.BlockSpec((bb, MLSTM_W), rows),
            pl.BlockSpec((bb, MLSTM_W), rows),
            pl.BlockSpec((bb, MLSTM_W), rows),
            pl.BlockSpec((bb, 2 * HEADS), rows),
            pl.BlockSpec((None, bb, HEADS, HEAD_DIM, HEAD_DIM), lambda i: (layer, i, 0, 0, 0)),
            pl.BlockSpec((bb, MLSTM_W), rows),
            pl.BlockSpec((bb, HEADS), rows),
            pl.BlockSpec((1, MLSTM_W), lambda i: (0, 0)),
        ] + ([pl.BlockSpec(memory_space=pl.ANY)] if stacked else []),
        out_specs=[
            pl.BlockSpec((bb, MLSTM_W), rows),
            pl.BlockSpec((None, bb, HEADS, HEAD_DIM, HEAD_DIM), lambda i: (layer, i, 0, 0, 0)),
            pl.BlockSpec((bb, MLSTM_W), rows),
            pl.BlockSpec((bb, HEADS), rows),
        ],
        out_shape=[
            jax.ShapeDtypeStruct((batch, MLSTM_W), BF16),
            jax.ShapeDtypeStruct(c_all.shape, F32),
            jax.ShapeDtypeStruct(n0.shape, F32),
            jax.ShapeDtypeStruct(m0.shape, F32),
        ],
        compiler_params=_params(("parallel",)),
        name="mlstm1",
    )(q, k, v, o, g, c_all, n0, m0, norm_w, *([c_stack] if stacked else []))


def _pool_kernel(u_ref, wp_ref, sc_ref, z_ref, buf_ref):
    t_len = u_ref.shape[0]
    t_idx = lax.broadcasted_iota(jnp.int32, (t_len, POOL_GROUP), 0)
    for g, w in enumerate(POOL_WINDOWS):
        cols = slice(g * POOL_GROUP, (g + 1) * POOL_GROUP)
        x = u_ref[:, cols]
        s = x
        k = 1
        while k < w:
            s = s + jnp.where(t_idx >= k, pltpu.roll(s, k, axis=0), 0.0)
            k *= 2
        cnt = jnp.minimum(t_idx + 1, w).astype(F32)
        r = s / cnt - x
        z = _dot(r.astype(BF16), wp_ref[g]) * sc_ref[:, cols]
        z_ref[:, cols] = z.astype(BF16)
    buf_ref[0] = u_ref[t_len - POOL_BUF:t_len, :]


def _pool(u, w_pool, scale, batch, t_len):
    n = u.shape[0]
    return pl.pallas_call(
        _pool_kernel,
        grid=(batch,),
        in_specs=[
            pl.BlockSpec((t_len, POOL_W), lambda b: (b, 0)),
            pl.BlockSpec(w_pool.shape, lambda b: (0, 0, 0)),
            pl.BlockSpec((1, POOL_W), lambda b: (0, 0)),
        ],
        out_specs=[
            pl.BlockSpec((t_len, POOL_W), lambda b: (b, 0)),
            pl.BlockSpec((1, POOL_BUF, POOL_W), lambda b: (b, 0, 0)),
        ],
        out_shape=[
            jax.ShapeDtypeStruct((n, POOL_W), BF16),
            jax.ShapeDtypeStruct((batch, POOL_BUF, POOL_W), F32),
        ],
        compiler_params=_params(("parallel",)),
        name="pool",
    )(u, w_pool, scale)


def _pool1_kernel(u_ref, buft_ref, wp_ref, sc_ref, z_ref, buft_out_ref):
    for g, w in enumerate(POOL_WINDOWS):
        cols = slice(g * POOL_GROUP, (g + 1) * POOL_GROUP)
        x = u_ref[:, cols]
        s = x
        for j in range(1, w):
            s = s + buft_ref[POOL_BUF - j, :, cols]
        r = s / float(w) - x
        z = _dot(r.astype(BF16), wp_ref[g]) * sc_ref[:, cols]
        z_ref[:, cols] = z.astype(BF16)
    for j in range(POOL_BUF - 1):
        buft_out_ref[j] = buft_ref[j + 1]
    buft_out_ref[POOL_BUF - 1] = u_ref[...]


def _pool1(u, buf_t, w_pool, scale):
    batch = u.shape[0]
    full2 = lambda i: (0, 0)
    full3 = lambda i: (0, 0, 0)
    return pl.pallas_call(
        _pool1_kernel,
        grid=(1,),
        in_specs=[
            pl.BlockSpec(u.shape, full2),
            pl.BlockSpec(buf_t.shape, full3),
            pl.BlockSpec(w_pool.shape, full3),
            pl.BlockSpec((1, POOL_W), full2),
        ],
        out_specs=[
            pl.BlockSpec((batch, POOL_W), full2),
            pl.BlockSpec(buf_t.shape, full3),
        ],
        out_shape=[
            jax.ShapeDtypeStruct((batch, POOL_W), BF16),
            jax.ShapeDtypeStruct(buf_t.shape, F32),
        ],
        compiler_params=_params(("arbitrary",)),
        name="pool1",
    )(u, buf_t, w_pool, scale)


def _mix_kernel(x_ref, hm_ref, zp_ref, wo_ref, g_ref, x1_ref, xnt_ref):
    mix = _dot(hm_ref[...], wo_ref[0:MLSTM_W, :]) + _dot(zp_ref[...], wo_ref[MLSTM_W:, :])
    x1 = x_ref[...] + mix
    x1_ref[...] = x1
    xnt_ref[...] = _rmsnorm(x1, g_ref[...]).T.astype(BF16)


def _mix(x, hm, zp, w_out, norm_g):
    n, d = x.shape
    tb = _token_block(n, 1024)
    rows = lambda i: (i, 0)
    const = lambda i: (0, 0)
    return pl.pallas_call(
        _mix_kernel,
        grid=(n // tb,),
        in_specs=[
            pl.BlockSpec((tb, d), rows),
            pl.BlockSpec((tb, MLSTM_W), rows),
            pl.BlockSpec((tb, POOL_W), rows),
            pl.BlockSpec(w_out.shape, const),
            pl.BlockSpec((1, d), const),
        ],
        out_specs=[
            pl.BlockSpec((tb, d), rows),
            pl.BlockSpec((d, tb), lambda i: (0, i)),
        ],
        out_shape=[
            jax.ShapeDtypeStruct((n, d), F32),
            jax.ShapeDtypeStruct((d, n), BF16),
        ],
        compiler_params=_params(("parallel",)),
        name="mix",
    )(x, hm, zp, w_out, norm_g)


def _extract_max(vals, pos):
    m = vals[0]
    for v in vals[1:]:
        m = jnp.maximum(m, v)
    m = jnp.max(m, axis=0, keepdims=True)
    big = jnp.float32(1e9)
    idx = None
    for v, p in zip(vals, pos):
        cand = jnp.where(v == m, p, big)
        idx = cand if idx is None else jnp.minimum(idx, cand)
    idx = jnp.min(idx, axis=0, keepdims=True)
    return m, idx


def _top16(s):
    tb = s.shape[1]
    row = lax.broadcasted_iota(jnp.int32, s.shape, 0).astype(F32)
    krow = lax.broadcasted_iota(jnp.int32, (PEER_TOPK, tb), 0)
    rank = jnp.full(s.shape, UNRANKED, F32)
    sv = jnp.zeros((PEER_TOPK, tb), F32)
    for k in range(PEER_TOPK):
        m, idx = _extract_max([s], [row])
        hit = row == idx
        rank = jnp.where(hit, float(k), rank)
        s = jnp.where(hit, NEG_INF, s)
        sv = jnp.where(krow == k, m, sv)
    return sv, rank


def _candidate_groups(sv1, sv2):
    tb = sv1.shape[1]
    vals, pos = [], []
    r8 = lax.broadcasted_iota(jnp.int32, (8, tb), 0)
    vals.append(sv1[0:1, :] + sv2[8:16, :])
    pos.append((r8 + 8).astype(F32))
    vals.append(sv1[0:1, :] + sv2[0:8, :])
    pos.append(r8.astype(F32))
    for k1 in range(1, 8):
        lim = PEER_TOPK // (k1 + 1)
        v = sv1[k1:k1 + 1, :] + sv2[0:8, :]
        vals.append(jnp.where(r8 < lim, v, NEG_INF))
        pos.append((r8 + k1 * PEER_TOPK).astype(F32))
    vals.append(sv1[8:16, :] + sv2[0:1, :])
    pos.append(((r8 + 8) * PEER_TOPK).astype(F32))
    return vals, pos


def _sorting_pairs(n):
    pairs = []
    p = 1
    while p < n:
        k = p
        while k >= 1:
            for j in range(k % p, n - k, 2 * k):
                for i in range(min(k, n - j - k)):
                    if (i + j) // (2 * p) == (i + j + k) // (2 * p):
                        pairs.append((i + j, i + j + k))
            k //= 2
        p *= 2
    return pairs


def _sublane_allreduce(x, op):
    for d in (1, 2, 4):
        x = op(x, pltpu.roll(x, d, axis=0))
    return x


def _sorted_top16(groups):
    g = list(groups)
    for i, j in _sorting_pairs(len(g)):
        g[i], g[j] = jnp.maximum(g[i], g[j]), jnp.minimum(g[i], g[j])
    n = len(g)
    for d in (1, 2, 4):
        p = [pltpu.roll(x, d, axis=0) for x in g]
        g = [jnp.maximum(g[i], p[n - 1 - i]) for i in range(n)]
        stride = n // 2
        while stride >= 1:
            for i in range(n):
                if i & stride == 0:
                    g[i], g[i + stride] = (jnp.maximum(g[i], g[i + stride]),
                                           jnp.minimum(g[i], g[i + stride]))
            stride //= 2
    return g


def _rank_bits(x, sv):
    c8 = x < sv[7]
    t = jnp.where(c8, sv[11], sv[3])
    c4 = x < t
    t = jnp.where(c8, jnp.where(c4, sv[13], sv[9]), jnp.where(c4, sv[5], sv[1]))
    c2 = x < t
    t = jnp.where(c8,
                  jnp.where(c4, jnp.where(c2, sv[14], sv[12]), jnp.where(c2, sv[10], sv[8])),
                  jnp.where(c4, jnp.where(c2, sv[6], sv[4]), jnp.where(c2, sv[2], sv[0])))
    c1 = x < t
    return (c8, c4, c2, c1), x < sv[15]


def _select16(bits, rows):
    c8, c4, c2, c1 = bits
    lvl = [jnp.where(c1, rows[2 * i + 1], rows[2 * i]) for i in range(8)]
    lvl = [jnp.where(c2, lvl[2 * i + 1], lvl[2 * i]) for i in range(4)]
    lvl = [jnp.where(c4, lvl[2 * i + 1], lvl[2 * i]) for i in range(2)]
    return jnp.where(c8, lvl[1], lvl[0])


def _route_fast(s1, s2):
    tb = s1.shape[1]
    r8 = lax.broadcasted_iota(jnp.int32, (8, tb), 0)
    n_grp = PEER_NKEYS // 8
    g1 = [s1[v * 8:(v + 1) * 8, :] for v in range(n_grp)]
    g2 = [s2[v * 8:(v + 1) * 8, :] for v in range(n_grp)]
    sv1 = _sorted_top16(g1)
    sv2 = _sorted_top16(g2)

    bad = jnp.zeros((8, tb), F32)
    for sv, grp in ((sv1, g1), (sv2, g2)):
        for k in range(PEER_TOPK - 1):
            bad = jnp.where(sv[k] == sv[k + 1], 1.0, bad)
        n_in = jnp.zeros((8, tb), F32)
        for x in grp:
            n_in = n_in + jnp.where(x >= sv[PEER_TOPK - 1], 1.0, 0.0)
        n_in = _sublane_allreduce(n_in, jnp.add)
        bad = jnp.where(n_in != float(PEER_TOPK), 1.0, bad)

    def by_sublane(rows):
        out = jnp.zeros((8, tb), F32)
        for k, row in enumerate(rows):
            out = jnp.where(r8 == k, row, out)
        return out
    a2_lo, a2_hi, a1_hi = by_sublane(sv2[0:8]), by_sublane(sv2[8:16]), by_sublane(sv1[8:16])
    orig = [sv1[0] + a2_hi, sv1[0] + a2_lo]
    for k1 in range(1, 8):
        orig.append(jnp.where(r8 < PEER_TOPK // (k1 + 1), sv1[k1] + a2_lo, NEG_INF))
    orig.append(a1_hi + sv2[0])
    vals = list(orig)
    maxima = []
    for _ in range(PEER_TOPK):
        m = vals[0]
        for v in vals[1:]:
            m = jnp.maximum(m, v)
        m = _sublane_allreduce(m, jnp.maximum)
        vals = [jnp.where(v == m, NEG_INF, v) for v in vals]
        maxima.append(m)

    def count_ge(thr):
        n = jnp.zeros((8, tb), F32)
        for o in orig:
            n = n + jnp.where(o >= thr, 1.0, 0.0)
        return _sublane_allreduce(n, jnp.add)
    n_last = count_ge(maxima[-1])
    n_prev = count_ge(maxima[-2])
    use_last = n_last == float(PEER_TOPK)
    thr = jnp.where(use_last, maxima[-1], maxima[-2])
    bad = jnp.where(use_last | (n_prev == float(PEER_TOPK)), bad, 1.0)
    sel = [o >= thr for o in orig]
    cmax = sv1[0] + sv2[0]
    z = jnp.zeros((8, tb), F32)
    for o, sl in zip(orig, sel):
        z = z + jnp.where(sl, jnp.exp(o - cmax), 0.0)
    z = _sublane_allreduce(z, jnp.add)
    ones = [jnp.where(sl, 1.0, 0.0) for sl in sel]
    cnt = [_sublane_allreduce(ones[0] + ones[1], jnp.add)]
    cnt += [_sublane_allreduce(ones[k1 + 1], jnp.add) for k1 in range(1, 8)]
    cnt += [ones[9][j:j + 1, :] for j in range(8)]

    cnt1, r2 = [], []
    weights = (8.0, 4.0, 2.0, 1.0)
    for x in g1:
        bits, below = _rank_bits(x, sv1)
        cnt1.append(jnp.where(below, 0.0, _select16(bits, cnt)))
    for x in g2:
        bits, below = _rank_bits(x, sv2)
        rank = jnp.zeros((8, tb), F32)
        for b, wgt in zip(bits, weights):
            rank = rank + jnp.where(b, wgt, 0.0)
        r2.append(jnp.where(below, UNRANKED, rank))
    return cnt1, r2, sv1[0][0:1, :], sv2[0][0:1, :], z[0:1, :], bad


def _route_exact(s1, s2):
    sv1, r1 = _top16(s1)
    sv2, r2 = _top16(s2)

    vals, pos = _candidate_groups(sv1, sv2)
    orig = list(vals)
    sel = [jnp.zeros(v.shape, F32) for v in vals]
    for _ in range(PEER_TOPK):
        _, idx = _extract_max(vals, pos)
        for i in range(len(vals)):
            hit = pos[i] == idx
            sel[i] = jnp.where(hit, 1.0, sel[i])
            vals[i] = jnp.where(hit, NEG_INF, vals[i])

    cmax = orig[1][0:1, :]
    z = None
    for o, sl in zip(orig, sel):
        part = jnp.sum(jnp.where(sl > 0.0, jnp.exp(o - cmax), 0.0), axis=0, keepdims=True)
        z = part if z is None else z + part
    cnt = [jnp.sum(sel[0] + sel[1], axis=0, keepdims=True)]
    cnt += [jnp.sum(sel[k1 + 1], axis=0, keepdims=True) for k1 in range(1, 8)]
    cnt += [sel[9][j:j + 1, :] for j in range(8)]

    cnt1 = jnp.zeros(r1.shape, F32)
    for k1 in range(PEER_TOPK):
        cnt1 = jnp.where(r1 == float(k1), cnt[k1], cnt1)
    return cnt1, r2, sv1[0:1, :], sv2[0:1, :], z


ROUTE_HEADS_PER_STEP = 4


def _route_kernel(xnt_ref, wqt_ref, k1_ref, k2_ref, cnt1_ref, g1_ref, r2_ref, e2_ref, s_ref):
    qt_all = _dot(wqt_ref[...], xnt_ref[...]).astype(BF16)

    def emit_gates(hh, s1, s2, max1, max2, z):
        g1_ref[hh] = jnp.exp(s1 - max1) * (0.5 / z)
        e2_ref[hh] = jnp.exp(s2 - max2).astype(BF16)

    ties = []
    for hh in range(ROUTE_HEADS_PER_STEP):
        qt = qt_all[hh * 2 * PEER_HALF:(hh + 1) * 2 * PEER_HALF, :]
        s1 = _dot(k1_ref[hh], qt[0:PEER_HALF, :])
        s2 = _dot(k2_ref[hh], qt[PEER_HALF:, :])
        s_ref[hh, 0] = s1
        s_ref[hh, 1] = s2
        cnt1, r2, max1, max2, z, tie = _route_fast(s1, s2)
        for v in range(PEER_NKEYS // 8):
            cnt1_ref[hh, v * 8:(v + 1) * 8, :] = cnt1[v]
            r2_ref[hh, v * 8:(v + 1) * 8, :] = r2[v].astype(BF16)
        emit_gates(hh, s1, s2, max1, max2, z)
        ties.append(jnp.max(tie))

    for hh in range(ROUTE_HEADS_PER_STEP):
        @pl.when(ties[hh] > 0.0)
        def _(hh=hh):
            s1, s2 = s_ref[hh, 0], s_ref[hh, 1]
            cnt1, r2, max1, max2, z = _route_exact(s1, s2)
            cnt1_ref[hh] = cnt1
            r2_ref[hh] = r2.astype(BF16)
            emit_gates(hh, s1, s2, max1, max2, z)


def _route(xnt, wqt, keys):
    d, n = xnt.shape
    tb = _token_block(n, 256)
    hps = ROUTE_HEADS_PER_STEP
    per_head = pl.BlockSpec((hps, PEER_NKEYS, tb), lambda h, i: (h, 0, i))
    shape = lambda dt: jax.ShapeDtypeStruct((PEER_HEADS, PEER_NKEYS, n), dt)
    return pl.pallas_call(
        _route_kernel,
        grid=(PEER_HEADS // hps, n // tb),
        in_specs=[
            pl.BlockSpec((d, tb), lambda h, i: (0, i)),
            pl.BlockSpec((hps * 2 * PEER_HALF, d), lambda h, i: (h, 0)),
            pl.BlockSpec((hps, PEER_NKEYS, PEER_HALF), lambda h, i: (h, 0, 0)),
            pl.BlockSpec((hps, PEER_NKEYS, PEER_HALF), lambda h, i: (h + PEER_HEADS // hps, 0, 0)),
        ],
        out_specs=[per_head] * 4,
        out_shape=[shape(F32), shape(F32), shape(BF16), shape(BF16)],
        scratch_shapes=[pltpu.VMEM((hps, 2, PEER_NKEYS, tb), F32)],
        compiler_params=_params(("parallel", "parallel")),
        name="route",
    )(xnt, wqt, keys, keys)


I1_PER_TILE = 16
BF16_TILE = (8, 2 * LANE)
EXPERT_TILE = I1_PER_TILE * PEER_NKEYS


def _experts_kernel(xnt_ref, cnt1_ref, g1_ref, r2_ref, e2_ref, u_ref, vt_ref, x1_ref,
                    x2_ref, acc_ref, act_ref, wa_ref):
    e = pl.program_id(1)
    tb = xnt_ref.shape[1]

    @pl.when(e == 0)
    def _():
        acc_ref[...] = jnp.zeros_like(acc_ref)

    a = _dot(u_ref[...], xnt_ref[...]).astype(BF16)
    act_ref[...] = a * (1.0 + jnp.tanh(a * (GELU_C0 + GELU_C1 * (a * a))))
    sub, width = BF16_TILE[0], min(BF16_TILE[1], tb)
    assert I1_PER_TILE % sub == 0
    i1_rows = pl.ds(pl.multiple_of(e * I1_PER_TILE, I1_PER_TILE), I1_PER_TILE)
    for lt in range(tb // width):
        lanes = pl.ds(lt * width, width)
        cnt_tile = [cnt1_ref[h, i1_rows, lanes] for h in range(PEER_HEADS)]
        g1_tile = [g1_ref[h, i1_rows, lanes] for h in range(PEER_HEADS)]
        for j in range(I1_PER_TILE):
            cnt_b = [jnp.broadcast_to(t[j:j + 1, :], (sub, width)).astype(BF16) for t in cnt_tile]
            g1_b = [jnp.broadcast_to(t[j:j + 1, :], (sub, width)).astype(BF16) for t in g1_tile]
            for rb in range(PEER_NKEYS // sub):
                krows = pl.ds(rb * sub, sub)
                w = None
                for h in range(PEER_HEADS):
                    mask = r2_ref[h, krows, lanes] < cnt_b[h]
                    term = jnp.where(mask, e2_ref[h, krows, lanes] * g1_b[h], jnp.zeros((), BF16))
                    w = term if w is None else w + term
                erows = pl.ds(j * PEER_NKEYS + rb * sub, sub)
                wa_ref[erows, lanes] = w * act_ref[erows, lanes]
    acc_ref[...] += _dot(vt_ref[...], wa_ref[...])

    @pl.when(e == pl.num_programs(1) - 1)
    def _():
        x2_ref[...] = x1_ref[...] + acc_ref[...].T


def _experts(xnt, cnt1, g1, r2, e2, u_tabs, vt_tabs, layer, x1):
    d, n = xnt.shape
    n_exp = u_tabs.shape[1]
    tb = _token_block(n, 512)
    assert tb % LANE == 0 and n_exp % EXPERT_TILE == 0
    per_head = pl.BlockSpec((PEER_HEADS, PEER_NKEYS, tb), lambda i, e: (0, 0, i))
    tile_buf = pltpu.VMEM((EXPERT_TILE, tb), BF16)
    return pl.pallas_call(
        _experts_kernel,
        grid=(n // tb, n_exp // EXPERT_TILE),
        in_specs=[
            pl.BlockSpec((d, tb), lambda i, e: (0, i)),
            per_head, per_head, per_head, per_head,
            pl.BlockSpec((None, EXPERT_TILE, d), lambda i, e: (layer, e, 0)),
            pl.BlockSpec((None, d, EXPERT_TILE), lambda i, e: (layer, 0, e)),
            pl.BlockSpec((tb, d), lambda i, e: (i, 0)),
        ],
        out_specs=pl.BlockSpec((tb, d), lambda i, e: (i, 0)),
        out_shape=jax.ShapeDtypeStruct((n, d), F32),
        scratch_shapes=[pltpu.VMEM((d, tb), F32), tile_buf, tile_buf],
        compiler_params=_params(("parallel", "arbitrary")),
        name="experts",
    )(xnt, cnt1, g1, r2, e2, u_tabs, vt_tabs, x1)


def _ple_kernel(x_ref, p_ref, wg_ref, wp_ref, nf_ref, y_ref, *, final_norm):
    x = x_ref[...]
    gate = _sigmoid(_dot(x.astype(BF16), wg_ref[...]))
    y = x + gate * _dot(p_ref[...].astype(BF16), wp_ref[...])
    if final_norm:
        y = _rmsnorm(y, nf_ref[...])
    y_ref[...] = y


def _ple(x, p_all, layer, w_gate, w_ple, norm_f, final_norm):
    n, d = x.shape
    tb = _token_block(n, 1024)
    rows = lambda i: (i, 0)
    const = lambda i: (0, 0)
    return pl.pallas_call(
        functools.partial(_ple_kernel, final_norm=final_norm),
        grid=(n // tb,),
        in_specs=[
            pl.BlockSpec((tb, d), rows),
            pl.BlockSpec((None, tb, p_all.shape[2]), lambda i: (layer, i, 0)),
            pl.BlockSpec(w_gate.shape, const),
            pl.BlockSpec(w_ple.shape, const),
            pl.BlockSpec((1, d), const),
        ],
        out_specs=pl.BlockSpec((tb, d), rows),
        out_shape=jax.ShapeDtypeStruct((n, d), F32),
        compiler_params=_params(("parallel",)),
        name="ple",
    )(x, p_all, w_gate, w_ple, norm_f)


def _prep_layer_weights(l, w_in, b_gate, mlstm_norm, w_pool, pool_scale, w_out, norm1, norm2,
                        peer_wq, peer_keys, peer_u, peer_v, w_ple, w_gate):
    w = MLSTM_W
    wi = w_in[l]
    d = wi.shape[0]
    wa = jnp.concatenate([wi[:, 0:w], wi[:, 2 * w:4 * w + POOL_W]], axis=1).astype(BF16)
    wkt = wi[:, w:2 * w].T.astype(BF16)
    wgt = wi[:, 4 * w + POOL_W:].T.astype(BF16)
    return dict(
        wa=wa, wkt=wkt, wgt=wgt,
        bg=b_gate[l].reshape(2 * HEADS, 1).astype(F32),
        mlstm_norm=mlstm_norm[l].reshape(1, w),
        w_pool=w_pool[l].astype(BF16),
        pool_scale=pool_scale[l].reshape(1, POOL_W),
        w_out=w_out[l].astype(BF16),
        norm1=norm1[l].reshape(1, d), norm2=norm2[l].reshape(1, d),
        wqt=peer_wq[l].T.astype(BF16),
        keys=peer_keys[l].reshape(2 * PEER_HEADS, PEER_NKEYS, PEER_HALF).astype(BF16),
        w_ple=w_ple[l].astype(BF16), w_gate=w_gate[l].astype(BF16),
        layer=l,
    )


def _ffn_and_embed(x1, xnt, p, lw, norm_f, final_norm):
    cnt1, g1, r2, e2 = _route(xnt, lw["wqt"], lw["keys"])
    x2 = _experts(xnt, cnt1, g1, r2, e2, lw["u_tabs"], lw["vt_tabs"], lw["layer"], x1)
    return _ple(x2, p, lw["layer"], lw["w_gate"], lw["w_ple"], norm_f, final_norm)


def _prompt_layer(x, p, lw, norm_f, final_norm, batch, t_len):
    q, kt, v, o, u, gt = _proj(x, lw["norm1"], lw["wa"], lw["wkt"], lw["wgt"], lw["bg"])
    gates3 = gt.reshape(2 * HEADS, (batch * t_len) // CHUNK, CHUNK)
    hm, cext, m8 = _mlstm(q, kt, v, o, gates3, lw["mlstm_norm"], batch, t_len)
    zp, buf = _pool(u, lw["w_pool"], lw["pool_scale"], batch, t_len)
    x1, xnt = _mix(x, hm, zp, lw["w_out"], lw["norm2"])
    y = _ffn_and_embed(x1, xnt, p, lw, norm_f, final_norm)
    return y, cext[..., :HEAD_DIM], cext[..., HEAD_DIM], m8[..., 0, 0], buf


def _sample_layer(x, p, c_all, c_stack, n0, m0, buf0, lw, norm_f, final_norm):
    q, kt, v, o, u, gt = _proj(x, lw["norm1"], lw["wa"], lw["wkt"], lw["wgt"], lw["bg"])
    hm, c1, n1, m1 = _mlstm1(q, kt.T, v, o, gt.T, c_all, lw["layer"], c_stack,
                             n0.reshape(-1, MLSTM_W), m0, lw["mlstm_norm"])
    n1 = n1.reshape(n0.shape)
    zp, buf_t = _pool1(u, jnp.swapaxes(buf0, 0, 1), lw["w_pool"], lw["pool_scale"])
    x1, xnt = _mix(x, hm, zp, lw["w_out"], lw["norm2"])
    y = _ffn_and_embed(x1, xnt, p, lw, norm_f, final_norm)
    return y, c1, n1, m1, jnp.swapaxes(buf_t, 0, 1)


def kernel(x_prompt, x_sample, p_prompt, p_sample, state_C, state_n, state_m, state_pool,
           w_in, b_gate, mlstm_norm, w_pool, pool_scale, w_out, norm1, norm2,
           peer_wq, peer_keys, peer_u, peer_v, w_ple, w_gate, norm_f):
    depth = w_in.shape[0]
    batch, t_len, d = x_prompt.shape
    dec_batch, dec_len, _ = x_sample.shape
    assert dec_len == 1 and t_len % CHUNK == 0
    nf = norm_f.reshape(1, d)

    xp = x_prompt.reshape(batch * t_len, d)
    xs = x_sample.reshape(dec_batch, d)
    outs_p, outs_s = [], []
    c_stack = None
    u_tabs = peer_u.astype(BF16)
    vt_tabs = jnp.swapaxes(peer_v, 1, 2).astype(BF16)
    for l in range(depth):
        lw = _prep_layer_weights(l, w_in, b_gate, mlstm_norm, w_pool, pool_scale, w_out, norm1,
                                 norm2, peer_wq, peer_keys, peer_u, peer_v, w_ple, w_gate)
        lw.update(u_tabs=u_tabs, vt_tabs=vt_tabs)
        final = l == depth - 1
        pp = p_prompt.reshape(depth, batch * t_len, -1)
        ps = p_sample.reshape(depth, dec_batch, -1)
        xp, c_p, n_p, m_p, buf_p = _prompt_layer(xp, pp, lw, nf, final, batch, t_len)
        xs, c_stack, n_s, m_s, buf_s = _sample_layer(xs, ps, state_C, c_stack, state_n[l],
                                                     state_m[l], state_pool[l], lw, nf, final)
        outs_p.append((c_p, n_p, m_p, buf_p))
        outs_s.append((None, n_s, m_s, buf_s))

    stack = lambda outs, i: jnp.stack([o[i] for o in outs])
    return (xp.reshape(batch, t_len, d), xs.reshape(dec_batch, dec_len, d),
            stack(outs_p, 0), stack(outs_p, 1), stack(outs_p, 2), stack(outs_p, 3),
            c_stack, stack(outs_s, 1), stack(outs_s, 2), stack(outs_s, 3))
```
